```python
import math
import jax, jax.numpy as jnp
from jax import lax
import numpy as np

D_MODEL = 2048
BATCH = 8
SEQ = 2048
DEPTH = 4

CHUNK = 64
D_MIX = D_MODEL
D_S5 = D_MIX // 2
D_CONV = D_MIX - D_S5
S5_P = 16
S5_G = D_S5 // S5_P
S5_N = 64
CONV_HEADS = 8
CONV_W = 3
D_IN = D_S5 + 3 * D_CONV
D_FF = 5632
LN_EPS = 1e-5
RMS_EPS = 1e-6
DEEPNORM_ALPHA = (2.0 * DEPTH) ** 0.25
DEEPNORM_BETA = (8.0 * DEPTH) ** -0.25

kernel_name = "hybrid_s5_shortconv_macaron_deepnorm"


def layer_norm(x, g, b):
    xf = x.astype(jnp.float32)
    mu = jnp.mean(xf, axis=-1, keepdims=True)
    xc = xf - mu
    var = jnp.mean(xc * xc, axis=-1, keepdims=True)
    y = xc * lax.rsqrt(var + LN_EPS)
    return (y * g.astype(jnp.float32) + b.astype(jnp.float32)).astype(x.dtype)


def rms_norm(x, g):
    xf = x.astype(jnp.float32)
    y = xf * lax.rsqrt(jnp.mean(xf * xf, axis=-1, keepdims=True) + RMS_EPS)
    return (y * g.astype(jnp.float32)).astype(x.dtype)


def swiglu(x, w_gate, w_up, w_down):
    return (jax.nn.silu(x @ w_gate) * (x @ w_up)) @ w_down


def s5_mixer(u, lam_re, lam_im, log_dt, b_re, b_im, c_re, c_im, d):
    bsz, seq, _ = u.shape
    f32 = jnp.float32
    uf = u.astype(f32).reshape(bsz, seq, S5_G, S5_P)
    lre = lam_re.astype(f32)
    lim = lam_im.astype(f32)
    dt = jnp.exp(log_dt.astype(f32))[:, None]
    mag = jnp.exp(lre * dt)
    ang = lim * dt
    ab_re = mag * jnp.cos(ang)
    ab_im = mag * jnp.sin(ang)
    den = lre * lre + lim * lim
    nr = ab_re - 1.0
    ni = ab_im
    q_re = (nr * lre + ni * lim) / den
    q_im = (ni * lre - nr * lim) / den
    br = b_re.astype(f32)
    bi = b_im.astype(f32)
    bb_re = q_re[..., None] * br - q_im[..., None] * bi
    bb_im = q_re[..., None] * bi + q_im[..., None] * br
    bu_re = jnp.einsum('blgp,gnp->blgn', uf, bb_re)
    bu_im = jnp.einsum('blgp,gnp->blgn', uf, bb_im)
    a_re = jnp.broadcast_to(ab_re, bu_re.shape)
    a_im = jnp.broadcast_to(ab_im, bu_im.shape)

    def combine(e1, e2):
        a1r, a1i, b1r, b1i = e1
        a2r, a2i, b2r, b2i = e2
        return (a1r * a2r - a1i * a2i,
                a1r * a2i + a1i * a2r,
                a2r * b1r - a2i * b1i + b2r,
                a2r * b1i + a2i * b1r + b2i)

    _, _, s_re, s_im = lax.associative_scan(combine, (a_re, a_im, bu_re, bu_im), axis=1)
    y = (jnp.einsum('blgn,gpn->blgp', s_re, c_re.astype(f32))
         - jnp.einsum('blgn,gpn->blgp', s_im, c_im.astype(f32))
         + d.astype(f32) * uf)
    return y.reshape(bsz, seq, S5_G * S5_P).astype(u.dtype)


def causal_depthwise_conv(z, w, b):
    seq = z.shape[1]
    zp = jnp.pad(z, ((0, 0), (CONV_W - 1, 0), (0, 0)))
    out = b
    for k in range(CONV_W):
        out = out + w[k] * zp[:, k:k + seq]
    return out


def hybrid_mixer(x, w_in, lam_re, lam_im, log_dt, b_re, b_im, c_re, c_im, d,
                 w_glu, conv_w, conv_b, g_s5, g_conv, w_out):
    proj = x @ w_in
    u = proj[..., :D_S5]
    gate_b = proj[..., D_S5:D_S5 + D_CONV]
    gate_c = proj[..., D_S5 + D_CONV:D_S5 + 2 * D_CONV]
    h = proj[..., D_S5 + 2 * D_CONV:]
    y = jax.nn.gelu(s5_mixer(u, lam_re, lam_im, log_dt, b_re, b_im, c_re, c_im, d))
    y = y * jax.nn.sigmoid(y @ w_glu)
    y = rms_norm(y, g_s5)
    z = gate_b * causal_depthwise_conv(gate_c * h, conv_w, conv_b)
    z = rms_norm(z, g_conv)
    return jnp.concatenate([y, z], axis=-1) @ w_out


def _fwd_setup_inputs(seed: int = 0) -> dict:
    key = jax.random.key(seed)
    ks = jax.random.split(key, 32)
    f32 = jnp.float32

    def nrm(k, shape, scale):
        return jax.random.normal(k, shape, f32) * scale

    def gain(k, shape):
        return 1.0 + 0.02 * jax.random.normal(k, shape, f32)

    L = DEPTH
    n_idx = jnp.arange(S5_N, dtype=f32)
    inp = {}
    inp["x"] = jax.random.normal(ks[0], (BATCH, SEQ, D_MODEL), f32)
    inp["ffn1_gate"] = nrm(ks[1], (L, D_MODEL, D_FF), D_MODEL ** -0.5)
    inp["ffn1_up"] = nrm(ks[2], (L, D_MODEL, D_FF), D_MODEL ** -0.5)
    inp["ffn1_down"] = nrm(ks[3], (L, D_FF, D_MODEL), D_FF ** -0.5 * DEEPNORM_BETA)
    inp["ln1_g"] = gain(ks[4], (L, D_MODEL))
    inp["ln1_b"] = nrm(ks[5], (L, D_MODEL), 0.02)
    inp["w_in"] = nrm(ks[6], (L, D_MODEL, D_IN), D_MODEL ** -0.5)
    inp["s5_lam_re"] = -0.5 * jnp.exp(0.05 * jax.random.normal(ks[7], (L, S5_G, S5_N), f32))
    inp["s5_lam_im"] = math.pi * n_idx + 0.01 * jax.random.normal(ks[8], (L, S5_G, S5_N), f32)
    inp["s5_log_dt"] = jax.random.uniform(ks[9], (L, S5_G), f32, math.log(1e-3), math.log(1e-1))
    inp["s5_b_re"] = nrm(ks[10], (L, S5_G, S5_N, S5_P), (2.0 * S5_P) ** -0.5)
    inp["s5_b_im"] = nrm(ks[11], (L, S5_G, S5_N, S5_P), (2.0 * S5_P) ** -0.5)
    inp["s5_c_re"] = nrm(ks[12], (L, S5_G, S5_P, S5_N), (2.0 * S5_N) ** -0.5)
    inp["s5_c_im"] = nrm(ks[13], (L, S5_G, S5_P, S5_N), (2.0 * S5_N) ** -0.5)
    inp["s5_d"] = nrm(ks[14], (L, S5_G, S5_P), 1.0)
    inp["s5_w_glu"] = nrm(ks[15], (L, D_S5, D_S5), D_S5 ** -0.5)
    inp["conv_w"] = nrm(ks[16], (L, CONV_W, D_CONV), CONV_W ** -0.5)
    inp["conv_b"] = nrm(ks[17], (L, D_CONV), 0.02)
    inp["g_s5"] = gain(ks[18], (L, D_S5))
    inp["g_conv"] = gain(ks[19], (L, D_CONV))
    inp["w_out"] = nrm(ks[20], (L, D_MIX, D_MODEL), D_MIX ** -0.5 * DEEPNORM_BETA)
    inp["ln2_g"] = gain(ks[21], (L, D_MODEL))
    inp["ln2_b"] = nrm(ks[22], (L, D_MODEL), 0.02)
    inp["ffn2_gate"] = nrm(ks[23], (L, D_MODEL, D_FF), D_MODEL ** -0.5)
    inp["ffn2_up"] = nrm(ks[24], (L, D_MODEL, D_FF), D_MODEL ** -0.5)
    inp["ffn2_down"] = nrm(ks[25], (L, D_FF, D_MODEL), D_FF ** -0.5 * DEEPNORM_BETA)
    inp["ln3_g"] = gain(ks[26], (L, D_MODEL))
    inp["ln3_b"] = nrm(ks[27], (L, D_MODEL), 0.02)
    return inp


def _fwd_reference(x, ffn1_gate, ffn1_up, ffn1_down, ln1_g, ln1_b, w_in, s5_lam_re, s5_lam_im,
              s5_log_dt, s5_b_re, s5_b_im, s5_c_re, s5_c_im, s5_d, s5_w_glu, conv_w, conv_b,
              g_s5, g_conv, w_out, ln2_g, ln2_b, ffn2_gate, ffn2_up, ffn2_down, ln3_g, ln3_b):
    for l in range(DEPTH):
        x = layer_norm(DEEPNORM_ALPHA * x + 0.5 * swiglu(x, ffn1_gate[l], ffn1_up[l], ffn1_down[l]),
                       ln1_g[l], ln1_b[l])
        m = hybrid_mixer(x, w_in[l], s5_lam_re[l], s5_lam_im[l], s5_log_dt[l], s5_b_re[l],
                         s5_b_im[l], s5_c_re[l], s5_c_im[l], s5_d[l], s5_w_glu[l], conv_w[l],
                         conv_b[l], g_s5[l], g_conv[l], w_out[l])
        x = layer_norm(DEEPNORM_ALPHA * x + m, ln2_g[l], ln2_b[l])
        x = layer_norm(DEEPNORM_ALPHA * x + 0.5 * swiglu(x, ffn2_gate[l], ffn2_up[l], ffn2_down[l]),
                       ln3_g[l], ln3_b[l])
    return x


import jax as _jax
import jax.numpy as _jnp

TWIN_FORMAT = 'train_step'
FWD_PARAMS = ['x', 'ffn1_gate', 'ffn1_up', 'ffn1_down', 'ln1_g', 'ln1_b', 'w_in', 's5_lam_re', 's5_lam_im', 's5_log_dt', 's5_b_re', 's5_b_im', 's5_c_re', 's5_c_im', 's5_d', 's5_w_glu', 'conv_w', 'conv_b', 'g_s5', 'g_conv', 'w_out', 'ln2_g', 'ln2_b', 'ffn2_gate', 'ffn2_up', 'ffn2_down', 'ln3_g', 'ln3_b']
TWIN_WEIGHTS = ['ffn1_gate', 'ffn1_up', 'ffn1_down', 'ln1_g', 'ln1_b', 'w_in', 's5_lam_re', 's5_lam_im', 's5_log_dt', 's5_b_re', 's5_b_im', 's5_c_re', 's5_c_im', 's5_d', 's5_w_glu', 'conv_w', 'conv_b', 'g_s5', 'g_conv', 'w_out', 'ln2_g', 'ln2_b', 'ffn2_gate', 'ffn2_up', 'ffn2_down', 'ln3_g', 'ln3_b']
TWIN_DIFF_INPUT = 'x'
TWIN_INPUTS = ['x', 'ffn1_gate', 'ffn1_up', 'ffn1_down', 'ln1_g', 'ln1_b', 'w_in', 's5_lam_re', 's5_lam_im', 's5_log_dt', 's5_b_re', 's5_b_im', 's5_c_re', 's5_c_im', 's5_d', 's5_w_glu', 'conv_w', 'conv_b', 'g_s5', 'g_conv', 'w_out', 'ln2_g', 'ln2_b', 'ffn2_gate', 'ffn2_up', 'ffn2_down', 'ln3_g', 'ln3_b', 'loss_target', 'm_ffn1_gate', 'm_ffn1_up', 'm_ffn1_down', 'm_ln1_g', 'm_ln1_b', 'm_w_in', 'm_s5_lam_re', 'm_s5_lam_im', 'm_s5_log_dt', 'm_s5_b_re', 'm_s5_b_im', 'm_s5_c_re', 'm_s5_c_im', 'm_s5_d', 'm_s5_w_glu', 'm_conv_w', 'm_conv_b', 'm_g_s5', 'm_g_conv', 'm_w_out', 'm_ln2_g', 'm_ln2_b', 'm_ffn2_gate', 'm_ffn2_up', 'm_ffn2_down', 'm_ln3_g', 'm_ln3_b', 'v_ffn1_gate', 'v_ffn1_up', 'v_ffn1_down', 'v_ln1_g', 'v_ln1_b', 'v_w_in', 'v_s5_lam_re', 'v_s5_lam_im', 'v_s5_log_dt', 'v_s5_b_re', 'v_s5_b_im', 'v_s5_c_re', 'v_s5_c_im', 'v_s5_d', 'v_s5_w_glu', 'v_conv_w', 'v_conv_b', 'v_g_s5', 'v_g_conv', 'v_w_out', 'v_ln2_g', 'v_ln2_b', 'v_ffn2_gate', 'v_ffn2_up', 'v_ffn2_down', 'v_ln3_g', 'v_ln3_b']
TWIN_OUTPUTS = ['loss', 'grad_x', 'grad_ffn1_gate', 'grad_ffn1_up', 'grad_ffn1_down', 'grad_ln1_g', 'grad_ln1_b', 'grad_w_in', 'grad_s5_lam_re', 'grad_s5_lam_im', 'grad_s5_log_dt', 'grad_s5_b_re', 'grad_s5_b_im', 'grad_s5_c_re', 'grad_s5_c_im', 'grad_s5_d', 'grad_s5_w_glu', 'grad_conv_w', 'grad_conv_b', 'grad_g_s5', 'grad_g_conv', 'grad_w_out', 'grad_ln2_g', 'grad_ln2_b', 'grad_ffn2_gate', 'grad_ffn2_up', 'grad_ffn2_down', 'grad_ln3_g', 'grad_ln3_b', 'delta_ffn1_gate', 'delta_ffn1_up', 'delta_ffn1_down', 'delta_ln1_g', 'delta_ln1_b', 'delta_w_in', 'delta_s5_lam_re', 'delta_s5_lam_im', 'delta_s5_log_dt', 'delta_s5_b_re', 'delta_s5_b_im', 'delta_s5_c_re', 'delta_s5_c_im', 'delta_s5_d', 'delta_s5_w_glu', 'delta_conv_w', 'delta_conv_b', 'delta_g_s5', 'delta_g_conv', 'delta_w_out', 'delta_ln2_g', 'delta_ln2_b', 'delta_ffn2_gate', 'delta_ffn2_up', 'delta_ffn2_down', 'delta_ln3_g', 'delta_ln3_b', 'new_m_ffn1_gate', 'new_m_ffn1_up', 'new_m_ffn1_down', 'new_m_ln1_g', 'new_m_ln1_b', 'new_m_w_in', 'new_m_s5_lam_re', 'new_m_s5_lam_im', 'new_m_s5_log_dt', 'new_m_s5_b_re', 'new_m_s5_b_im', 'new_m_s5_c_re', 'new_m_s5_c_im', 'new_m_s5_d', 'new_m_s5_w_glu', 'new_m_conv_w', 'new_m_conv_b', 'new_m_g_s5', 'new_m_g_conv', 'new_m_w_out', 'new_m_ln2_g', 'new_m_ln2_b', 'new_m_ffn2_gate', 'new_m_ffn2_up', 'new_m_ffn2_down', 'new_m_ln3_g', 'new_m_ln3_b', 'new_v_ffn1_gate', 'new_v_ffn1_up', 'new_v_ffn1_down', 'new_v_ln1_g', 'new_v_ln1_b', 'new_v_w_in', 'new_v_s5_lam_re', 'new_v_s5_lam_im', 'new_v_s5_log_dt', 'new_v_s5_b_re', 'new_v_s5_b_im', 'new_v_s5_c_re', 'new_v_s5_c_im', 'new_v_s5_d', 'new_v_s5_w_glu', 'new_v_conv_w', 'new_v_conv_b', 'new_v_g_s5', 'new_v_g_conv', 'new_v_w_out', 'new_v_ln2_g', 'new_v_ln2_b', 'new_v_ffn2_gate', 'new_v_ffn2_up', 'new_v_ffn2_down', 'new_v_ln3_g', 'new_v_ln3_b']
TWIN_LEAF_KINDS = {'loss': 'loss', 'grad_x': 'grad_x', 'grad_ffn1_gate': 'grad_w', 'grad_ffn1_up': 'grad_w', 'grad_ffn1_down': 'grad_w', 'grad_ln1_g': 'grad_w', 'grad_ln1_b': 'grad_w', 'grad_w_in': 'grad_w', 'grad_s5_lam_re': 'grad_w', 'grad_s5_lam_im': 'grad_w', 'grad_s5_log_dt': 'grad_w', 'grad_s5_b_re': 'grad_w', 'grad_s5_b_im': 'grad_w', 'grad_s5_c_re': 'grad_w', 'grad_s5_c_im': 'grad_w', 'grad_s5_d': 'grad_w', 'grad_s5_w_glu': 'grad_w', 'grad_conv_w': 'grad_w', 'grad_conv_b': 'grad_w', 'grad_g_s5': 'grad_w', 'grad_g_conv': 'grad_w', 'grad_w_out': 'grad_w', 'grad_ln2_g': 'grad_w', 'grad_ln2_b': 'grad_w', 'grad_ffn2_gate': 'grad_w', 'grad_ffn2_up': 'grad_w', 'grad_ffn2_down': 'grad_w', 'grad_ln3_g': 'grad_w', 'grad_ln3_b': 'grad_w', 'delta_ffn1_gate': 'delta_w', 'delta_ffn1_up': 'delta_w', 'delta_ffn1_down': 'delta_w', 'delta_ln1_g': 'delta_w', 'delta_ln1_b': 'delta_w', 'delta_w_in': 'delta_w', 'delta_s5_lam_re': 'delta_w', 'delta_s5_lam_im': 'delta_w', 'delta_s5_log_dt': 'delta_w', 'delta_s5_b_re': 'delta_w', 'delta_s5_b_im': 'delta_w', 'delta_s5_c_re': 'delta_w', 'delta_s5_c_im': 'delta_w', 'delta_s5_d': 'delta_w', 'delta_s5_w_glu': 'delta_w', 'delta_conv_w': 'delta_w', 'delta_conv_b': 'delta_w', 'delta_g_s5': 'delta_w', 'delta_g_conv': 'delta_w', 'delta_w_out': 'delta_w', 'delta_ln2_g': 'delta_w', 'delta_ln2_b': 'delta_w', 'delta_ffn2_gate': 'delta_w', 'delta_ffn2_up': 'delta_w', 'delta_ffn2_down': 'delta_w', 'delta_ln3_g': 'delta_w', 'delta_ln3_b': 'delta_w', 'new_m_ffn1_gate': 'new_m', 'new_m_ffn1_up': 'new_m', 'new_m_ffn1_down': 'new_m', 'new_m_ln1_g': 'new_m', 'new_m_ln1_b': 'new_m', 'new_m_w_in': 'new_m', 'new_m_s5_lam_re': 'new_m', 'new_m_s5_lam_im': 'new_m', 'new_m_s5_log_dt': 'new_m', 'new_m_s5_b_re': 'new_m', 'new_m_s5_b_im': 'new_m', 'new_m_s5_c_re': 'new_m', 'new_m_s5_c_im': 'new_m', 'new_m_s5_d': 'new_m', 'new_m_s5_w_glu': 'new_m', 'new_m_conv_w': 'new_m', 'new_m_conv_b': 'new_m', 'new_m_g_s5': 'new_m', 'new_m_g_conv': 'new_m', 'new_m_w_out': 'new_m', 'new_m_ln2_g': 'new_m', 'new_m_ln2_b': 'new_m', 'new_m_ffn2_gate': 'new_m', 'new_m_ffn2_up': 'new_m', 'new_m_ffn2_down': 'new_m', 'new_m_ln3_g': 'new_m', 'new_m_ln3_b': 'new_m', 'new_v_ffn1_gate': 'new_v', 'new_v_ffn1_up': 'new_v', 'new_v_ffn1_down': 'new_v', 'new_v_ln1_g': 'new_v', 'new_v_ln1_b': 'new_v', 'new_v_w_in': 'new_v', 'new_v_s5_lam_re': 'new_v', 'new_v_s5_lam_im': 'new_v', 'new_v_s5_log_dt': 'new_v', 'new_v_s5_b_re': 'new_v', 'new_v_s5_b_im': 'new_v', 'new_v_s5_c_re': 'new_v', 'new_v_s5_c_im': 'new_v', 'new_v_s5_d': 'new_v', 'new_v_s5_w_glu': 'new_v', 'new_v_conv_w': 'new_v', 'new_v_conv_b': 'new_v', 'new_v_g_s5': 'new_v', 'new_v_g_conv': 'new_v', 'new_v_w_out': 'new_v', 'new_v_ln2_g': 'new_v', 'new_v_ln2_b': 'new_v', 'new_v_ffn2_gate': 'new_v', 'new_v_ffn2_up': 'new_v', 'new_v_ffn2_down': 'new_v', 'new_v_ln3_g': 'new_v', 'new_v_ln3_b': 'new_v'}


def _forward(args):
    return _fwd_reference(*[args[k] for k in FWD_PARAMS])


def _output_shape():
    out = _jax.eval_shape(lambda: _forward(_fwd_setup_inputs(0)))
    return out.shape, out.dtype

N_MICROBATCH = 1
ADAM_LR = 0.001
ADAM_B1 = 0.9
ADAM_B2 = 0.999
ADAM_EPS = 1e-08
ADAM_WD = 0.01
ADAM_STEP = 10
PER_EXAMPLE_BATCH_AXIS = {'x': 0, 'loss_target': 0}
SHARED_INPUTS = []
_WEIGHT_DTYPES = {'ffn1_gate': _jnp.float32, 'ffn1_up': _jnp.float32, 'ffn1_down': _jnp.float32, 'ln1_g': _jnp.float32, 'ln1_b': _jnp.float32, 'w_in': _jnp.float32, 's5_lam_re': _jnp.float32, 's5_lam_im': _jnp.float32, 's5_log_dt': _jnp.float32, 's5_b_re': _jnp.float32, 's5_b_im': _jnp.float32, 's5_c_re': _jnp.float32, 's5_c_im': _jnp.float32, 's5_d': _jnp.float32, 's5_w_glu': _jnp.float32, 'conv_w': _jnp.float32, 'conv_b': _jnp.float32, 'g_s5': _jnp.float32, 'g_conv': _jnp.float32, 'w_out': _jnp.float32, 'ln2_g': _jnp.float32, 'ln2_b': _jnp.float32, 'ffn2_gate': _jnp.float32, 'ffn2_up': _jnp.float32, 'ffn2_down': _jnp.float32, 'ln3_g': _jnp.float32, 'ln3_b': _jnp.float32}
MOMENT_SCALE = {'ffn1_gate': 3.171530e-03, 'ffn1_up': 3.076594e-03, 'ffn1_down': 1.212432e-02, 'ln1_g': 2.444859e-01, 'ln1_b': 1.393887e-01, 'w_in': 1.650860e-02, 's5_lam_re': 9.692830e-04, 's5_lam_im': 9.058576e-04, 's5_log_dt': 4.607758e-01, 's5_b_re': 5.865861e-04, 's5_b_im': 5.851448e-04, 's5_c_re': 1.173615e-03, 's5_c_im': 1.169883e-03, 's5_d': 2.197533e-02, 's5_w_glu': 5.025378e-03, 'conv_w': 1.639117e-02, 'conv_b': 1.643826e-02, 'g_s5': 2.032162e-02, 'g_conv': 1.619108e-02, 'w_out': 4.361368e-02, 'ln2_g': 2.620039e-01, 'ln2_b': 1.412773e-01, 'ffn2_gate': 3.066568e-03, 'ffn2_up': 2.980696e-03, 'ffn2_down': 1.175563e-02, 'ln3_g': 4.030561e+00, 'ln3_b': 6.655907e-01}


def _to_microbatches(a, axis):
    t = _jnp.moveaxis(a, axis, 0)
    t = t.reshape((N_MICROBATCH, t.shape[0] // N_MICROBATCH) + t.shape[1:])
    return _jnp.moveaxis(t, 1, axis + 1)


def setup_inputs(seed: int = 0) -> dict:
    inp = _fwd_setup_inputs(seed)
    key = _jax.random.fold_in(_jax.random.key(seed), 7919)
    shape, _ = _output_shape()
    out = dict(inp)
    out["loss_target"] = _jax.random.normal(_jax.random.fold_in(key, 0), shape, _jnp.float32)
    for i, name in enumerate(TWIN_WEIGHTS):
        w = inp[name].astype(_jnp.float32)
        if MOMENT_SCALE is None:
            s = _jnp.sqrt(_jnp.mean(_jnp.square(w)) + 1e-30)
        else:
            s = MOMENT_SCALE[name]
        km, kv = _jax.random.split(_jax.random.fold_in(key, i + 1))
        out[name] = w
        out["m_" + name] = s * _jax.random.normal(km, w.shape, _jnp.float32)
        out["v_" + name] = (s * s) * _jax.random.uniform(kv, w.shape, _jnp.float32, 0.5, 1.5)
    if N_MICROBATCH > 1:
        for name, axis in PER_EXAMPLE_BATCH_AXIS.items():
            out[name] = _to_microbatches(out[name], axis)
    return {'x': out['x'], 'ffn1_gate': out['ffn1_gate'], 'ffn1_up': out['ffn1_up'], 'ffn1_down': out['ffn1_down'], 'ln1_g': out['ln1_g'], 'ln1_b': out['ln1_b'], 'w_in': out['w_in'], 's5_lam_re': out['s5_lam_re'], 's5_lam_im': out['s5_lam_im'], 's5_log_dt': out['s5_log_dt'], 's5_b_re': out['s5_b_re'], 's5_b_im': out['s5_b_im'], 's5_c_re': out['s5_c_re'], 's5_c_im': out['s5_c_im'], 's5_d': out['s5_d'], 's5_w_glu': out['s5_w_glu'], 'conv_w': out['conv_w'], 'conv_b': out['conv_b'], 'g_s5': out['g_s5'], 'g_conv': out['g_conv'], 'w_out': out['w_out'], 'ln2_g': out['ln2_g'], 'ln2_b': out['ln2_b'], 'ffn2_gate': out['ffn2_gate'], 'ffn2_up': out['ffn2_up'], 'ffn2_down': out['ffn2_down'], 'ln3_g': out['ln3_g'], 'ln3_b': out['ln3_b'], 'loss_target': out['loss_target'], 'm_ffn1_gate': out['m_ffn1_gate'], 'm_ffn1_up': out['m_ffn1_up'], 'm_ffn1_down': out['m_ffn1_down'], 'm_ln1_g': out['m_ln1_g'], 'm_ln1_b': out['m_ln1_b'], 'm_w_in': out['m_w_in'], 'm_s5_lam_re': out['m_s5_lam_re'], 'm_s5_lam_im': out['m_s5_lam_im'], 'm_s5_log_dt': out['m_s5_log_dt'], 'm_s5_b_re': out['m_s5_b_re'], 'm_s5_b_im': out['m_s5_b_im'], 'm_s5_c_re': out['m_s5_c_re'], 'm_s5_c_im': out['m_s5_c_im'], 'm_s5_d': out['m_s5_d'], 'm_s5_w_glu': out['m_s5_w_glu'], 'm_conv_w': out['m_conv_w'], 'm_conv_b': out['m_conv_b'], 'm_g_s5': out['m_g_s5'], 'm_g_conv': out['m_g_conv'], 'm_w_out': out['m_w_out'], 'm_ln2_g': out['m_ln2_g'], 'm_ln2_b': out['m_ln2_b'], 'm_ffn2_gate': out['m_ffn2_gate'], 'm_ffn2_up': out['m_ffn2_up'], 'm_ffn2_down': out['m_ffn2_down'], 'm_ln3_g': out['m_ln3_g'], 'm_ln3_b': out['m_ln3_b'], 'v_ffn1_gate': out['v_ffn1_gate'], 'v_ffn1_up': out['v_ffn1_up'], 'v_ffn1_down': out['v_ffn1_down'], 'v_ln1_g': out['v_ln1_g'], 'v_ln1_b': out['v_ln1_b'], 'v_w_in': out['v_w_in'], 'v_s5_lam_re': out['v_s5_lam_re'], 'v_s5_lam_im': out['v_s5_lam_im'], 'v_s5_log_dt': out['v_s5_log_dt'], 'v_s5_b_re': out['v_s5_b_re'], 'v_s5_b_im': out['v_s5_b_im'], 'v_s5_c_re': out['v_s5_c_re'], 'v_s5_c_im': out['v_s5_c_im'], 'v_s5_d': out['v_s5_d'], 'v_s5_w_glu': out['v_s5_w_glu'], 'v_conv_w': out['v_conv_w'], 'v_conv_b': out['v_conv_b'], 'v_g_s5': out['v_g_s5'], 'v_g_conv': out['v_g_conv'], 'v_w_out': out['v_w_out'], 'v_ln2_g': out['v_ln2_g'], 'v_ln2_b': out['v_ln2_b'], 'v_ffn2_gate': out['v_ffn2_gate'], 'v_ffn2_up': out['v_ffn2_up'], 'v_ffn2_down': out['v_ffn2_down'], 'v_ln3_g': out['v_ln3_g'], 'v_ln3_b': out['v_ln3_b']}


def _loss(weights, diff, rest, loss_target):
    with _jax.named_scope("forward"):
        args = {**rest, TWIN_DIFF_INPUT: diff, **{k: w.astype(_WEIGHT_DTYPES[k]) for k, w in weights.items()}}
        y = _forward(args)
    with _jax.named_scope("loss_head"):
        err = _jnp.square(y.astype(_jnp.float32) - loss_target)
        return 0.5 * _jnp.sum(_jnp.mean(err, axis=-1)) if err.ndim else 0.5 * err


def _adamw(w, g, m, v):
    m = ADAM_B1 * m + (1.0 - ADAM_B1) * g
    v = ADAM_B2 * v + (1.0 - ADAM_B2) * _jnp.square(g)
    m_hat = m / (1.0 - ADAM_B1 ** ADAM_STEP)
    v_hat = v / (1.0 - ADAM_B2 ** ADAM_STEP)
    delta = -ADAM_LR * (m_hat / (_jnp.sqrt(v_hat) + ADAM_EPS) + ADAM_WD * w)
    return delta, m, v


def reference(x, ffn1_gate, ffn1_up, ffn1_down, ln1_g, ln1_b, w_in, s5_lam_re, s5_lam_im, s5_log_dt, s5_b_re, s5_b_im, s5_c_re, s5_c_im, s5_d, s5_w_glu, conv_w, conv_b, g_s5, g_conv, w_out, ln2_g, ln2_b, ffn2_gate, ffn2_up, ffn2_down, ln3_g, ln3_b, loss_target, m_ffn1_gate, m_ffn1_up, m_ffn1_down, m_ln1_g, m_ln1_b, m_w_in, m_s5_lam_re, m_s5_lam_im, m_s5_log_dt, m_s5_b_re, m_s5_b_im, m_s5_c_re, m_s5_c_im, m_s5_d, m_s5_w_glu, m_conv_w, m_conv_b, m_g_s5, m_g_conv, m_w_out, m_ln2_g, m_ln2_b, m_ffn2_gate, m_ffn2_up, m_ffn2_down, m_ln3_g, m_ln3_b, v_ffn1_gate, v_ffn1_up, v_ffn1_down, v_ln1_g, v_ln1_b, v_w_in, v_s5_lam_re, v_s5_lam_im, v_s5_log_dt, v_s5_b_re, v_s5_b_im, v_s5_c_re, v_s5_c_im, v_s5_d, v_s5_w_glu, v_conv_w, v_conv_b, v_g_s5, v_g_conv, v_w_out, v_ln2_g, v_ln2_b, v_ffn2_gate, v_ffn2_up, v_ffn2_down, v_ln3_g, v_ln3_b):
    given = dict(x=x, ffn1_gate=ffn1_gate, ffn1_up=ffn1_up, ffn1_down=ffn1_down, ln1_g=ln1_g, ln1_b=ln1_b, w_in=w_in, s5_lam_re=s5_lam_re, s5_lam_im=s5_lam_im, s5_log_dt=s5_log_dt, s5_b_re=s5_b_re, s5_b_im=s5_b_im, s5_c_re=s5_c_re, s5_c_im=s5_c_im, s5_d=s5_d, s5_w_glu=s5_w_glu, conv_w=conv_w, conv_b=conv_b, g_s5=g_s5, g_conv=g_conv, w_out=w_out, ln2_g=ln2_g, ln2_b=ln2_b, ffn2_gate=ffn2_gate, ffn2_up=ffn2_up, ffn2_down=ffn2_down, ln3_g=ln3_g, ln3_b=ln3_b, loss_target=loss_target, m_ffn1_gate=m_ffn1_gate, m_ffn1_up=m_ffn1_up, m_ffn1_down=m_ffn1_down, m_ln1_g=m_ln1_g, m_ln1_b=m_ln1_b, m_w_in=m_w_in, m_s5_lam_re=m_s5_lam_re, m_s5_lam_im=m_s5_lam_im, m_s5_log_dt=m_s5_log_dt, m_s5_b_re=m_s5_b_re, m_s5_b_im=m_s5_b_im, m_s5_c_re=m_s5_c_re, m_s5_c_im=m_s5_c_im, m_s5_d=m_s5_d, m_s5_w_glu=m_s5_w_glu, m_conv_w=m_conv_w, m_conv_b=m_conv_b, m_g_s5=m_g_s5, m_g_conv=m_g_conv, m_w_out=m_w_out, m_ln2_g=m_ln2_g, m_ln2_b=m_ln2_b, m_ffn2_gate=m_ffn2_gate, m_ffn2_up=m_ffn2_up, m_ffn2_down=m_ffn2_down, m_ln3_g=m_ln3_g, m_ln3_b=m_ln3_b, v_ffn1_gate=v_ffn1_gate, v_ffn1_up=v_ffn1_up, v_ffn1_down=v_ffn1_down, v_ln1_g=v_ln1_g, v_ln1_b=v_ln1_b, v_w_in=v_w_in, v_s5_lam_re=v_s5_lam_re, v_s5_lam_im=v_s5_lam_im, v_s5_log_dt=v_s5_log_dt, v_s5_b_re=v_s5_b_re, v_s5_b_im=v_s5_b_im, v_s5_c_re=v_s5_c_re, v_s5_c_im=v_s5_c_im, v_s5_d=v_s5_d, v_s5_w_glu=v_s5_w_glu, v_conv_w=v_conv_w, v_conv_b=v_conv_b, v_g_s5=v_g_s5, v_g_conv=v_g_conv, v_w_out=v_w_out, v_ln2_g=v_ln2_g, v_ln2_b=v_ln2_b, v_ffn2_gate=v_ffn2_gate, v_ffn2_up=v_ffn2_up, v_ffn2_down=v_ffn2_down, v_ln3_g=v_ln3_g, v_ln3_b=v_ln3_b)
    weights = {n: given[n] for n in TWIN_WEIGHTS}
    shared = {n: given[n] for n in SHARED_INPUTS}
    per_example = {n: given[n] for n in ['x']}
    grad_fn = _jax.value_and_grad(_loss, argnums=(0, 1))

    def one_microbatch(ex, loss_target):
        ex = dict(ex)
        diff = ex.pop(TWIN_DIFF_INPUT)
        return grad_fn(weights, diff, {**shared, **ex}, loss_target)

    if N_MICROBATCH == 1:
        loss, (grad_w, grad_x) = one_microbatch(per_example, given["loss_target"])
    else:
        def body(carry, xs):
            loss_sum, grad_sum = carry
            l_k, (gw_k, gx_k) = one_microbatch(xs[0], xs[1])
            with _jax.named_scope("update"):
                return (loss_sum + l_k, _jax.tree.map(_jnp.add, grad_sum, gw_k)), gx_k

        init = (_jnp.zeros((), _jnp.float32), _jax.tree.map(_jnp.zeros_like, weights))
        (loss, grad_w), grad_x = _jax.lax.scan(body, init, (per_example, given["loss_target"]))
    with _jax.named_scope("update"):
        delta_w, new_m, new_v = {}, {}, {}
        for n in TWIN_WEIGHTS:
            delta_w[n], new_m[n], new_v[n] = _adamw(weights[n], grad_w[n], given["m_" + n], given["v_" + n])
    return (loss, grad_x, *[grad_w[n] for n in TWIN_WEIGHTS], *[delta_w[n] for n in TWIN_WEIGHTS],
            *[new_m[n] for n in TWIN_WEIGHTS], *[new_v[n] for n in TWIN_WEIGHTS])
```

```python
import functools

import jax
import jax.numpy as jnp
from jax import lax
from jax.experimental import pallas as pl
from jax.experimental.pallas import tpu as pltpu

F32 = jnp.float32
BF16 = jnp.bfloat16
MESH = pl.DeviceIdType.MESH
HIGH = lax.Precision.HIGHEST

N_CHIPS = 4
N_DEV = 8
V7X_VMEM_LIMIT = 56 * 1024 * 1024
SUBLANES = 8
LANES = 128
S5_P = 16
S5_N = 64
S5_GB = 8
CONV_W = 3
LN_EPS = 1e-5
RMS_EPS = 1e-6
ADAM_LR = 0.001
ADAM_B1 = 0.9
ADAM_B2 = 0.999
ADAM_EPS = 1e-08
ADAM_WD = 0.01
ADAM_STEP = 10
GELU_K = 0.7978845608028654
GELU_C = 0.044715


def _params():
    return pltpu.CompilerParams(vmem_limit_bytes=V7X_VMEM_LIMIT)


def _tile(n, pref, mult=SUBLANES):
    best = None
    for t in range(mult, min(n, pref) + 1, mult):
        if n % t == 0:
            best = t
    return best if best is not None else n


def _mm(a, b, precision=None):
    return jnp.dot(a, b, preferred_element_type=F32, precision=precision)


def _mm_nt(a, b, precision=None):
    return lax.dot_general(a, b, (((1,), (1,)), ((), ())), preferred_element_type=F32, precision=precision)


def _mm_tn(a, b, precision=None):
    return lax.dot_general(a, b, (((0,), (0,)), ((), ())), preferred_element_type=F32, precision=precision)


def _sigmoid(x):
    return 1.0 / (1.0 + jnp.exp(-x))


def _gelu(x):
    return 0.5 * x * (1.0 + jnp.tanh(GELU_K * (x + GELU_C * x * x * x)))


def _gelu_grad(x):
    th = jnp.tanh(GELU_K * (x + GELU_C * x * x * x))
    return 0.5 * (1.0 + th) + 0.5 * x * (1.0 - th * th) * GELU_K * (1.0 + 3.0 * GELU_C * x * x)


def _layer_norm(r, g, b):
    mu = jnp.mean(r, axis=-1, keepdims=True)
    xc = r - mu
    rstd = lax.rsqrt(jnp.mean(xc * xc, axis=-1, keepdims=True) + LN_EPS)
    xhat = xc * rstd
    return xhat * g + b, xhat, rstd


def _layer_norm_bwd(dy, xhat, rstd, g):
    dxh = dy * g
    m1 = jnp.mean(dxh, axis=-1, keepdims=True)
    m2 = jnp.mean(dxh * xhat, axis=-1, keepdims=True)
    return rstd * (dxh - m1 - xhat * m2)


def _rms_inv(x):
    return lax.rsqrt(jnp.mean(x * x, axis=-1, keepdims=True) + RMS_EPS)


def _rms_bwd(dy, x, rinv, g):
    dxh = dy * g
    return rinv * dxh - x * (rinv * rinv * rinv) * jnp.mean(dxh * x, axis=-1, keepdims=True)


def _rowwise(name, fn, rows, bcast=(), outs=(), sums=(), prev=(), nxt=(), tm=256):
    rows = [r if isinstance(r, tuple) else (r, r.shape[1], 0) for r in rows]
    L = rows[0][0].shape[0]
    tm = _tile(L, tm)
    n = L // tm
    hb = tm // SUBLANES
    nh = L // SUBLANES
    nr, nb, npv, nnx, no, ns = len(rows), len(bcast), len(prev), len(nxt), len(outs), len(sums)

    def body(*refs):
        i = pl.program_id(0)
        k = 0
        R = [r[...] for r in refs[k:k + nr]]; k += nr
        B = [r[...] for r in refs[k:k + nb]]; k += nb
        P = [r[...] for r in refs[k:k + npv]]; k += npv
        N = [r[...] for r in refs[k:k + nnx]]; k += nnx
        o_refs = refs[k:k + no]; k += no
        s_refs = refs[k:k + ns]
        O, S = fn(i, n, R, B, P, N)
        for ref, val in zip(o_refs, O):
            ref[...] = val.astype(ref.dtype)
        if ns:
            @pl.when(i == 0)
            def _():
                for ref in s_refs:
                    ref[...] = jnp.zeros_like(ref)
            for ref, val in zip(s_refs, S):
                ref[...] += val

    in_specs = [pl.BlockSpec((tm, w), functools.partial(lambda i, cb: (i, cb), cb=cb)) for _, w, cb in rows]
    in_specs += [pl.BlockSpec(b.shape, lambda i: (0, 0)) for b in bcast]
    in_specs += [pl.BlockSpec((SUBLANES, rows[j][1]),
                              functools.partial(lambda i, cb: (jnp.maximum(i * hb - 1, 0), cb), cb=rows[j][2]))
                 for j in prev]
    in_specs += [pl.BlockSpec((SUBLANES, rows[j][1]),
                              functools.partial(lambda i, cb: (jnp.minimum((i + 1) * hb, nh - 1), cb), cb=rows[j][2]))
                 for j in nxt]
    out_specs = [pl.BlockSpec((tm, w), lambda i: (i, 0)) for w, _ in outs]
    out_specs += [pl.BlockSpec((1, w), lambda i: (0, 0)) for w in sums]
    out_shape = [jax.ShapeDtypeStruct((L, w), dt) for w, dt in outs]
    out_shape += [jax.ShapeDtypeStruct((1, w), F32) for w in sums]
    args = [r[0] for r in rows] + list(bcast) + [rows[j][0] for j in prev] + [rows[j][0] for j in nxt]
    return pl.pallas_call(body, name=name, grid=(n,), in_specs=in_specs, out_specs=out_specs,
                          out_shape=out_shape, compiler_params=_params())(*args)


def _mm_expand(name, a, ws, nt, epi, extras, outs, tm=256):
    L, K = a.shape
    tm = _tile(L, tm)
    nw, ne = len(ws), len(extras)
    co = ws[0].shape[1] if nt else ws[0].shape[2]

    def body(a_ref, *refs):
        av = a_ref[...]
        ps = [(_mm_nt if nt else _mm)(av, w[...]) for w in refs[:nw]]
        vals = epi(ps, [e[...] for e in refs[nw:nw + ne]])
        for ref, val in zip(refs[nw + ne:], vals):
            ref[...] = val.astype(ref.dtype)

    in_specs = [pl.BlockSpec((tm, K), lambda j, i: (i, 0))]
    in_specs += [pl.BlockSpec((None,) + w.shape[1:], lambda j, i: (j, 0, 0)) for w in ws]
    in_specs += [pl.BlockSpec((tm, co), lambda j, i: (i, j)) for _ in extras]
    out_specs = [pl.BlockSpec((tm, co), lambda j, i: (i, j)) for _ in outs]
    out_shape = [jax.ShapeDtypeStruct((L, N_CHIPS * co), dt) for dt in outs]
    return pl.pallas_call(body, name=name, grid=(N_CHIPS, L // tm), in_specs=in_specs, out_specs=out_specs,
                          out_shape=out_shape, compiler_params=_params())(a, *ws, *extras)


def _mm_contract(name, as_, ws, nt, epi, extras, bcast, outs, tm=256):
    L = as_[0].shape[0]
    tm = _tile(L, tm)
    na, ne, nb = len(as_), len(extras), len(bcast)
    cb = as_[0].shape[1] // N_CHIPS
    n_out = ws[0].shape[1] if nt else ws[0].shape[2]

    def body(*refs):
        k = pl.program_id(1)
        a_refs, w_refs = refs[:na], refs[na:2 * na]
        e_refs = refs[2 * na:2 * na + ne]
        b_refs = refs[2 * na + ne:2 * na + ne + nb]
        o_refs = refs[2 * na + ne + nb:-1]
        acc = refs[-1]

        @pl.when(k == 0)
        def _():
            acc[...] = jnp.zeros_like(acc)

        part = None
        for a_ref, w_ref in zip(a_refs, w_refs):
            p = (_mm_nt if nt else _mm)(a_ref[...], w_ref[...])
            part = p if part is None else part + p
        acc[...] += part

        @pl.when(k == N_CHIPS - 1)
        def _():
            vals = epi(acc[...], [e[...] for e in e_refs], [b[...] for b in b_refs])
            for ref, val in zip(o_refs, vals):
                ref[...] = val.astype(ref.dtype)

    in_specs = [pl.BlockSpec((tm, cb), lambda i, k: (i, k)) for _ in as_]
    in_specs += [pl.BlockSpec((None,) + w.shape[1:], lambda i, k: (k, 0, 0)) for w in ws]
    in_specs += [pl.BlockSpec((tm, e.shape[1]), lambda i, k: (i, 0)) for e in extras]
    in_specs += [pl.BlockSpec(b.shape, lambda i, k: (0, 0)) for b in bcast]
    out_specs = [pl.BlockSpec((tm, w), lambda i, k: (i, 0)) for w, _ in outs]
    out_shape = [jax.ShapeDtypeStruct((L, w), dt) for w, dt in outs]
    return pl.pallas_call(body, name=name, grid=(L // tm, N_CHIPS), in_specs=in_specs, out_specs=out_specs,
                          out_shape=out_shape, scratch_shapes=[pltpu.VMEM((tm, n_out), F32)],
                          compiler_params=_params())(*as_, *ws, *extras, *bcast)


def _wgrad_cols(name, a, b, tk=512):
    L, K = a.shape
    C = b.shape[1] // N_CHIPS
    tk = _tile(K, tk, LANES)

    def body(a_ref, b_ref, o_ref):
        o_ref[...] = _mm_tn(a_ref[...], b_ref[...]).astype(o_ref.dtype)

    return pl.pallas_call(
        body, name=name, grid=(N_CHIPS, K // tk),
        in_specs=[pl.BlockSpec((L, tk), lambda j, kb: (0, kb)), pl.BlockSpec((L, C), lambda j, kb: (0, j))],
        out_specs=pl.BlockSpec((None, tk, C), lambda j, kb: (j, kb, 0)),
        out_shape=jax.ShapeDtypeStruct((N_CHIPS, K, C), BF16), compiler_params=_params())(a, b)


def _wgrad_rows(name, a, b, tr=512):
    L, N = b.shape
    R = a.shape[1] // N_CHIPS
    tr = _tile(R, tr, LANES)
    nrb = R // tr

    def body(a_ref, b_ref, o_ref):
        o_ref[...] = _mm_tn(a_ref[...], b_ref[...]).astype(o_ref.dtype)

    return pl.pallas_call(
        body, name=name, grid=(N_CHIPS, nrb),
        in_specs=[pl.BlockSpec((L, tr), lambda j, rb: (0, j * nrb + rb)), pl.BlockSpec((L, N), lambda j, rb: (0, 0))],
        out_specs=pl.BlockSpec((None, tr, N), lambda j, rb: (j, rb, 0)),
        out_shape=jax.ShapeDtypeStruct((N_CHIPS, R, N), BF16), compiler_params=_params())(a, b)


def _bd_mm(name, pairs, nt, epi, extras, bex, outs, tm=256):
    L = pairs[0][0].shape[0]
    nblk = pairs[0][1].shape[0]
    tm = _tile(L, tm)
    npair, ne, nx = len(pairs), len(extras), len(bex)
    w0 = pairs[0][1]
    ca, co = (w0.shape[2], w0.shape[1]) if nt else (w0.shape[1], w0.shape[2])

    def body(*refs):
        acc = None
        for q in range(npair):
            p = (_mm_nt if nt else _mm)(refs[2 * q][...], refs[2 * q + 1][...], HIGH)
            acc = p if acc is None else acc + p
        k = 2 * npair
        vals = epi(acc, [e[...] for e in refs[k:k + ne]], [e[...] for e in refs[k + ne:k + ne + nx]])
        for ref, val in zip(refs[k + ne + nx:], vals):
            ref[...] = val.astype(ref.dtype)

    in_specs = []
    args = []
    for a, w in pairs:
        in_specs += [pl.BlockSpec((tm, ca), lambda i, g: (i, g)), pl.BlockSpec((None,) + w.shape[1:], lambda i, g: (g, 0, 0))]
        args += [a, w]
    in_specs += [pl.BlockSpec((tm, co), lambda i, g: (i, g)) for _ in extras]
    in_specs += [pl.BlockSpec((1, co), lambda i, g: (0, g)) for _ in bex]
    out_specs = [pl.BlockSpec((tm, co), lambda i, g: (i, g)) for _ in outs]
    out_shape = [jax.ShapeDtypeStruct((L, nblk * co), dt) for dt in outs]
    return pl.pallas_call(body, name=name, grid=(L // tm, nblk), in_specs=in_specs, out_specs=out_specs,
                          out_shape=out_shape, compiler_params=_params())(*args, *extras, *bex)


def _bd_wgrad(name, a, b, nblk):
    L = a.shape[0]
    ca, cb = a.shape[1] // nblk, b.shape[1] // nblk

    def body(a_ref, b_ref, o_ref):
        o_ref[...] = _mm_tn(a_ref[...], b_ref[...], HIGH)

    return pl.pallas_call(
        body, name=name, grid=(nblk,),
        in_specs=[pl.BlockSpec((L, ca), lambda g: (0, g)), pl.BlockSpec((L, cb), lambda g: (0, g))],
        out_specs=pl.BlockSpec((None, ca, cb), lambda g: (g, 0, 0)),
        out_shape=jax.ShapeDtypeStruct((nblk, ca, cb), F32), compiler_params=_params())(a, b)


def _cmul(ar, ai, br, bi):
    return ar * br - ai * bi, ar * bi + ai * br


def _scan(name, ar, ai, xr, xi, reverse=False, state=None, tc=256):
    L, S = xr.shape
    tc = _tile(S, tc, LANES)
    nt = L // SUBLANES
    with_da = state is not None

    def body(*refs):
        ar_ref, ai_ref, xr_ref, xi_ref = refs[:4]
        if with_da:
            sr_ref, si_ref, yr_ref, yi_ref, dar_ref, dai_ref = refs[4:]
        else:
            yr_ref, yi_ref = refs[4:]
        a1 = (ar_ref[...], ai_ref[...])
        pw = [a1]
        for _ in range(SUBLANES - 1):
            pw.append(_cmul(*pw[-1], *a1))
        row = lax.broadcasted_iota(jnp.int32, (SUBLANES, tc), 0)
        tr = jnp.zeros((SUBLANES, tc), F32)
        ti = jnp.zeros((SUBLANES, tc), F32)
        for t in range(SUBLANES):
            p = pw[SUBLANES - 1 - t] if reverse else pw[t]
            tr = jnp.where(row == t, p[0], tr)
            ti = jnp.where(row == t, p[1], ti)

        def step(n, carry):
            idx = (nt - 1 - n) if reverse else n
            rows = pl.ds(pl.multiple_of(idx * SUBLANES, SUBLANES), SUBLANES)
            vr, vi = xr_ref[rows, :], xi_ref[rows, :]
            for d in (1, 2, 4):
                pr, pi = pw[d - 1]
                if reverse:
                    qr, qi = pltpu.roll(vr, SUBLANES - d, 0), pltpu.roll(vi, SUBLANES - d, 0)
                    keep = row < SUBLANES - d
                else:
                    qr, qi = pltpu.roll(vr, d, 0), pltpu.roll(vi, d, 0)
                    keep = row >= d
                mr, mi = _cmul(pr, pi, qr, qi)
                vr = vr + jnp.where(keep, mr, 0.0)
                vi = vi + jnp.where(keep, mi, 0.0)
            cr, ci = carry[0], carry[1]
            mr, mi = _cmul(tr, ti, cr, ci)
            vr, vi = vr + mr, vi + mi
            yr_ref[rows, :] = vr
            yi_ref[rows, :] = vi
            edge = 0 if reverse else SUBLANES - 1
            new = (vr[edge:edge + 1, :], vi[edge:edge + 1, :])
            if not with_da:
                return new
            pidx = jnp.maximum(idx - 1, 0)
            prows = pl.ds(pl.multiple_of(pidx * SUBLANES, SUBLANES), SUBLANES)
            live = (idx > 0).astype(F32)
            s0r = sr_ref[prows, :][SUBLANES - 1:SUBLANES, :] * live
            s0i = si_ref[prows, :][SUBLANES - 1:SUBLANES, :] * live
            spr = jnp.where(row == 0, s0r, pltpu.roll(sr_ref[rows, :], 1, 0))
            spi = jnp.where(row == 0, s0i, pltpu.roll(si_ref[rows, :], 1, 0))
            return new + (carry[2] + vr * spr + vi * spi, carry[3] + vi * spr - vr * spi)

        zero = jnp.zeros((1, tc), F32)
        init = (zero, zero)
        if with_da:
            acc0 = jnp.zeros((SUBLANES, tc), F32)
            init = init + (acc0, acc0)
        fin = lax.fori_loop(0, nt, step, init)
        if with_da:
            dar_ref[...] = jnp.sum(fin[2], axis=0, keepdims=True)
            dai_ref[...] = jnp.sum(fin[3], axis=0, keepdims=True)

    col = pl.BlockSpec((L, tc), lambda j: (0, j))
    vec = pl.BlockSpec((1, tc), lambda j: (0, j))
    n_in = 6 if with_da else 4
    in_specs = [vec, vec] + [col] * (n_in - 2)
    out_specs = [col, col] + ([vec, vec] if with_da else [])
    out_shape = [jax.ShapeDtypeStruct((L, S), F32)] * 2 + ([jax.ShapeDtypeStruct((1, S), F32)] * 2 if with_da else [])
    args = (ar, ai, xr, xi) + (tuple(state) if with_da else ())
    return pl.pallas_call(body, name=name, grid=(S // tc,), in_specs=in_specs, out_specs=out_specs,
                          out_shape=out_shape, compiler_params=_params())(*args)


def _bd_build(w_gnp):
    G, N, P = w_gnp.shape
    nb = G // S5_GB
    eye = jnp.eye(S5_GB, dtype=F32)
    x = w_gnp.reshape(nb, S5_GB, N, P).transpose(0, 1, 3, 2)
    w = x[:, :, :, None, :] * eye[None, :, None, :, None]
    return w.reshape(nb, S5_GB * P, S5_GB * N)


def _bd_extract(w, G, N, P):
    nb = G // S5_GB
    eye = jnp.eye(S5_GB, dtype=F32)
    w5 = w.reshape(nb, S5_GB, P, S5_GB, N)
    d = jnp.sum(w5 * eye[None, :, None, :, None], axis=3)
    return d.transpose(0, 1, 3, 2).reshape(G, N, P)


def _s5_discretize(lam_re, lam_im, log_dt, b_re, b_im):
    dt = jnp.exp(log_dt)[:, None]
    mag = jnp.exp(lam_re * dt)
    ang = lam_im * dt
    ab_re = mag * jnp.cos(ang)
    ab_im = mag * jnp.sin(ang)
    den = lam_re * lam_re + lam_im * lam_im
    nr = ab_re - 1.0
    ni = ab_im
    q_re = (nr * lam_re + ni * lam_im) / den
    q_im = (ni * lam_re - nr * lam_im) / den
    bb_re = q_re[..., None] * b_re - q_im[..., None] * b_im
    bb_im = q_re[..., None] * b_im + q_im[..., None] * b_re
    return ab_re, ab_im, bb_re, bb_im


HBM_SPEC = pl.BlockSpec(memory_space=pl.ANY)


def _place():
    x, y, c = lax.axis_index("x"), lax.axis_index("y"), lax.axis_index("c")
    others = [(1 - x, y), (x, 1 - y), (1 - x, 1 - y)]
    return x, y, c, 2 * x + y, others


def _half(ref, h, axis):
    n = ref.shape[axis] // 2
    idx = [slice(None)] * len(ref.shape)
    idx[axis] = pl.ds(h * n, n)
    return ref.at[tuple(idx)]


def _all_gather_chips(name, shards):
    T = len(shards)

    def body(*refs):
        ins, outs = refs[:T], refs[T:2 * T]
        send, recv, loc = refs[2 * T:]
        x, y, c, me, others = _place()
        sib = (x, y, 1 - c)
        first, passed = [], []
        for t in range(T):
            own = pltpu.make_async_copy(ins[t], outs[t].at[me], loc.at[t])
            own.start()
            first.append(own)
        for t in range(T):
            for k, (ox, oy) in enumerate(others):
                cp = pltpu.make_async_remote_copy(
                    src_ref=_half(ins[t], c, 0), dst_ref=_half(outs[t].at[me], c, 0),
                    send_sem=send.at[t, k], recv_sem=recv.at[t, k], device_id=(ox, oy, c), device_id_type=MESH)
                cp.start()
                first.append(cp)
        for t in range(T):
            for k, (ox, oy) in enumerate(others):
                got = _half(outs[t].at[2 * ox + oy], c, 0)
                cp = pltpu.make_async_remote_copy(
                    src_ref=got, dst_ref=got, send_sem=send.at[t, 3 + k], recv_sem=recv.at[t, 3 + k],
                    device_id=sib, device_id_type=MESH)
                pltpu.make_async_remote_copy(src_ref=got, dst_ref=got, send_sem=send.at[t, k], recv_sem=recv.at[t, k],
                                             device_id=sib, device_id_type=MESH).wait_recv()
                cp.start()
                passed.append(cp)
        for t in range(T):
            for k, (ox, oy) in enumerate(others):
                theirs = _half(outs[t].at[2 * ox + oy], 1 - c, 0)
                pltpu.make_async_remote_copy(src_ref=theirs, dst_ref=theirs, send_sem=send.at[t, 3 + k],
                                             recv_sem=recv.at[t, 3 + k], device_id=sib, device_id_type=MESH).wait_recv()
        for cp in first[:T]:
            cp.wait()
        for cp in first[T:] + passed:
            cp.wait_send()

    return pl.pallas_call(
        body, name=name, in_specs=[HBM_SPEC] * T, out_specs=[HBM_SPEC] * T,
        out_shape=[jax.ShapeDtypeStruct((N_CHIPS,) + s.shape, s.dtype) for s in shards],
        scratch_shapes=[pltpu.SemaphoreType.DMA((T, 6)), pltpu.SemaphoreType.DMA((T, 6)), pltpu.SemaphoreType.DMA((T,))],
    )(*shards)


def _swap_halves(name, grads):
    T = len(grads)

    def body(*refs):
        ins, mine, theirs = refs[:T], refs[T:2 * T], refs[2 * T:3 * T]
        send, recv, loc = refs[3 * T:]
        x, y, c, me, others = _place()
        cps = []
        for t in range(T):
            own = pltpu.make_async_copy(_half(ins[t], c, 1), mine[t], loc.at[t])
            own.start()
            cp = pltpu.make_async_remote_copy(
                src_ref=_half(ins[t], 1 - c, 1), dst_ref=theirs[t], send_sem=send.at[t], recv_sem=recv.at[t],
                device_id=(x, y, 1 - c), device_id_type=MESH)
            cp.start()
            cps.append((own, cp))
        for own, cp in cps:
            own.wait()
            cp.wait()

    shp = [jax.ShapeDtypeStruct((g.shape[0], g.shape[1] // 2, g.shape[2]), g.dtype) for g in grads]
    return pl.pallas_call(
        body, name=name, in_specs=[HBM_SPEC] * T, out_specs=[HBM_SPEC] * (2 * T), out_shape=shp + shp,
        scratch_shapes=[pltpu.SemaphoreType.DMA((T,)), pltpu.SemaphoreType.DMA((T,)), pltpu.SemaphoreType.DMA((T,))],
    )(*grads)


def _exchange_chips(name, parts):
    T = len(parts)

    def body(*refs):
        ins, outs = refs[:T], refs[T:2 * T]
        send, recv, loc = refs[2 * T:]
        x, y, c, me, others = _place()
        cps = []
        for t in range(T):
            own = pltpu.make_async_copy(ins[t].at[me], outs[t].at[me], loc.at[t])
            own.start()
            cps.append(own)
        for t in range(T):
            for k, (ox, oy) in enumerate(others):
                cp = pltpu.make_async_remote_copy(
                    src_ref=ins[t].at[2 * ox + oy], dst_ref=outs[t].at[me], send_sem=send.at[t, k],
                    recv_sem=recv.at[t, k], device_id=(ox, oy, c), device_id_type=MESH)
                cp.start()
                cps.append(cp)
        for cp in cps:
            cp.wait()

    return pl.pallas_call(
        body, name=name, in_specs=[HBM_SPEC] * T, out_specs=[HBM_SPEC] * T,
        out_shape=[jax.ShapeDtypeStruct(p.shape, p.dtype) for p in parts],
        scratch_shapes=[pltpu.SemaphoreType.DMA((T, 3)), pltpu.SemaphoreType.DMA((T, 3)), pltpu.SemaphoreType.DMA((T,))],
    )(*parts)


def _join_halves(name, pieces):
    T = len(pieces)

    def body(*refs):
        ins, outs = refs[:T], refs[T:2 * T]
        send, recv, loc = refs[2 * T:]
        x, y, c, me, others = _place()
        cps = []
        for t in range(T):
            own = pltpu.make_async_copy(ins[t], _half(outs[t], c, 0), loc.at[t])
            own.start()
            cp = pltpu.make_async_remote_copy(
                src_ref=ins[t], dst_ref=_half(outs[t], c, 0), send_sem=send.at[t], recv_sem=recv.at[t],
                device_id=(x, y, 1 - c), device_id_type=MESH)
            cp.start()
            cps.append((own, cp))
        for t, (own, cp) in enumerate(cps):
            own.wait()
            cp.wait_send()
            got = _half(outs[t], 1 - c, 0)
            pltpu.make_async_remote_copy(src_ref=got, dst_ref=got, send_sem=send.at[t], recv_sem=recv.at[t],
                                         device_id=(x, y, 1 - c), device_id_type=MESH).wait_recv()

    return pl.pallas_call(
        body, name=name, in_specs=[HBM_SPEC] * T, out_specs=[HBM_SPEC] * T,
        out_shape=[jax.ShapeDtypeStruct((2 * p.shape[0], p.shape[1]), p.dtype) for p in pieces],
        scratch_shapes=[pltpu.SemaphoreType.DMA((T,)), pltpu.SemaphoreType.DMA((T,)), pltpu.SemaphoreType.DMA((T,))],
    )(*pieces)


def _all_reduce_small(name, buf):
    _, r, w = buf.shape

    def body(in_ref, out_ref, land, send, recv):
        x, y, c, _, _ = _place()
        me = 4 * x + 2 * y + c

        def peer(k):
            return (1 - x if k & 4 else x, 1 - y if k & 2 else y, 1 - c if k & 1 else c)

        cps = []
        for k in range(1, N_DEV):
            px, py, pc = peer(k)
            cp = pltpu.make_async_remote_copy(
                src_ref=in_ref.at[4 * px + 2 * py + pc], dst_ref=land.at[me], send_sem=send.at[k - 1],
                recv_sem=recv.at[k - 1], device_id=(px, py, pc), device_id_type=MESH)
            cp.start()
            cps.append(cp)
        land[me] = in_ref[me]
        for cp in cps:
            cp.wait()
        total = land[0]
        for d in range(1, N_DEV):
            total = total + land[d]
        out_ref[me] = total
        cps = []
        for k in range(1, N_DEV):
            cp = pltpu.make_async_remote_copy(
                src_ref=out_ref.at[me], dst_ref=out_ref.at[me], send_sem=send.at[N_DEV - 2 + k],
                recv_sem=recv.at[N_DEV - 2 + k], device_id=peer(k), device_id_type=MESH)
            cp.start()
            cps.append(cp)
        for cp in cps:
            cp.wait()

    vm = pl.BlockSpec(memory_space=pltpu.VMEM)
    return pl.pallas_call(
        body, name=name, in_specs=[vm], out_specs=vm, out_shape=jax.ShapeDtypeStruct(buf.shape, F32),
        scratch_shapes=[pltpu.VMEM(buf.shape, F32), pltpu.SemaphoreType.DMA((2 * N_DEV - 2,)),
                        pltpu.SemaphoreType.DMA((2 * N_DEV - 2,))],
        compiler_params=_params())(buf)


def _add_pairs(name, a, b):
    nb, H, C = a.shape
    th = _tile(H, max(SUBLANES * 2, (1 << 19) // C), 16)

    def body(a_ref, b_ref, o_ref):
        o_ref[...] = (a_ref[...].astype(F32) + b_ref[...].astype(F32)).astype(o_ref.dtype)

    spec = pl.BlockSpec((None, th, C), lambda j, i: (j, i, 0))
    return pl.pallas_call(body, name=name, grid=(nb, H // th), in_specs=[spec, spec], out_specs=spec,
                          out_shape=jax.ShapeDtypeStruct(a.shape, a.dtype), compiler_params=_params())(a, b)


def _sum_chips(name, parts):
    nb, H, C = parts.shape
    th = _tile(H, max(SUBLANES * 2, (1 << 19) // C), 16)

    def body(p_ref, o_ref):
        tot = p_ref[0].astype(F32)
        for b in range(1, nb):
            tot = tot + p_ref[b].astype(F32)
        o_ref[...] = tot

    return pl.pallas_call(body, name=name, grid=(H // th,),
                          in_specs=[pl.BlockSpec((nb, th, C), lambda i: (0, i, 0))],
                          out_specs=pl.BlockSpec((th, C), lambda i: (i, 0)),
                          out_shape=jax.ShapeDtypeStruct((H, C), F32), compiler_params=_params())(parts)


def _adamw_math(w, g, m, v):
    m = ADAM_B1 * m + (1.0 - ADAM_B1) * g
    v = ADAM_B2 * v + (1.0 - ADAM_B2) * (g * g)
    m_hat = m / (1.0 - ADAM_B1 ** ADAM_STEP)
    v_hat = v / (1.0 - ADAM_B2 ** ADAM_STEP)
    delta = -ADAM_LR * (m_hat / (jnp.sqrt(v_hat) + ADAM_EPS) + ADAM_WD * w)
    return delta, m, v


def _adamw_big(name, w, m, v, gs):
    depth, R, C = w.shape
    tr = _tile(R, max(SUBLANES, (1 << 18) // C))

    def body(w_ref, m_ref, v_ref, *refs):
        g_refs, (go, do, mo, vo) = refs[:depth], refs[depth:]
        li = pl.program_id(0)
        g = g_refs[0][...]
        for l in range(1, depth):
            g = jnp.where(li == l, g_refs[l][...], g)
        delta, mn, vn = _adamw_math(w_ref[...], g, m_ref[...], v_ref[...])
        go[...] = g
        do[...] = delta
        mo[...] = mn
        vo[...] = vn

    spec = pl.BlockSpec((None, tr, C), lambda li, i: (li, i, 0))
    g_specs = [pl.BlockSpec((tr, C), functools.partial(lambda li, i, l: (jnp.where(li == l, i, 0), 0), l=l))
               for l in range(depth)]
    return pl.pallas_call(body, name=name, grid=(depth, R // tr), in_specs=[spec] * 3 + g_specs,
                          out_specs=[spec] * 4, out_shape=[jax.ShapeDtypeStruct(w.shape, F32)] * 4,
                          compiler_params=_params())(w, m, v, *gs)


def _adamw_small(name, w, g, m, v):
    def fn(i, n, R, B, P, N):
        return list(_adamw_math(*R)), []

    return _rowwise(name, fn, [w, g, m, v], outs=[(LANES, F32)] * 3)


def _pack(arrs, rows_mult):
    flat = jnp.concatenate([a.reshape(-1) for a in arrs])
    n = flat.shape[0]
    per = rows_mult * LANES
    pad = (-n) % per
    flat = jnp.pad(flat, (0, pad))
    return flat.reshape(-1, LANES), n


def _unpack(buf, like):
    flat = buf.reshape(-1)
    out, off = [], 0
    for a in like:
        out.append(flat[off:off + a.size].reshape(a.shape))
        off += a.size
    return out


def _ffn_fwd(tag, alpha, xf, xb, wg, wu, wd, g, b):
    def up(ps, _):
        hg, hu = ps
        return hg, hu, hg * _sigmoid(hg) * hu

    hg, hu, act = _mm_expand(f"{tag}_up", xb, [wg, wu], False, up, [], [F32, F32, BF16])

    def down(acc, ex, bc):
        y, xhat, rstd = _layer_norm(alpha * ex[0] + 0.5 * acc, bc[0], bc[1])
        return y, y, xhat, jnp.broadcast_to(rstd, (rstd.shape[0], LANES))

    D = xf.shape[1]
    yf, yb, xhat, rstd = _mm_contract(f"{tag}_down", [act], [wd], False, down, [xf], [g, b],
                                      [(D, F32), (D, BF16), (D, F32), (LANES, F32)])
    return yf, yb, (xb, hg, hu, act, xhat, rstd)


def _ln_bwd(name, dy, xhat, rstd, g, scale):
    D = dy.shape[1]

    def fn(i, n, R, B, P, N):
        d, xh, rs = R
        dr = _layer_norm_bwd(d, xh, rs[:, :1], B[0])
        return [dr, scale * dr], [jnp.sum(d * xh, axis=0, keepdims=True), jnp.sum(d, axis=0, keepdims=True)]

    return _rowwise(name, fn, [dy, xhat, rstd], bcast=[g], outs=[(D, F32), (D, BF16)], sums=[D, D])


def _ffn_bwd(tag, alpha, dy, saved, wg, wu, wd, g):
    xb, hg, hu, act, xhat, rstd = saved
    dr, dfb, dg, db = _ln_bwd(f"{tag}_ln_bwd", dy, xhat, rstd, g, 0.5)

    def dact(ps, ex):
        da, hgv, huv = ps[0], ex[0], ex[1]
        sg = _sigmoid(hgv)
        return da * huv * (sg * (1.0 + hgv * (1.0 - sg))), da * (hgv * sg)

    dhg, dhu = _mm_expand(f"{tag}_dact", dfb, [wd], True, dact, [hg, hu], [BF16, BF16])
    g_wd = _wgrad_rows(f"{tag}_gwd", act, dfb, tr=1408)
    g_wg = _wgrad_cols(f"{tag}_gwg", xb, dhg)
    g_wu = _wgrad_cols(f"{tag}_gwu", xb, dhu)

    def dxin(acc, ex, bc):
        return [alpha * ex[0] + acc]

    D = dy.shape[1]
    dx, = _mm_contract(f"{tag}_dx", [dhg, dhu], [wg, wu], True, dxin, [dr], [], [(D, F32)])
    return dx, (g_wg, g_wu, g_wd), dg, db


def _conv_taps(v, pv, i, w):
    tm = v.shape[0]
    ext = jnp.concatenate([pv * (i > 0).astype(F32), v], axis=0)
    v1 = pltpu.roll(ext, 1, 0)[SUBLANES:SUBLANES + tm]
    v2 = pltpu.roll(ext, 2, 0)[SUBLANES:SUBLANES + tm]
    return w[0:1] * v2 + w[1:2] * v1 + w[2:3] * v, v1, v2


def _mixer_fwd(tag, alpha, xf, xb, p, w_in, w_glu, w_out):
    Dh = w_in.shape[2]
    proj, = _mm_expand(f"{tag}_proj", xb, [w_in], False, lambda ps, _: ps, [], [F32])
    u, gb, gc, h = [(proj, Dh, j) for j in range(4)]
    nblk = p["wb_re"].shape[0]
    S = nblk * p["wb_re"].shape[2]
    ident = lambda acc, ex, bx: [acc]
    bu_re, = _bd_mm(f"{tag}_bu_re", [(proj, p["wb_re"])], False, ident, [], [], [F32])
    bu_im, = _bd_mm(f"{tag}_bu_im", [(proj, p["wb_im"])], False, ident, [], [], [F32])
    s_re, s_im = _scan(f"{tag}_scan", p["a_re"], p["a_im"], bu_re, bu_im)

    def yout(acc, ex, bx):
        ys = acc + bx[0] * ex[0]
        return ys, _gelu(ys), _gelu(ys)

    ys, yg, ygb = _bd_mm(f"{tag}_yout", [(s_re, p["wc_re"]), (s_im, p["wc_imn"])], False, yout, [proj], [p["d"]],
                         [F32, F32, BF16])

    def glu(acc, ex, bc):
        yy = ex[0] * _sigmoid(acc)
        return acc, yy * _rms_inv(yy) * bc[0]

    t, yn = _mm_contract(f"{tag}_glu", [ygb], [w_glu], False, glu, [yg], [p["g_s5"]], [(Dh, F32), (Dh, BF16)])

    def conv(i, n, R, B, P, N):
        gbv, gcv, hv = R
        cw, cb, gcn = B
        cv, _, _ = _conv_taps(gcv * hv, P[0] * P[1], i, cw)
        z = gbv * (cv + cb)
        return [z * _rms_inv(z) * gcn], []

    zn, = _rowwise(f"{tag}_conv", conv, [gb, gc, h], bcast=[p["conv_w"], p["conv_b"], p["g_conv"]],
                   outs=[(Dh, BF16)], prev=[1, 2])
    cat = jnp.concatenate([yn, zn], axis=1)

    def out(acc, ex, bc):
        y, xhat, rstd = _layer_norm(alpha * ex[0] + acc, bc[0], bc[1])
        return y, y, xhat, jnp.broadcast_to(rstd, (rstd.shape[0], LANES))

    D = xf.shape[1]
    yf, yb, xhat, rstd = _mm_contract(f"{tag}_out", [cat], [w_out], False, out, [xf], [p["ln2_g"], p["ln2_b"]],
                                      [(D, F32), (D, BF16), (D, F32), (LANES, F32)])
    return yf, yb, (xb, proj, s_re, s_im, ys, yg, ygb, t, cat, xhat, rstd)


def _mixer_bwd(tag, alpha, dy, saved, p, w_in, w_glu, w_out):
    xb, proj, s_re, s_im, ys, yg, ygb, t, cat, xhat, rstd = saved
    Dh = w_in.shape[2]
    D = dy.shape[1]
    gb, gc, h = [(proj, Dh, j) for j in range(1, 4)]
    dr, dmb, dg2, db2 = _ln_bwd(f"{tag}_ln_bwd", dy, xhat, rstd, p["ln2_g"], 1.0)
    dcat, = _mm_expand(f"{tag}_dcat", dmb, [w_out], True, lambda ps, _: ps, [], [F32])
    g_wout = _wgrad_rows(f"{tag}_gwout", cat, dmb)
    half = dcat.shape[1] // 2

    def conv_b1(i, n, R, B, P, N):
        dzn, gbv, gcv, hv = R
        cw, cb, gcn = B
        v = gcv * hv
        cv, v1, v2 = _conv_taps(v, P[0] * P[1], i, cw)
        cv = cv + cb
        z = gbv * cv
        rinv = _rms_inv(z)
        dz = _rms_bwd(dzn, z, rinv, gcn)
        dcv = dz * gbv
        col = lambda a: jnp.sum(a, axis=0, keepdims=True)
        return [dz * cv, dcv], [col(dzn * z * rinv), col(dcv), col(dcv * v2), col(dcv * v1), col(dcv * v)]

    dgb, dcv, dg_conv, dconv_b, dw0, dw1, dw2 = _rowwise(
        f"{tag}_conv_b1", conv_b1, [(dcat, half, 1), gb, gc, h], bcast=[p["conv_w"], p["conv_b"], p["g_conv"]],
        outs=[(Dh, BF16), (Dh, F32)], sums=[Dh] * 5, prev=[2, 3])

    def conv_b2(i, n, R, B, P, N):
        d, gcv, hv = R
        cw = B[0]
        tm = d.shape[0]
        ext = jnp.concatenate([d, N[0] * (i < n - 1).astype(F32)], axis=0)
        d1 = pltpu.roll(ext, tm + SUBLANES - 1, 0)[:tm]
        d2 = pltpu.roll(ext, tm + SUBLANES - 2, 0)[:tm]
        dv = cw[2:3] * d + cw[1:2] * d1 + cw[0:1] * d2
        return [dv * hv, dv * gcv], []

    dgc, dh = _rowwise(f"{tag}_conv_b2", conv_b2, [dcv, gc, h], bcast=[p["conv_w"]], outs=[(Dh, BF16)] * 2, nxt=[0])

    def glu_b(i, n, R, B, P, N):
        dyn, ygv, tv = R
        sg = _sigmoid(tv)
        yy = ygv * sg
        rinv = _rms_inv(yy)
        dyy = _rms_bwd(dyn, yy, rinv, B[0])
        return [dyy * ygv * sg * (1.0 - sg), dyy * sg], [jnp.sum(dyn * yy * rinv, axis=0, keepdims=True)]

    dtb, dyg0, dg_s5 = _rowwise(f"{tag}_glu_b", glu_b, [(dcat, half, 0), yg, t], bcast=[p["g_s5"]],
                                outs=[(Dh, BF16), (Dh, F32)], sums=[Dh])

    def dys_epi(ps, ex):
        return [(ps[0] + ex[0]) * _gelu_grad(ex[1])]

    dys, = _mm_expand(f"{tag}_dys", dtb, [w_glu], True, dys_epi, [dyg0, ys], [F32])
    g_wglu = _wgrad_rows(f"{tag}_gwglu", ygb, dtb)
    ident = lambda acc, ex, bx: [acc]
    ds_re, = _bd_mm(f"{tag}_ds_re", [(dys, p["wc_re"])], True, ident, [], [], [F32])
    ds_im, = _bd_mm(f"{tag}_ds_im", [(dys, p["wc_imn"])], True, ident, [], [], [F32])
    l_re, l_im, da_re, da_im = _scan(f"{tag}_rscan", p["a_re"], -p["a_im"], ds_re, ds_im, reverse=True,
                                     state=(s_re, s_im))

    def du_epi(acc, ex, bx):
        return [acc + bx[0] * ex[0]]

    du, = _bd_mm(f"{tag}_du", [(l_re, p["wb_re"]), (l_im, p["wb_im"])], True, du_epi, [dys], [p["d"]], [BF16])
    nblk = p["wb_re"].shape[0]
    u = (proj, Dh, 0)
    u_arr = proj[:, :Dh]
    g_wb_re = _bd_wgrad(f"{tag}_gwb_re", u_arr, l_re, nblk)
    g_wb_im = _bd_wgrad(f"{tag}_gwb_im", u_arr, l_im, nblk)
    g_wc_re = _bd_wgrad(f"{tag}_gwc_re", s_re, dys, nblk)
    g_wc_imn = _bd_wgrad(f"{tag}_gwc_im", s_im, dys, nblk)

    def dd_fn(i, n, R, B, P, N):
        return [], [jnp.sum(R[0] * R[1], axis=0, keepdims=True)]

    dd, = _rowwise(f"{tag}_dd", dd_fn, [dys, u], sums=[Dh])

    dproj = jnp.concatenate([du, dgb, dgc, dh], axis=1)

    def dxin(acc, ex, bc):
        return [alpha * ex[0] + acc]

    dx, = _mm_contract(f"{tag}_dx", [dproj], [w_in], True, dxin, [dr], [], [(D, F32)])
    g_win = _wgrad_cols(f"{tag}_gwin", xb, dproj)
    small = dict(ln2_g=dg2, ln2_b=db2, g_conv=dg_conv, conv_b=dconv_b,
                 conv_w=jnp.concatenate([dw0, dw1, dw2], axis=0), g_s5=dg_s5, d=dd,
                 da_re=da_re, da_im=da_im, wb_re=g_wb_re, wb_im=g_wb_im, wc_re=g_wc_re, wc_imn=g_wc_imn)
    return dx, (g_win, g_wglu, g_wout), small


def _reduce_layer(tag, grads):
    mine, theirs = _split2(_swap_halves(f"{tag}_swap", grads))
    parts = [_add_pairs(f"{tag}_add{t}", a, b) for t, (a, b) in enumerate(zip(mine, theirs))]
    got = _exchange_chips(f"{tag}_xchg", parts)
    pieces = [_sum_chips(f"{tag}_sum{t}", g) for t, g in enumerate(got)]
    return _join_halves(f"{tag}_join", pieces)


def _split2(xs):
    n = len(xs) // 2
    return xs[:n], xs[n:]


BIG = ["ffn1_gate", "ffn1_up", "ffn1_down", "w_in", "s5_w_glu", "w_out", "ffn2_gate", "ffn2_up", "ffn2_down"]
SMALL = ["ln1_g", "ln1_b", "s5_lam_re", "s5_lam_im", "s5_log_dt", "s5_b_re", "s5_b_im", "s5_c_re", "s5_c_im", "s5_d",
         "conv_w", "conv_b", "g_s5", "g_conv", "ln2_g", "ln2_b", "ln3_g", "ln3_b"]
WEIGHTS = ['ffn1_gate', 'ffn1_up', 'ffn1_down', 'ln1_g', 'ln1_b', 'w_in', 's5_lam_re', 's5_lam_im', 's5_log_dt',
           's5_b_re', 's5_b_im', 's5_c_re', 's5_c_im', 's5_d', 's5_w_glu', 'conv_w', 'conv_b', 'g_s5', 'g_conv',
           'w_out', 'ln2_g', 'ln2_b', 'ffn2_gate', 'ffn2_up', 'ffn2_down', 'ln3_g', 'ln3_b']


def _step(W, M, V, x, target):
    depth = W["ffn1_gate"].shape[0]
    alpha = (2.0 * depth) ** 0.25
    L, D = x.shape
    G, N = W["s5_lam_re"].shape[1:]
    P = W["s5_b_re"].shape[3]
    Dh = G * P
    chip = 2 * lax.axis_index("x") + lax.axis_index("y")
    cw_cols = W["conv_w"].shape[2]

    conv_w_blk = jnp.zeros((depth, CONV_W, N_CHIPS, cw_cols), F32)
    conv_w_blk = lax.dynamic_update_slice(conv_w_blk, W["conv_w"][:, :, None, :] * 0.5, (0, 0, chip, 0))
    pre, _ = _pack([conv_w_blk], SUBLANES * N_DEV)
    pre = _all_reduce_small("conv_w_gather", pre.reshape(N_DEV, -1, LANES))
    conv_w_full = pre.reshape(-1)[:conv_w_blk.size].reshape(depth, CONV_W, N_CHIPS * cw_cols)

    gathered = []
    for l in range(depth):
        gathered.append(_all_gather_chips(f"gather{l}", [W[n][l].astype(BF16) for n in BIG]))

    s5_vjps, lp = [], []
    for l in range(depth):
        (a_re, a_im, bb_re, bb_im), vjp = jax.vjp(_s5_discretize, W["s5_lam_re"][l], W["s5_lam_im"][l],
                                                  W["s5_log_dt"][l], W["s5_b_re"][l], W["s5_b_im"][l])
        s5_vjps.append(vjp)
        row = lambda a: a.reshape(1, -1)
        lp.append(dict(
            a_re=row(a_re), a_im=row(a_im), wb_re=_bd_build(bb_re), wb_im=_bd_build(bb_im),
            wc_re=_bd_build(W["s5_c_re"][l].transpose(0, 2, 1)).transpose(0, 2, 1),
            wc_imn=-_bd_build(W["s5_c_im"][l].transpose(0, 2, 1)).transpose(0, 2, 1),
            d=row(W["s5_d"][l]), g_s5=row(W["g_s5"][l]), g_conv=row(W["g_conv"][l]), conv_b=row(W["conv_b"][l]),
            conv_w=conv_w_full[l], ln2_g=row(W["ln2_g"][l]), ln2_b=row(W["ln2_b"][l])))

    def cast(i, n, R, B, P_, N_):
        return [R[0]], []

    xb, = _rowwise("cast_x", cast, [x], outs=[(D, BF16)])
    xf = x
    saved = []
    for l in range(depth):
        wg1, wu1, wd1, w_in, w_glu, w_out, wg2, wu2, wd2 = gathered[l]
        row = lambda a: a.reshape(1, -1)
        xf, xb, s1 = _ffn_fwd(f"l{l}_ffn1", alpha, xf, xb, wg1, wu1, wd1, row(W["ln1_g"][l]), row(W["ln1_b"][l]))
        xf, xb, s2 = _mixer_fwd(f"l{l}_mix", alpha, xf, xb, lp[l], w_in, w_glu, w_out)
        xf, xb, s3 = _ffn_fwd(f"l{l}_ffn2", alpha, xf, xb, wg2, wu2, wd2, row(W["ln3_g"][l]), row(W["ln3_b"][l]))
        saved.append((s1, s2, s3))

    def loss_fn(i, n, R, B, P_, N_):
        err = R[0] - R[1]
        return [err * (1.0 / D)], [jnp.sum(0.5 * err * err * (1.0 / D), axis=0, keepdims=True)]

    dx, loss_cols = _rowwise("loss", loss_fn, [xf, target], outs=[(D, F32)], sums=[D])
    loss = lax.psum(jnp.sum(loss_cols), ("x", "y", "c"))

    big_g = {n: [None] * depth for n in BIG}
    small_g = {n: [None] * depth for n in SMALL}
    for l in reversed(range(depth)):
        wg1, wu1, wd1, w_in, w_glu, w_out, wg2, wu2, wd2 = gathered[l]
        s1, s2, s3 = saved[l]
        row = lambda a: a.reshape(1, -1)
        dx, (g_wg2, g_wu2, g_wd2), dg3, db3 = _ffn_bwd(f"l{l}_ffn2", alpha, dx, s3, wg2, wu2, wd2, row(W["ln3_g"][l]))
        dx, (g_win, g_wglu, g_wout), sm = _mixer_bwd(f"l{l}_mix", alpha, dx, s2, lp[l], w_in, w_glu, w_out)
        dx, (g_wg1, g_wu1, g_wd1), dg1, db1 = _ffn_bwd(f"l{l}_ffn1", alpha, dx, s1, wg1, wu1, wd1, row(W["ln1_g"][l]))
        red = _reduce_layer(f"l{l}_red", [g_wg1, g_wu1, g_wd1, g_win, g_wglu, g_wout, g_wg2, g_wu2, g_wd2])
        for n, r in zip(BIG, red):
            big_g[n][l] = r
        d_bb_re = _bd_extract(sm["wb_re"], G, N, P)
        d_bb_im = _bd_extract(sm["wb_im"], G, N, P)
        d_lre, d_lim, d_ldt, d_bre, d_bim = s5_vjps[l]((sm["da_re"].reshape(G, N), sm["da_im"].reshape(G, N),
                                                        d_bb_re, d_bb_im))
        d_cre = _bd_extract(sm["wc_re"].transpose(0, 2, 1), G, N, P).transpose(0, 2, 1)
        d_cim = -_bd_extract(sm["wc_imn"].transpose(0, 2, 1), G, N, P).transpose(0, 2, 1)
        vals = dict(ln1_g=dg1, ln1_b=db1, s5_lam_re=d_lre, s5_lam_im=d_lim, s5_log_dt=d_ldt, s5_b_re=d_bre,
                    s5_b_im=d_bim, s5_c_re=d_cre, s5_c_im=d_cim, s5_d=sm["d"], conv_w=sm["conv_w"],
                    conv_b=sm["conv_b"], g_s5=sm["g_s5"], g_conv=sm["g_conv"], ln2_g=sm["ln2_g"], ln2_b=sm["ln2_b"],
                    ln3_g=dg3, ln3_b=db3)
        for n in SMALL:
            small_g[n][l] = vals[n].reshape((W[n].shape[1:] if n != "conv_w" else (CONV_W, N_CHIPS * cw_cols)))
    grad_x = dx

    sg = [jnp.stack(small_g[n]) for n in SMALL]
    packed, _ = _pack(sg, SUBLANES * N_DEV)
    packed = _all_reduce_small("small_allreduce", packed.reshape(N_DEV, -1, LANES)).reshape(-1, LANES)
    sg = dict(zip(SMALL, _unpack(packed, sg)))
    sg["conv_w"] = lax.dynamic_slice_in_dim(sg["conv_w"], chip * cw_cols, cw_cols, axis=2)
    gp, _ = _pack([sg[n] for n in SMALL], SUBLANES)
    wp, _ = _pack([W[n] for n in SMALL], SUBLANES)
    mp, _ = _pack([M[n] for n in SMALL], SUBLANES)
    vp, _ = _pack([V[n] for n in SMALL], SUBLANES)
    like = [W[n] for n in SMALL]
    dsm, msm, vsm = [dict(zip(SMALL, _unpack(a, like))) for a in _adamw_small("adamw_small", wp, gp, mp, vp)]

    grads, deltas, new_m, new_v = dict(sg), dsm, msm, vsm
    for n in BIG:
        shp = W[n].shape
        flat = lambda a: a.reshape(shp[0], shp[1], shp[2])
        gr, de, mn, vn = _adamw_big(f"adamw_{n}", flat(W[n]), flat(M[n]), flat(V[n]), big_g[n])
        grads[n], deltas[n], new_m[n], new_v[n] = gr, de, mn, vn

    outs = [loss, grad_x[None]]
    for d in (grads, deltas, new_m, new_v):
        outs += [d[n] for n in WEIGHTS]
    return tuple(outs)


def kernel(x, ffn1_gate, ffn1_up, ffn1_down, ln1_g, ln1_b, w_in, s5_lam_re, s5_lam_im, s5_log_dt, s5_b_re, s5_b_im, s5_c_re, s5_c_im, s5_d, s5_w_glu, conv_w, conv_b, g_s5, g_conv, w_out, ln2_g, ln2_b, ffn2_gate, ffn2_up, ffn2_down, ln3_g, ln3_b, loss_target, m_ffn1_gate, m_ffn1_up, m_ffn1_down, m_ln1_g, m_ln1_b, m_w_in, m_s5_lam_re, m_s5_lam_im, m_s5_log_dt, m_s5_b_re, m_s5_b_im, m_s5_c_re, m_s5_c_im, m_s5_d, m_s5_w_glu, m_conv_w, m_conv_b, m_g_s5, m_g_conv, m_w_out, m_ln2_g, m_ln2_b, m_ffn2_gate, m_ffn2_up, m_ffn2_down, m_ln3_g, m_ln3_b, v_ffn1_gate, v_ffn1_up, v_ffn1_down, v_ln1_g, v_ln1_b, v_w_in, v_s5_lam_re, v_s5_lam_im, v_s5_log_dt, v_s5_b_re, v_s5_b_im, v_s5_c_re, v_s5_c_im, v_s5_d, v_s5_w_glu, v_conv_w, v_conv_b, v_g_s5, v_g_conv, v_w_out, v_ln2_g, v_ln2_b, v_ffn2_gate, v_ffn2_up, v_ffn2_down, v_ln3_g, v_ln3_b):
    a = dict(locals())
    W = {n: a[n] for n in WEIGHTS}
    M = {n: a["m_" + n] for n in WEIGHTS}
    V = {n: a["v_" + n] for n in WEIGHTS}
    return _step(W, M, V, x[0], loss_target[0])
```

```python
import functools

import jax
import jax.numpy as jnp
from jax import lax
from jax.experimental import pallas as pl
from jax.experimental.pallas import tpu as pltpu

F32 = jnp.float32
BF16 = jnp.bfloat16
MESH = pl.DeviceIdType.MESH
HIGH = lax.Precision.HIGHEST

N_CHIPS = 4
N_DEV = 8
V7X_VMEM_LIMIT = 56 * 1024 * 1024
SUBLANES = 8
LANES = 128
S5_P = 16
S5_N = 64
S5_GB = 8
CONV_W = 3
LN_EPS = 1e-5
RMS_EPS = 1e-6
ADAM_LR = 0.001
ADAM_B1 = 0.9
ADAM_B2 = 0.999
ADAM_EPS = 1e-08
ADAM_WD = 0.01
ADAM_STEP = 10
GELU_K = 0.7978845608028654
GELU_C = 0.044715


def _params():
    return pltpu.CompilerParams(vmem_limit_bytes=V7X_VMEM_LIMIT)


def _tile(n, pref, mult=SUBLANES):
    best = None
    for t in range(mult, min(n, pref) + 1, mult):
        if n % t == 0:
            best = t
    return best if best is not None else n


def _mm(a, b, precision=None):
    return jnp.dot(a, b, preferred_element_type=F32, precision=precision)


def _mm_nt(a, b, precision=None):
    return lax.dot_general(a, b, (((1,), (1,)), ((), ())), preferred_element_type=F32, precision=precision)


def _mm_tn(a, b, precision=None):
    return lax.dot_general(a, b, (((0,), (0,)), ((), ())), preferred_element_type=F32, precision=precision)


def _sigmoid(x):
    return 1.0 / (1.0 + jnp.exp(-x))


def _gelu(x):
    return 0.5 * x * (1.0 + jnp.tanh(GELU_K * (x + GELU_C * x * x * x)))


def _gelu_grad(x):
    th = jnp.tanh(GELU_K * (x + GELU_C * x * x * x))
    return 0.5 * (1.0 + th) + 0.5 * x * (1.0 - th * th) * GELU_K * (1.0 + 3.0 * GELU_C * x * x)


def _layer_norm(r, g, b):
    mu = jnp.mean(r, axis=-1, keepdims=True)
    xc = r - mu
    rstd = lax.rsqrt(jnp.mean(xc * xc, axis=-1, keepdims=True) + LN_EPS)
    xhat = xc * rstd
    return xhat * g + b, xhat, rstd


def _layer_norm_bwd(dy, xhat, rstd, g):
    dxh = dy * g
    m1 = jnp.mean(dxh, axis=-1, keepdims=True)
    m2 = jnp.mean(dxh * xhat, axis=-1, keepdims=True)
    return rstd * (dxh - m1 - xhat * m2)


def _rms_inv(x):
    return lax.rsqrt(jnp.mean(x * x, axis=-1, keepdims=True) + RMS_EPS)


def _rms_bwd(dy, x, rinv, g):
    dxh = dy * g
    return rinv * dxh - x * (rinv * rinv * rinv) * jnp.mean(dxh * x, axis=-1, keepdims=True)


def _rowwise(name, fn, rows, bcast=(), outs=(), sums=(), prev=(), nxt=(), tm=256):
    rows = [r if isinstance(r, tuple) else (r, r.shape[1], 0) for r in rows]
    L = rows[0][0].shape[0]
    tm = _tile(L, tm)
    n = L // tm
    hb = tm // SUBLANES
    nh = L // SUBLANES
    nr, nb, npv, nnx, no, ns = len(rows), len(bcast), len(prev), len(nxt), len(outs), len(sums)

    def body(*refs):
        i = pl.program_id(0)
        k = 0
        R = [r[...] for r in refs[k:k + nr]]; k += nr
        B = [r[...] for r in refs[k:k + nb]]; k += nb
        P = [r[...] for r in refs[k:k + npv]]; k += npv
        N = [r[...] for r in refs[k:k + nnx]]; k += nnx
        o_refs = refs[k:k + no]; k += no
        s_refs = refs[k:k + ns]
        O, S = fn(i, n, R, B, P, N)
        for ref, val in zip(o_refs, O):
            ref[...] = val.astype(ref.dtype)
        if ns:
            @pl.when(i == 0)
            def _():
                for ref in s_refs:
                    ref[...] = jnp.zeros_like(ref)
            for ref, val in zip(s_refs, S):
                ref[...] += val

    in_specs = [pl.BlockSpec((tm, w), functools.partial(lambda i, cb: (i, cb), cb=cb)) for _, w, cb in rows]
    in_specs += [pl.BlockSpec(b.shape, lambda i: (0, 0)) for b in bcast]
    in_specs += [pl.BlockSpec((SUBLANES, rows[j][1]),
                              functools.partial(lambda i, cb: (jnp.maximum(i * hb - 1, 0), cb), cb=rows[j][2]))
                 for j in prev]
    in_specs += [pl.BlockSpec((SUBLANES, rows[j][1]),
                              functools.partial(lambda i, cb: (jnp.minimum((i + 1) * hb, nh - 1), cb), cb=rows[j][2]))
                 for j in nxt]
    out_specs = [pl.BlockSpec((tm, w), lambda i: (i, 0)) for w, _ in outs]
    out_specs += [pl.BlockSpec((1, w), lambda i: (0, 0)) for w in sums]
    out_shape = [jax.ShapeDtypeStruct((L, w), dt) for w, dt in outs]
    out_shape += [jax.ShapeDtypeStruct((1, w), F32) for w in sums]
    args = [r[0] for r in rows] + list(bcast) + [rows[j][0] for j in prev] + [rows[j][0] for j in nxt]
    return pl.pallas_call(body, name=name, grid=(n,), in_specs=in_specs, out_specs=out_specs,
                          out_shape=out_shape, compiler_params=_params())(*args)


def _mm_expand(name, a, ws, nt, epi, extras, outs, tm=256):
    L, K = a.shape
    tm = _tile(L, tm)
    nw, ne = len(ws), len(extras)
    co = ws[0].shape[1] if nt else ws[0].shape[2]

    def body(a_ref, *refs):
        av = a_ref[...]
        ps = [(_mm_nt if nt else _mm)(av, w[...]) for w in refs[:nw]]
        vals = epi(ps, [e[...] for e in refs[nw:nw + ne]])
        for ref, val in zip(refs[nw + ne:], vals):
            ref[...] = val.astype(ref.dtype)

    in_specs = [pl.BlockSpec((tm, K), lambda j, i: (i, 0))]
    in_specs += [pl.BlockSpec((None,) + w.shape[1:], lambda j, i: (j, 0, 0)) for w in ws]
    in_specs += [pl.BlockSpec((tm, co), lambda j, i: (i, j)) for _ in extras]
    out_specs = [pl.BlockSpec((tm, co), lambda j, i: (i, j)) for _ in outs]
    out_shape = [jax.ShapeDtypeStruct((L, N_CHIPS * co), dt) for dt in outs]
    return pl.pallas_call(body, name=name, grid=(N_CHIPS, L // tm), in_specs=in_specs, out_specs=out_specs,
                          out_shape=out_shape, compiler_params=_params())(a, *ws, *extras)


def _mm_contract(name, as_, ws, nt, epi, extras, bcast, outs, tm=256):
    L = as_[0].shape[0]
    tm = _tile(L, tm)
    na, ne, nb = len(as_), len(extras), len(bcast)
    cb = as_[0].shape[1] // N_CHIPS
    n_out = ws[0].shape[1] if nt else ws[0].shape[2]

    def body(*refs):
        k = pl.program_id(1)
        a_refs, w_refs = refs[:na], refs[na:2 * na]
        e_refs = refs[2 * na:2 * na + ne]
        b_refs = refs[2 * na + ne:2 * na + ne + nb]
        o_refs = refs[2 * na + ne + nb:-1]
        acc = refs[-1]

        @pl.when(k == 0)
        def _():
            acc[...] = jnp.zeros_like(acc)

        part = None
        for a_ref, w_ref in zip(a_refs, w_refs):
            p = (_mm_nt if nt else _mm)(a_ref[...], w_ref[...])
            part = p if part is None else part + p
        acc[...] += part

        @pl.when(k == N_CHIPS - 1)
        def _():
            vals = epi(acc[...], [e[...] for e in e_refs], [b[...] for b in b_refs])
            for ref, val in zip(o_refs, vals):
                ref[...] = val.astype(ref.dtype)

    in_specs = [pl.BlockSpec((tm, cb), lambda i, k: (i, k)) for _ in as_]
    in_specs += [pl.BlockSpec((None,) + w.shape[1:], lambda i, k: (k, 0, 0)) for w in ws]
    in_specs += [pl.BlockSpec((tm, e.shape[1]), lambda i, k: (i, 0)) for e in extras]
    in_specs += [pl.BlockSpec(b.shape, lambda i, k: (0, 0)) for b in bcast]
    out_specs = [pl.BlockSpec((tm, w), lambda i, k: (i, 0)) for w, _ in outs]
    out_shape = [jax.ShapeDtypeStruct((L, w), dt) for w, dt in outs]
    return pl.pallas_call(body, name=name, grid=(L // tm, N_CHIPS), in_specs=in_specs, out_specs=out_specs,
                          out_shape=out_shape, scratch_shapes=[pltpu.VMEM((tm, n_out), F32)],
                          compiler_params=_params())(*as_, *ws, *extras, *bcast)


def _wgrad_cols(name, a, b, tk=512):
    L, K = a.shape
    C = b.shape[1] // N_CHIPS
    tk = _tile(K, tk, LANES)

    def body(a_ref, b_ref, o_ref):
        o_ref[...] = _mm_tn(a_ref[...], b_ref[...]).astype(o_ref.dtype)

    return pl.pallas_call(
        body, name=name, grid=(N_CHIPS, K // tk),
        in_specs=[pl.BlockSpec((L, tk), lambda j, kb: (0, kb)), pl.BlockSpec((L, C), lambda j, kb: (0, j))],
        out_specs=pl.BlockSpec((None, tk, C), lambda j, kb: (j, kb, 0)),
        out_shape=jax.ShapeDtypeStruct((N_CHIPS, K, C), BF16), compiler_params=_params())(a, b)


def _wgrad_rows(name, a, b, tr=512):
    L, N = b.shape
    R = a.shape[1] // N_CHIPS
    tr = _tile(R, tr, LANES)
    nrb = R // tr

    def body(a_ref, b_ref, o_ref):
        o_ref[...] = _mm_tn(a_ref[...], b_ref[...]).astype(o_ref.dtype)

    return pl.pallas_call(
        body, name=name, grid=(N_CHIPS, nrb),
        in_specs=[pl.BlockSpec((L, tr), lambda j, rb: (0, j * nrb + rb)), pl.BlockSpec((L, N), lambda j, rb: (0, 0))],
        out_specs=pl.BlockSpec((None, tr, N), lambda j, rb: (j, rb, 0)),
        out_shape=jax.ShapeDtypeStruct((N_CHIPS, R, N), BF16), compiler_params=_params())(a, b)


def _bd_mm(name, pairs, nt, epi, extras, bex, outs, tm=256):
    L = pairs[0][0].shape[0]
    nblk = pairs[0][1].shape[0]
    tm = _tile(L, tm)
    npair, ne, nx = len(pairs), len(extras), len(bex)
    w0 = pairs[0][1]
    ca, co = (w0.shape[2], w0.shape[1]) if nt else (w0.shape[1], w0.shape[2])

    def body(*refs):
        acc = None
        for q in range(npair):
            p = (_mm_nt if nt else _mm)(refs[2 * q][...], refs[2 * q + 1][...], HIGH)
            acc = p if acc is None else acc + p
        k = 2 * npair
        vals = epi(acc, [e[...] for e in refs[k:k + ne]], [e[...] for e in refs[k + ne:k + ne + nx]])
        for ref, val in zip(refs[k + ne + nx:], vals):
            ref[...] = val.astype(ref.dtype)

    in_specs = []
    args = []
    for a, w in pairs:
        in_specs += [pl.BlockSpec((tm, ca), lambda i, g: (i, g)), pl.BlockSpec((None,) + w.shape[1:], lambda i, g: (g, 0, 0))]
        args += [a, w]
    in_specs += [pl.BlockSpec((tm, co), lambda i, g: (i, g)) for _ in extras]
    in_specs += [pl.BlockSpec((1, co), lambda i, g: (0, g)) for _ in bex]
    out_specs = [pl.BlockSpec((tm, co), lambda i, g: (i, g)) for _ in outs]
    out_shape = [jax.ShapeDtypeStruct((L, nblk * co), dt) for dt in outs]
    return pl.pallas_call(body, name=name, grid=(L // tm, nblk), in_specs=in_specs, out_specs=out_specs,
                          out_shape=out_shape, compiler_params=_params())(*args, *extras, *bex)


def _bd_wgrad(name, a, b, nblk):
    L = a.shape[0]
    ca, cb = a.shape[1] // nblk, b.shape[1] // nblk

    def body(a_ref, b_ref, o_ref):
        o_ref[...] = _mm_tn(a_ref[...], b_ref[...], HIGH)

    return pl.pallas_call(
        body, name=name, grid=(nblk,),
        in_specs=[pl.BlockSpec((L, ca), lambda g: (0, g)), pl.BlockSpec((L, cb), lambda g: (0, g))],
        out_specs=pl.BlockSpec((None, ca, cb), lambda g: (g, 0, 0)),
        out_shape=jax.ShapeDtypeStruct((nblk, ca, cb), F32), compiler_params=_params())(a, b)


def _cmul(ar, ai, br, bi):
    return ar * br - ai * bi, ar * bi + ai * br


def _scan(name, ar, ai, xr, xi, reverse=False, state=None, tc=256):
    L, S = xr.shape
    tc = _tile(S, tc, LANES)
    nt = L // SUBLANES
    with_da = state is not None

    def body(*refs):
        ar_ref, ai_ref, xr_ref, xi_ref = refs[:4]
        if with_da:
            sr_ref, si_ref, yr_ref, yi_ref, dar_ref, dai_ref = refs[4:]
        else:
            yr_ref, yi_ref = refs[4:]
        a1 = (ar_ref[...], ai_ref[...])
        pw = [a1]
        for _ in range(SUBLANES - 1):
            pw.append(_cmul(*pw[-1], *a1))
        row = lax.broadcasted_iota(jnp.int32, (SUBLANES, tc), 0)
        tr = jnp.zeros((SUBLANES, tc), F32)
        ti = jnp.zeros((SUBLANES, tc), F32)
        for t in range(SUBLANES):
            p = pw[SUBLANES - 1 - t] if reverse else pw[t]
            tr = jnp.where(row == t, p[0], tr)
            ti = jnp.where(row == t, p[1], ti)

        def step(n, carry):
            idx = (nt - 1 - n) if reverse else n
            rows = pl.ds(pl.multiple_of(idx * SUBLANES, SUBLANES), SUBLANES)
            vr, vi = xr_ref[rows, :], xi_ref[rows, :]
            for d in (1, 2, 4):
                pr, pi = pw[d - 1]
                if reverse:
                    qr, qi = pltpu.roll(vr, SUBLANES - d, 0), pltpu.roll(vi, SUBLANES - d, 0)
                    keep = row < SUBLANES - d
                else:
                    qr, qi = pltpu.roll(vr, d, 0), pltpu.roll(vi, d, 0)
                    keep = row >= d
                mr, mi = _cmul(pr, pi, qr, qi)
                vr = vr + jnp.where(keep, mr, 0.0)
                vi = vi + jnp.where(keep, mi, 0.0)
            cr, ci = carry[0], carry[1]
            mr, mi = _cmul(tr, ti, cr, ci)
            vr, vi = vr + mr, vi + mi
            yr_ref[rows, :] = vr
            yi_ref[rows, :] = vi
            edge = 0 if reverse else SUBLANES - 1
            new = (vr[edge:edge + 1, :], vi[edge:edge + 1, :])
            if not with_da:
                return new
            pidx = jnp.maximum(idx - 1, 0)
            prows = pl.ds(pl.multiple_of(pidx * SUBLANES, SUBLANES), SUBLANES)
            live = (idx > 0).astype(F32)
            s0r = sr_ref[prows, :][SUBLANES - 1:SUBLANES, :] * live
            s0i = si_ref[prows, :][SUBLANES - 1:SUBLANES, :] * live
            spr = jnp.where(row == 0, s0r, pltpu.roll(sr_ref[rows, :], 1, 0))
            spi = jnp.where(row == 0, s0i, pltpu.roll(si_ref[rows, :], 1, 0))
            return new + (carry[2] + vr * spr + vi * spi, carry[3] + vi * spr - vr * spi)

        zero = jnp.zeros((1, tc), F32)
        init = (zero, zero)
        if with_da:
            acc0 = jnp.zeros((SUBLANES, tc), F32)
            init = init + (acc0, acc0)
        fin = lax.fori_loop(0, nt, step, init)
        if with_da:
            dar_ref[...] = jnp.sum(fin[2], axis=0, keepdims=True)
            dai_ref[...] = jnp.sum(fin[3], axis=0, keepdims=True)

    col = pl.BlockSpec((L, tc), lambda j: (0, j))
    vec = pl.BlockSpec((1, tc), lambda j: (0, j))
    n_in = 6 if with_da else 4
    in_specs = [vec, vec] + [col] * (n_in - 2)
    out_specs = [col, col] + ([vec, vec] if with_da else [])
    out_shape = [jax.ShapeDtypeStruct((L, S), F32)] * 2 + ([jax.ShapeDtypeStruct((1, S), F32)] * 2 if with_da else [])
    args = (ar, ai, xr, xi) + (tuple(state) if with_da else ())
    return pl.pallas_call(body, name=name, grid=(S // tc,), in_specs=in_specs, out_specs=out_specs,
                          out_shape=out_shape, compiler_params=_params())(*args)


def _bd_build(w_gnp):
    G, N, P = w_gnp.shape
    nb = G // S5_GB
    eye = jnp.eye(S5_GB, dtype=F32)
    x = w_gnp.reshape(nb, S5_GB, N, P).transpose(0, 1, 3, 2)
    w = x[:, :, :, None, :] * eye[None, :, None, :, None]
    return w.reshape(nb, S5_GB * P, S5_GB * N)


def _bd_extract(w, G, N, P):
    nb = G // S5_GB
    eye = jnp.eye(S5_GB, dtype=F32)
    w5 = w.reshape(nb, S5_GB, P, S5_GB, N)
    d = jnp.sum(w5 * eye[None, :, None, :, None], axis=3)
    return d.transpose(0, 1, 3, 2).reshape(G, N, P)


def _s5_discretize(lam_re, lam_im, log_dt, b_re, b_im):
    dt = jnp.exp(log_dt)[:, None]
    mag = jnp.exp(lam_re * dt)
    ang = lam_im * dt
    ab_re = mag * jnp.cos(ang)
    ab_im = mag * jnp.sin(ang)
    den = lam_re * lam_re + lam_im * lam_im
    nr = ab_re - 1.0
    ni = ab_im
    q_re = (nr * lam_re + ni * lam_im) / den
    q_im = (ni * lam_re - nr * lam_im) / den
    bb_re = q_re[..., None] * b_re - q_im[..., None] * b_im
    bb_im = q_re[..., None] * b_im + q_im[..., None] * b_re
    return ab_re, ab_im, bb_re, bb_im


HBM_SPEC = pl.BlockSpec(memory_space=pl.ANY)


def _place():
    x, y, c = lax.axis_index("x"), lax.axis_index("y"), lax.axis_index("c")
    others = [(1 - x, y), (x, 1 - y), (1 - x, 1 - y)]
    return x, y, c, 2 * x + y, others


def _half(ref, h, axis):
    n = ref.shape[axis] // 2
    idx = [slice(None)] * len(ref.shape)
    idx[axis] = pl.ds(h * n, n)
    return ref.at[tuple(idx)]


def _ids():
    return jnp.stack([2 * lax.axis_index("x") + lax.axis_index("y"), lax.axis_index("c")]).astype(jnp.int32)


def _cast_place(name, w, layer, ids):
    _, R, C = w.shape
    tr = _tile(R, max(16, (1 << 19) // C), 16)

    def body(ids_ref, w_ref, o_ref):
        o_ref[...] = w_ref[...].astype(o_ref.dtype)

    grid_spec = pltpu.PrefetchScalarGridSpec(
        num_scalar_prefetch=1, grid=(R // tr,),
        in_specs=[pl.BlockSpec((None, tr, C), lambda i, ids_ref: (layer, i, 0))],
        out_specs=pl.BlockSpec((None, tr, C), lambda i, ids_ref: (ids_ref[0], i, 0)))
    return pl.pallas_call(body, name=name, grid_spec=grid_spec,
                          out_shape=jax.ShapeDtypeStruct((N_CHIPS, R, C), BF16), compiler_params=_params())(ids, w)


def _all_gather_chips(name, bufs):
    T = len(bufs)

    def body(*refs):
        outs = refs[T:2 * T]
        send, recv = refs[2 * T:]
        x, y, c, me, others = _place()
        sib = (x, y, 1 - c)
        first, passed = [], []
        for t in range(T):
            for k, (ox, oy) in enumerate(others):
                mine = _half(outs[t].at[me], c, 0)
                cp = pltpu.make_async_remote_copy(
                    src_ref=mine, dst_ref=mine,
                    send_sem=send.at[t, k], recv_sem=recv.at[t, k], device_id=(ox, oy, c), device_id_type=MESH)
                cp.start()
                first.append(cp)
        for t in range(T):
            for k, (ox, oy) in enumerate(others):
                got = _half(outs[t].at[2 * ox + oy], c, 0)
                cp = pltpu.make_async_remote_copy(
                    src_ref=got, dst_ref=got, send_sem=send.at[t, 3 + k], recv_sem=recv.at[t, 3 + k],
                    device_id=sib, device_id_type=MESH)
                pltpu.make_async_remote_copy(src_ref=got, dst_ref=got, send_sem=send.at[t, k], recv_sem=recv.at[t, k],
                                             device_id=sib, device_id_type=MESH).wait_recv()
                cp.start()
                passed.append(cp)
        for t in range(T):
            for k, (ox, oy) in enumerate(others):
                theirs = _half(outs[t].at[2 * ox + oy], 1 - c, 0)
                pltpu.make_async_remote_copy(src_ref=theirs, dst_ref=theirs, send_sem=send.at[t, 3 + k],
                                             recv_sem=recv.at[t, 3 + k], device_id=sib, device_id_type=MESH).wait_recv()
        for cp in first + passed:
            cp.wait_send()

    return pl.pallas_call(
        body, name=name, in_specs=[HBM_SPEC] * T, out_specs=[HBM_SPEC] * T,
        out_shape=[jax.ShapeDtypeStruct(b.shape, b.dtype) for b in bufs],
        input_output_aliases={t: t for t in range(T)},
        scratch_shapes=[pltpu.SemaphoreType.DMA((T, 6)), pltpu.SemaphoreType.DMA((T, 6))],
    )(*bufs)


def _swap_halves(name, grads):
    T = len(grads)

    def body(*refs):
        ins, theirs = refs[:T], refs[T:2 * T]
        send, recv = refs[2 * T:]
        x, y, c, me, others = _place()
        cps = []
        for t in range(T):
            cp = pltpu.make_async_remote_copy(
                src_ref=_half(ins[t], 1 - c, 1), dst_ref=theirs[t], send_sem=send.at[t], recv_sem=recv.at[t],
                device_id=(x, y, 1 - c), device_id_type=MESH)
            cp.start()
            cps.append(cp)
        for cp in cps:
            cp.wait()

    shp = [jax.ShapeDtypeStruct((g.shape[0], g.shape[1] // 2, g.shape[2]), g.dtype) for g in grads]
    return pl.pallas_call(
        body, name=name, in_specs=[HBM_SPEC] * T, out_specs=[HBM_SPEC] * T, out_shape=shp,
        scratch_shapes=[pltpu.SemaphoreType.DMA((T,)), pltpu.SemaphoreType.DMA((T,))],
    )(*grads)


def _exchange_chips(name, parts, lands):
    T = len(parts)

    def body(*refs):
        ins, outs = refs[:T], refs[2 * T:3 * T]
        send, recv = refs[3 * T:]
        x, y, c, me, others = _place()
        cps = []
        for t in range(T):
            for k, (ox, oy) in enumerate(others):
                cp = pltpu.make_async_remote_copy(
                    src_ref=ins[t].at[2 * ox + oy], dst_ref=outs[t].at[me], send_sem=send.at[t, k],
                    recv_sem=recv.at[t, k], device_id=(ox, oy, c), device_id_type=MESH)
                cp.start()
                cps.append(cp)
        for cp in cps:
            cp.wait()

    return pl.pallas_call(
        body, name=name, in_specs=[HBM_SPEC] * (2 * T), out_specs=[HBM_SPEC] * T,
        out_shape=[jax.ShapeDtypeStruct(p.shape, p.dtype) for p in lands],
        input_output_aliases={T + t: t for t in range(T)},
        scratch_shapes=[pltpu.SemaphoreType.DMA((T, 3)), pltpu.SemaphoreType.DMA((T, 3))],
    )(*parts, *lands)


def _join_halves(name, fulls):
    T = len(fulls)

    def body(*refs):
        outs = refs[T:2 * T]
        send, recv = refs[2 * T:]
        x, y, c, me, others = _place()
        cps = []
        for t in range(T):
            mine = _half(outs[t], c, 0)
            cp = pltpu.make_async_remote_copy(
                src_ref=mine, dst_ref=mine, send_sem=send.at[t], recv_sem=recv.at[t],
                device_id=(x, y, 1 - c), device_id_type=MESH)
            cp.start()
            cps.append(cp)
        for cp in cps:
            cp.wait()

    return pl.pallas_call(
        body, name=name, in_specs=[HBM_SPEC] * T, out_specs=[HBM_SPEC] * T,
        out_shape=[jax.ShapeDtypeStruct(f.shape, f.dtype) for f in fulls],
        input_output_aliases={t: t for t in range(T)},
        scratch_shapes=[pltpu.SemaphoreType.DMA((T,)), pltpu.SemaphoreType.DMA((T,))],
    )(*fulls)


def _all_reduce_small(name, buf):
    _, r, w = buf.shape

    def body(in_ref, out_ref, land, send, recv):
        x, y, c, _, _ = _place()
        me = 4 * x + 2 * y + c

        def peer(k):
            return (1 - x if k & 4 else x, 1 - y if k & 2 else y, 1 - c if k & 1 else c)

        cps = []
        for k in range(1, N_DEV):
            px, py, pc = peer(k)
            cp = pltpu.make_async_remote_copy(
                src_ref=in_ref.at[4 * px + 2 * py + pc], dst_ref=land.at[me], send_sem=send.at[k - 1],
                recv_sem=recv.at[k - 1], device_id=(px, py, pc), device_id_type=MESH)
            cp.start()
            cps.append(cp)
        land[me] = in_ref[me]
        for cp in cps:
            cp.wait()
        total = land[0]
        for d in range(1, N_DEV):
            total = total + land[d]
        out_ref[me] = total
        cps = []
        for k in range(1, N_DEV):
            cp = pltpu.make_async_remote_copy(
                src_ref=out_ref.at[me], dst_ref=out_ref.at[me], send_sem=send.at[N_DEV - 2 + k],
                recv_sem=recv.at[N_DEV - 2 + k], device_id=peer(k), device_id_type=MESH)
            cp.start()
            cps.append(cp)
        for cp in cps:
            cp.wait()

    vm = pl.BlockSpec(memory_space=pltpu.VMEM)
    return pl.pallas_call(
        body, name=name, in_specs=[vm], out_specs=vm, out_shape=jax.ShapeDtypeStruct(buf.shape, F32),
        scratch_shapes=[pltpu.VMEM(buf.shape, F32), pltpu.SemaphoreType.DMA((2 * N_DEV - 2,)),
                        pltpu.SemaphoreType.DMA((2 * N_DEV - 2,))],
        compiler_params=_params())(buf)


def _add_pairs(name, g, theirs, ids):
    nb, H, C = theirs.shape
    th = _tile(H, max(SUBLANES * 2, (1 << 19) // C), 16)
    nh = H // th

    def body(ids_ref, g_ref, t_ref, p_ref, l_ref):
        s = (g_ref[...].astype(F32) + t_ref[...].astype(F32)).astype(p_ref.dtype)
        p_ref[...] = s

        @pl.when(pl.program_id(1) == ids_ref[0])
        def _():
            l_ref[...] = s

    blk = (None, th, C)
    grid_spec = pltpu.PrefetchScalarGridSpec(
        num_scalar_prefetch=1, grid=(nh, nb),
        in_specs=[pl.BlockSpec(blk, lambda i, j, ids_ref: (j, ids_ref[1] * nh + i, 0)),
                  pl.BlockSpec(blk, lambda i, j, ids_ref: (j, i, 0))],
        out_specs=[pl.BlockSpec(blk, lambda i, j, ids_ref: (j, i, 0)),
                   pl.BlockSpec(blk, lambda i, j, ids_ref: (ids_ref[0], i, 0))])
    return pl.pallas_call(body, name=name, grid_spec=grid_spec,
                          out_shape=[jax.ShapeDtypeStruct(theirs.shape, theirs.dtype)] * 2,
                          compiler_params=_params())(ids, g, theirs)


def _sum_chips(name, parts, ids):
    nb, H, C = parts.shape
    th = _tile(H, max(SUBLANES * 2, (1 << 19) // C), 16)
    nh = H // th

    def body(ids_ref, p_ref, o_ref):
        tot = p_ref[0].astype(F32)
        for b in range(1, nb):
            tot = tot + p_ref[b].astype(F32)
        o_ref[...] = tot

    grid_spec = pltpu.PrefetchScalarGridSpec(
        num_scalar_prefetch=1, grid=(nh,),
        in_specs=[pl.BlockSpec((nb, th, C), lambda i, ids_ref: (0, i, 0))],
        out_specs=pl.BlockSpec((th, C), lambda i, ids_ref: (ids_ref[1] * nh + i, 0)))
    return pl.pallas_call(body, name=name, grid_spec=grid_spec,
                          out_shape=jax.ShapeDtypeStruct((2 * H, C), F32), compiler_params=_params())(ids, parts)


def _adamw_math(w, g, m, v):
    m = ADAM_B1 * m + (1.0 - ADAM_B1) * g
    v = ADAM_B2 * v + (1.0 - ADAM_B2) * (g * g)
    m_hat = m / (1.0 - ADAM_B1 ** ADAM_STEP)
    v_hat = v / (1.0 - ADAM_B2 ** ADAM_STEP)
    delta = -ADAM_LR * (m_hat / (jnp.sqrt(v_hat) + ADAM_EPS) + ADAM_WD * w)
    return delta, m, v


def _adamw_big(name, w, m, v, gs):
    depth, R, C = w.shape
    tr = _tile(R, max(SUBLANES, (1 << 18) // C))

    def body(w_ref, m_ref, v_ref, *refs):
        g_refs, (go, do, mo, vo) = refs[:depth], refs[depth:]
        li = pl.program_id(0)
        g = g_refs[0][...]
        for l in range(1, depth):
            g = jnp.where(li == l, g_refs[l][...], g)
        delta, mn, vn = _adamw_math(w_ref[...], g, m_ref[...], v_ref[...])
        go[...] = g
        do[...] = delta
        mo[...] = mn
        vo[...] = vn

    spec = pl.BlockSpec((None, tr, C), lambda li, i: (li, i, 0))
    g_specs = [pl.BlockSpec((tr, C), functools.partial(lambda li, i, l: (jnp.where(li == l, i, 0), 0), l=l))
               for l in range(depth)]
    return pl.pallas_call(body, name=name, grid=(depth, R // tr), in_specs=[spec] * 3 + g_specs,
                          out_specs=[spec] * 4, out_shape=[jax.ShapeDtypeStruct(w.shape, F32)] * 4,
                          compiler_params=_params())(w, m, v, *gs)


def _adamw_small(name, w, g, m, v):
    def fn(i, n, R, B, P, N):
        return list(_adamw_math(*R)), []

    return _rowwise(name, fn, [w, g, m, v], outs=[(LANES, F32)] * 3)


def _pack(arrs, rows_mult):
    flat = jnp.concatenate([a.reshape(-1) for a in arrs])
    n = flat.shape[0]
    per = rows_mult * LANES
    pad = (-n) % per
    flat = jnp.pad(flat, (0, pad))
    return flat.reshape(-1, LANES), n


def _unpack(buf, like):
    flat = buf.reshape(-1)
    out, off = [], 0
    for a in like:
        out.append(flat[off:off + a.size].reshape(a.shape))
        off += a.size
    return out


def _ffn_fwd(tag, alpha, xf, xb, wg, wu, wd, g, b):
    def up(ps, _):
        hg, hu = ps
        return hg, hu, hg * _sigmoid(hg) * hu

    hg, hu, act = _mm_expand(f"{tag}_up", xb, [wg, wu], False, up, [], [F32, F32, BF16])

    def down(acc, ex, bc):
        y, xhat, rstd = _layer_norm(alpha * ex[0] + 0.5 * acc, bc[0], bc[1])
        return y, y, xhat, jnp.broadcast_to(rstd, (rstd.shape[0], LANES))

    D = xf.shape[1]
    yf, yb, xhat, rstd = _mm_contract(f"{tag}_down", [act], [wd], False, down, [xf], [g, b],
                                      [(D, F32), (D, BF16), (D, F32), (LANES, F32)])
    return yf, yb, (xb, hg, hu, act, xhat, rstd)


def _ln_bwd(name, dy, xhat, rstd, g, scale):
    D = dy.shape[1]

    def fn(i, n, R, B, P, N):
        d, xh, rs = R
        dr = _layer_norm_bwd(d, xh, rs[:, :1], B[0])
        return [dr, scale * dr], [jnp.sum(d * xh, axis=0, keepdims=True), jnp.sum(d, axis=0, keepdims=True)]

    return _rowwise(name, fn, [dy, xhat, rstd], bcast=[g], outs=[(D, F32), (D, BF16)], sums=[D, D])


def _ffn_bwd(tag, alpha, dy, saved, wg, wu, wd, g):
    xb, hg, hu, act, xhat, rstd = saved
    dr, dfb, dg, db = _ln_bwd(f"{tag}_ln_bwd", dy, xhat, rstd, g, 0.5)

    def dact(ps, ex):
        da, hgv, huv = ps[0], ex[0], ex[1]
        sg = _sigmoid(hgv)
        return da * huv * (sg * (1.0 + hgv * (1.0 - sg))), da * (hgv * sg)

    dhg, dhu = _mm_expand(f"{tag}_dact", dfb, [wd], True, dact, [hg, hu], [BF16, BF16])
    g_wd = _wgrad_rows(f"{tag}_gwd", act, dfb, tr=1408)
    g_wg = _wgrad_cols(f"{tag}_gwg", xb, dhg)
    g_wu = _wgrad_cols(f"{tag}_gwu", xb, dhu)

    def dxin(acc, ex, bc):
        return [alpha * ex[0] + acc]

    D = dy.shape[1]
    dx, = _mm_contract(f"{tag}_dx", [dhg, dhu], [wg, wu], True, dxin, [dr], [], [(D, F32)])
    return dx, (g_wg, g_wu, g_wd), dg, db


def _conv_taps(v, pv, i, w):
    tm = v.shape[0]
    ext = jnp.concatenate([pv * (i > 0).astype(F32), v], axis=0)
    v1 = pltpu.roll(ext, 1, 0)[SUBLANES:SUBLANES + tm]
    v2 = pltpu.roll(ext, 2, 0)[SUBLANES:SUBLANES + tm]
    return w[0:1] * v2 + w[1:2] * v1 + w[2:3] * v, v1, v2


def _mixer_fwd(tag, alpha, xf, xb, p, w_in, w_glu, w_out):
    Dh = w_in.shape[2]
    proj, = _mm_expand(f"{tag}_proj", xb, [w_in], False, lambda ps, _: ps, [], [F32])
    u, gb, gc, h = [(proj, Dh, j) for j in range(4)]
    nblk = p["wb_re"].shape[0]
    S = nblk * p["wb_re"].shape[2]
    ident = lambda acc, ex, bx: [acc]
    bu_re, = _bd_mm(f"{tag}_bu_re", [(proj, p["wb_re"])], False, ident, [], [], [F32])
    bu_im, = _bd_mm(f"{tag}_bu_im", [(proj, p["wb_im"])], False, ident, [], [], [F32])
    s_re, s_im = _scan(f"{tag}_scan", p["a_re"], p["a_im"], bu_re, bu_im)

    def yout(acc, ex, bx):
        ys = acc + bx[0] * ex[0]
        return ys, _gelu(ys), _gelu(ys)

    ys, yg, ygb = _bd_mm(f"{tag}_yout", [(s_re, p["wc_re"]), (s_im, p["wc_imn"])], False, yout, [proj], [p["d"]],
                         [F32, F32, BF16])

    def glu(acc, ex, bc):
        yy = ex[0] * _sigmoid(acc)
        return acc, yy * _rms_inv(yy) * bc[0]

    t, yn = _mm_contract(f"{tag}_glu", [ygb], [w_glu], False, glu, [yg], [p["g_s5"]], [(Dh, F32), (Dh, BF16)])

    def conv(i, n, R, B, P, N):
        gbv, gcv, hv = R
        cw, cb, gcn = B
        cv, _, _ = _conv_taps(gcv * hv, P[0] * P[1], i, cw)
        z = gbv * (cv + cb)
        return [z * _rms_inv(z) * gcn], []

    zn, = _rowwise(f"{tag}_conv", conv, [gb, gc, h], bcast=[p["conv_w"], p["conv_b"], p["g_conv"]],
                   outs=[(Dh, BF16)], prev=[1, 2])
    cat = jnp.concatenate([yn, zn], axis=1)

    def out(acc, ex, bc):
        y, xhat, rstd = _layer_norm(alpha * ex[0] + acc, bc[0], bc[1])
        return y, y, xhat, jnp.broadcast_to(rstd, (rstd.shape[0], LANES))

    D = xf.shape[1]
    yf, yb, xhat, rstd = _mm_contract(f"{tag}_out", [cat], [w_out], False, out, [xf], [p["ln2_g"], p["ln2_b"]],
                                      [(D, F32), (D, BF16), (D, F32), (LANES, F32)])
    return yf, yb, (xb, proj, s_re, s_im, ys, yg, ygb, t, cat, xhat, rstd)


def _mixer_bwd(tag, alpha, dy, saved, p, w_in, w_glu, w_out):
    xb, proj, s_re, s_im, ys, yg, ygb, t, cat, xhat, rstd = saved
    Dh = w_in.shape[2]
    D = dy.shape[1]
    gb, gc, h = [(proj, Dh, j) for j in range(1, 4)]
    dr, dmb, dg2, db2 = _ln_bwd(f"{tag}_ln_bwd", dy, xhat, rstd, p["ln2_g"], 1.0)
    dcat, = _mm_expand(f"{tag}_dcat", dmb, [w_out], True, lambda ps, _: ps, [], [F32])
    g_wout = _wgrad_rows(f"{tag}_gwout", cat, dmb)
    half = dcat.shape[1] // 2

    def conv_b1(i, n, R, B, P, N):
        dzn, gbv, gcv, hv = R
        cw, cb, gcn = B
        v = gcv * hv
        cv, v1, v2 = _conv_taps(v, P[0] * P[1], i, cw)
        cv = cv + cb
        z = gbv * cv
        rinv = _rms_inv(z)
        dz = _rms_bwd(dzn, z, rinv, gcn)
        dcv = dz * gbv
        col = lambda a: jnp.sum(a, axis=0, keepdims=True)
        return [dz * cv, dcv], [col(dzn * z * rinv), col(dcv), col(dcv * v2), col(dcv * v1), col(dcv * v)]

    dgb, dcv, dg_conv, dconv_b, dw0, dw1, dw2 = _rowwise(
        f"{tag}_conv_b1", conv_b1, [(dcat, half, 1), gb, gc, h], bcast=[p["conv_w"], p["conv_b"], p["g_conv"]],
        outs=[(Dh, BF16), (Dh, F32)], sums=[Dh] * 5, prev=[2, 3])

    def conv_b2(i, n, R, B, P, N):
        d, gcv, hv = R
        cw = B[0]
        tm = d.shape[0]
        ext = jnp.concatenate([d, N[0] * (i < n - 1).astype(F32)], axis=0)
        d1 = pltpu.roll(ext, tm + SUBLANES - 1, 0)[:tm]
        d2 = pltpu.roll(ext, tm + SUBLANES - 2, 0)[:tm]
        dv = cw[2:3] * d + cw[1:2] * d1 + cw[0:1] * d2
        return [dv * hv, dv * gcv], []

    dgc, dh = _rowwise(f"{tag}_conv_b2", conv_b2, [dcv, gc, h], bcast=[p["conv_w"]], outs=[(Dh, BF16)] * 2, nxt=[0])

    def glu_b(i, n, R, B, P, N):
        dyn, ygv, tv = R
        sg = _sigmoid(tv)
        yy = ygv * sg
        rinv = _rms_inv(yy)
        dyy = _rms_bwd(dyn, yy, rinv, B[0])
        return [dyy * ygv * sg * (1.0 - sg), dyy * sg], [jnp.sum(dyn * yy * rinv, axis=0, keepdims=True)]

    dtb, dyg0, dg_s5 = _rowwise(f"{tag}_glu_b", glu_b, [(dcat, half, 0), yg, t], bcast=[p["g_s5"]],
                                outs=[(Dh, BF16), (Dh, F32)], sums=[Dh])

    def dys_epi(ps, ex):
        return [(ps[0] + ex[0]) * _gelu_grad(ex[1])]

    dys, = _mm_expand(f"{tag}_dys", dtb, [w_glu], True, dys_epi, [dyg0, ys], [F32])
    g_wglu = _wgrad_rows(f"{tag}_gwglu", ygb, dtb)
    ident = lambda acc, ex, bx: [acc]
    ds_re, = _bd_mm(f"{tag}_ds_re", [(dys, p["wc_re"])], True, ident, [], [], [F32])
    ds_im, = _bd_mm(f"{tag}_ds_im", [(dys, p["wc_imn"])], True, ident, [], [], [F32])
    l_re, l_im, da_re, da_im = _scan(f"{tag}_rscan", p["a_re"], -p["a_im"], ds_re, ds_im, reverse=True,
                                     state=(s_re, s_im))

    def du_epi(acc, ex, bx):
        return [acc + bx[0] * ex[0]]

    du, = _bd_mm(f"{tag}_du", [(l_re, p["wb_re"]), (l_im, p["wb_im"])], True, du_epi, [dys], [p["d"]], [BF16])
    nblk = p["wb_re"].shape[0]
    u = (proj, Dh, 0)
    u_arr = proj[:, :Dh]
    g_wb_re = _bd_wgrad(f"{tag}_gwb_re", u_arr, l_re, nblk)
    g_wb_im = _bd_wgrad(f"{tag}_gwb_im", u_arr, l_im, nblk)
    g_wc_re = _bd_wgrad(f"{tag}_gwc_re", s_re, dys, nblk)
    g_wc_imn = _bd_wgrad(f"{tag}_gwc_im", s_im, dys, nblk)

    def dd_fn(i, n, R, B, P, N):
        return [], [jnp.sum(R[0] * R[1], axis=0, keepdims=True)]

    dd, = _rowwise(f"{tag}_dd", dd_fn, [dys, u], sums=[Dh])

    dproj = jnp.concatenate([du, dgb, dgc, dh], axis=1)

    def dxin(acc, ex, bc):
        return [alpha * ex[0] + acc]

    dx, = _mm_contract(f"{tag}_dx", [dproj], [w_in], True, dxin, [dr], [], [(D, F32)])
    g_win = _wgrad_cols(f"{tag}_gwin", xb, dproj)
    small = dict(ln2_g=dg2, ln2_b=db2, g_conv=dg_conv, conv_b=dconv_b,
                 conv_w=jnp.concatenate([dw0, dw1, dw2], axis=0), g_s5=dg_s5, d=dd,
                 da_re=da_re, da_im=da_im, wb_re=g_wb_re, wb_im=g_wb_im, wc_re=g_wc_re, wc_imn=g_wc_imn)
    return dx, (g_win, g_wglu, g_wout), small


def _reduce_layer(tag, grads, ids):
    theirs = _swap_halves(f"{tag}_swap", grads)
    added = [_add_pairs(f"{tag}_add{t}", g, b, ids) for t, (g, b) in enumerate(zip(grads, theirs))]
    got = _exchange_chips(f"{tag}_xchg", [a[0] for a in added], [a[1] for a in added])
    fulls = [_sum_chips(f"{tag}_sum{t}", g, ids) for t, g in enumerate(got)]
    return _join_halves(f"{tag}_join", fulls)


BIG = ["ffn1_gate", "ffn1_up", "ffn1_down", "w_in", "s5_w_glu", "w_out", "ffn2_gate", "ffn2_up", "ffn2_down"]
SMALL = ["ln1_g", "ln1_b", "s5_lam_re", "s5_lam_im", "s5_log_dt", "s5_b_re", "s5_b_im", "s5_c_re", "s5_c_im", "s5_d",
         "conv_w", "conv_b", "g_s5", "g_conv", "ln2_g", "ln2_b", "ln3_g", "ln3_b"]
WEIGHTS = ['ffn1_gate', 'ffn1_up', 'ffn1_down', 'ln1_g', 'ln1_b', 'w_in', 's5_lam_re', 's5_lam_im', 's5_log_dt',
           's5_b_re', 's5_b_im', 's5_c_re', 's5_c_im', 's5_d', 's5_w_glu', 'conv_w', 'conv_b', 'g_s5', 'g_conv',
           'w_out', 'ln2_g', 'ln2_b', 'ffn2_gate', 'ffn2_up', 'ffn2_down', 'ln3_g', 'ln3_b']


def _step(W, M, V, x, target):
    depth = W["ffn1_gate"].shape[0]
    alpha = (2.0 * depth) ** 0.25
    L, D = x.shape
    G, N = W["s5_lam_re"].shape[1:]
    P = W["s5_b_re"].shape[3]
    Dh = G * P
    chip = 2 * lax.axis_index("x") + lax.axis_index("y")
    cw_cols = W["conv_w"].shape[2]

    conv_w_blk = jnp.zeros((depth, CONV_W, N_CHIPS, cw_cols), F32)
    conv_w_blk = lax.dynamic_update_slice(conv_w_blk, W["conv_w"][:, :, None, :] * 0.5, (0, 0, chip, 0))
    pre, _ = _pack([conv_w_blk], SUBLANES * N_DEV)
    pre = _all_reduce_small("conv_w_gather", pre.reshape(N_DEV, -1, LANES))
    conv_w_full = pre.reshape(-1)[:conv_w_blk.size].reshape(depth, CONV_W, N_CHIPS * cw_cols)

    ids = _ids()
    gathered = []
    for l in range(depth):
        placed = [_cast_place(f"place{l}_{n}", W[n], l, ids) for n in BIG]
        gathered.append(_all_gather_chips(f"gather{l}", placed))

    s5_vjps, lp = [], []
    for l in range(depth):
        (a_re, a_im, bb_re, bb_im), vjp = jax.vjp(_s5_discretize, W["s5_lam_re"][l], W["s5_lam_im"][l],
                                                  W["s5_log_dt"][l], W["s5_b_re"][l], W["s5_b_im"][l])
        s5_vjps.append(vjp)
        row = lambda a: a.reshape(1, -1)
        lp.append(dict(
            a_re=row(a_re), a_im=row(a_im), wb_re=_bd_build(bb_re), wb_im=_bd_build(bb_im),
            wc_re=_bd_build(W["s5_c_re"][l].transpose(0, 2, 1)).transpose(0, 2, 1),
            wc_imn=-_bd_build(W["s5_c_im"][l].transpose(0, 2, 1)).transpose(0, 2, 1),
            d=row(W["s5_d"][l]), g_s5=row(W["g_s5"][l]), g_conv=row(W["g_conv"][l]), conv_b=row(W["conv_b"][l]),
            conv_w=conv_w_full[l], ln2_g=row(W["ln2_g"][l]), ln2_b=row(W["ln2_b"][l])))

    def cast(i, n, R, B, P_, N_):
        return [R[0]], []

    xb, = _rowwise("cast_x", cast, [x], outs=[(D, BF16)])
    xf = x
    saved = []
    for l in range(depth):
        wg1, wu1, wd1, w_in, w_glu, w_out, wg2, wu2, wd2 = gathered[l]
        row = lambda a: a.reshape(1, -1)
        xf, xb, s1 = _ffn_fwd(f"l{l}_ffn1", alpha, xf, xb, wg1, wu1, wd1, row(W["ln1_g"][l]), row(W["ln1_b"][l]))
        xf, xb, s2 = _mixer_fwd(f"l{l}_mix", alpha, xf, xb, lp[l], w_in, w_glu, w_out)
        xf, xb, s3 = _ffn_fwd(f"l{l}_ffn2", alpha, xf, xb, wg2, wu2, wd2, row(W["ln3_g"][l]), row(W["ln3_b"][l]))
        saved.append((s1, s2, s3))

    def loss_fn(i, n, R, B, P_, N_):
        err = R[0] - R[1]
        return [err * (1.0 / D)], [jnp.sum(0.5 * err * err * (1.0 / D), axis=0, keepdims=True)]

    dx, loss_cols = _rowwise("loss", loss_fn, [xf, target], outs=[(D, F32)], sums=[D])
    loss = lax.psum(jnp.sum(loss_cols), ("x", "y", "c"))

    big_g = {n: [None] * depth for n in BIG}
    small_g = {n: [None] * depth for n in SMALL}
    for l in reversed(range(depth)):
        wg1, wu1, wd1, w_in, w_glu, w_out, wg2, wu2, wd2 = gathered[l]
        s1, s2, s3 = saved[l]
        row = lambda a: a.reshape(1, -1)
        dx, (g_wg2, g_wu2, g_wd2), dg3, db3 = _ffn_bwd(f"l{l}_ffn2", alpha, dx, s3, wg2, wu2, wd2, row(W["ln3_g"][l]))
        dx, (g_win, g_wglu, g_wout), sm = _mixer_bwd(f"l{l}_mix", alpha, dx, s2, lp[l], w_in, w_glu, w_out)
        dx, (g_wg1, g_wu1, g_wd1), dg1, db1 = _ffn_bwd(f"l{l}_ffn1", alpha, dx, s1, wg1, wu1, wd1, row(W["ln1_g"][l]))
        red = _reduce_layer(f"l{l}_red", [g_wg1, g_wu1, g_wd1, g_win, g_wglu, g_wout, g_wg2, g_wu2, g_wd2], ids)
        for n, r in zip(BIG, red):
            big_g[n][l] = r
        d_bb_re = _bd_extract(sm["wb_re"], G, N, P)
        d_bb_im = _bd_extract(sm["wb_im"], G, N, P)
        d_lre, d_lim, d_ldt, d_bre, d_bim = s5_vjps[l]((sm["da_re"].reshape(G, N), sm["da_im"].reshape(G, N),
                                                        d_bb_re, d_bb_im))
        d_cre = _bd_extract(sm["wc_re"].transpose(0, 2, 1), G, N, P).transpose(0, 2, 1)
        d_cim = -_bd_extract(sm["wc_imn"].transpose(0, 2, 1), G, N, P).transpose(0, 2, 1)
        vals = dict(ln1_g=dg1, ln1_b=db1, s5_lam_re=d_lre, s5_lam_im=d_lim, s5_log_dt=d_ldt, s5_b_re=d_bre,
                    s5_b_im=d_bim, s5_c_re=d_cre, s5_c_im=d_cim, s5_d=sm["d"], conv_w=sm["conv_w"],
                    conv_b=sm["conv_b"], g_s5=sm["g_s5"], g_conv=sm["g_conv"], ln2_g=sm["ln2_g"], ln2_b=sm["ln2_b"],
                    ln3_g=dg3, ln3_b=db3)
        for n in SMALL:
            small_g[n][l] = vals[n].reshape((W[n].shape[1:] if n != "conv_w" else (CONV_W, N_CHIPS * cw_cols)))
    grad_x = dx

    sg = [jnp.stack(small_g[n]) for n in SMALL]
    packed, _ = _pack(sg, SUBLANES * N_DEV)
    packed = _all_reduce_small("small_allreduce", packed.reshape(N_DEV, -1, LANES)).reshape(-1, LANES)
    sg = dict(zip(SMALL, _unpack(packed, sg)))
    sg["conv_w"] = lax.dynamic_slice_in_dim(sg["conv_w"], chip * cw_cols, cw_cols, axis=2)
    gp, _ = _pack([sg[n] for n in SMALL], SUBLANES)
    wp, _ = _pack([W[n] for n in SMALL], SUBLANES)
    mp, _ = _pack([M[n] for n in SMALL], SUBLANES)
    vp, _ = _pack([V[n] for n in SMALL], SUBLANES)
    like = [W[n] for n in SMALL]
    dsm, msm, vsm = [dict(zip(SMALL, _unpack(a, like))) for a in _adamw_small("adamw_small", wp, gp, mp, vp)]

    grads, deltas, new_m, new_v = dict(sg), dsm, msm, vsm
    for n in BIG:
        shp = W[n].shape
        flat = lambda a: a.reshape(shp[0], shp[1], shp[2])
        gr, de, mn, vn = _adamw_big(f"adamw_{n}", flat(W[n]), flat(M[n]), flat(V[n]), big_g[n])
        grads[n], deltas[n], new_m[n], new_v[n] = gr, de, mn, vn

    outs = [loss, grad_x[None]]
    for d in (grads, deltas, new_m, new_v):
        outs += [d[n] for n in WEIGHTS]
    return tuple(outs)


def kernel(x, ffn1_gate, ffn1_up, ffn1_down, ln1_g, ln1_b, w_in, s5_lam_re, s5_lam_im, s5_log_dt, s5_b_re, s5_b_im, s5_c_re, s5_c_im, s5_d, s5_w_glu, conv_w, conv_b, g_s5, g_conv, w_out, ln2_g, ln2_b, ffn2_gate, ffn2_up, ffn2_down, ln3_g, ln3_b, loss_target, m_ffn1_gate, m_ffn1_up, m_ffn1_down, m_ln1_g, m_ln1_b, m_w_in, m_s5_lam_re, m_s5_lam_im, m_s5_log_dt, m_s5_b_re, m_s5_b_im, m_s5_c_re, m_s5_c_im, m_s5_d, m_s5_w_glu, m_conv_w, m_conv_b, m_g_s5, m_g_conv, m_w_out, m_ln2_g, m_ln2_b, m_ffn2_gate, m_ffn2_up, m_ffn2_down, m_ln3_g, m_ln3_b, v_ffn1_gate, v_ffn1_up, v_ffn1_down, v_ln1_g, v_ln1_b, v_w_in, v_s5_lam_re, v_s5_lam_im, v_s5_log_dt, v_s5_b_re, v_s5_b_im, v_s5_c_re, v_s5_c_im, v_s5_d, v_s5_w_glu, v_conv_w, v_conv_b, v_g_s5, v_g_conv, v_w_out, v_ln2_g, v_ln2_b, v_ffn2_gate, v_ffn2_up, v_ffn2_down, v_ln3_g, v_ln3_b):
    a = dict(locals())
    W = {n: a[n] for n in WEIGHTS}
    M = {n: a["m_" + n] for n in WEIGHTS}
    V = {n: a["v_" + n] for n in WEIGHTS}
    return _step(W, M, V, x[0], loss_target[0])
```

```python
import functools

import jax
import jax.numpy as jnp
from jax import lax
from jax.experimental import pallas as pl
from jax.experimental.pallas import tpu as pltpu

F32 = jnp.float32
BF16 = jnp.bfloat16
MESH = pl.DeviceIdType.MESH
HIGH = lax.Precision.HIGHEST

N_CHIPS = 4
N_DEV = 8
V7X_VMEM_LIMIT = 56 * 1024 * 1024
SUBLANES = 8
LANES = 128
S5_P = 16
S5_N = 64
S5_GB = 8
CONV_W = 3
LN_EPS = 1e-5
RMS_EPS = 1e-6
ADAM_LR = 0.001
ADAM_B1 = 0.9
ADAM_B2 = 0.999
ADAM_EPS = 1e-08
ADAM_WD = 0.01
ADAM_STEP = 10
GELU_K = 0.7978845608028654
GELU_C = 0.044715


def _params():
    return pltpu.CompilerParams(vmem_limit_bytes=V7X_VMEM_LIMIT)


def _tile(n, pref, mult=SUBLANES):
    best = None
    for t in range(mult, min(n, pref) + 1, mult):
        if n % t == 0:
            best = t
    return best if best is not None else n


def _mm(a, b, precision=None):
    return jnp.dot(a, b, preferred_element_type=F32, precision=precision)


def _mm_nt(a, b, precision=None):
    return lax.dot_general(a, b, (((1,), (1,)), ((), ())), preferred_element_type=F32, precision=precision)


def _mm_tn(a, b, precision=None):
    return lax.dot_general(a, b, (((0,), (0,)), ((), ())), preferred_element_type=F32, precision=precision)


def _sigmoid(x):
    return 1.0 / (1.0 + jnp.exp(-x))


def _gelu(x):
    return 0.5 * x * (1.0 + jnp.tanh(GELU_K * (x + GELU_C * x * x * x)))


def _gelu_grad(x):
    th = jnp.tanh(GELU_K * (x + GELU_C * x * x * x))
    return 0.5 * (1.0 + th) + 0.5 * x * (1.0 - th * th) * GELU_K * (1.0 + 3.0 * GELU_C * x * x)


def _layer_norm(r, g, b):
    mu = jnp.mean(r, axis=-1, keepdims=True)
    xc = r - mu
    rstd = lax.rsqrt(jnp.mean(xc * xc, axis=-1, keepdims=True) + LN_EPS)
    xhat = xc * rstd
    return xhat * g + b, xhat, rstd


def _layer_norm_bwd(dy, xhat, rstd, g):
    dxh = dy * g
    m1 = jnp.mean(dxh, axis=-1, keepdims=True)
    m2 = jnp.mean(dxh * xhat, axis=-1, keepdims=True)
    return rstd * (dxh - m1 - xhat * m2)


def _rms_inv(x):
    return lax.rsqrt(jnp.mean(x * x, axis=-1, keepdims=True) + RMS_EPS)


def _rms_bwd(dy, x, rinv, g):
    dxh = dy * g
    return rinv * dxh - x * (rinv * rinv * rinv) * jnp.mean(dxh * x, axis=-1, keepdims=True)


def _rowwise(name, fn, rows, bcast=(), outs=(), sums=(), prev=(), nxt=(), tm=256):
    rows = [r if isinstance(r, tuple) else (r, r.shape[1], 0) for r in rows]
    L = rows[0][0].shape[0]
    tm = _tile(L, tm)
    n = L // tm
    hb = tm // SUBLANES
    nh = L // SUBLANES
    nr, nb, npv, nnx, no, ns = len(rows), len(bcast), len(prev), len(nxt), len(outs), len(sums)

    def body(*refs):
        i = pl.program_id(0)
        k = 0
        R = [r[...] for r in refs[k:k + nr]]; k += nr
        B = [r[...] for r in refs[k:k + nb]]; k += nb
        P = [r[...] for r in refs[k:k + npv]]; k += npv
        N = [r[...] for r in refs[k:k + nnx]]; k += nnx
        o_refs = refs[k:k + no]; k += no
        s_refs = refs[k:k + ns]
        O, S = fn(i, n, R, B, P, N)
        for ref, val in zip(o_refs, O):
            ref[...] = val.astype(ref.dtype)
        if ns:
            @pl.when(i == 0)
            def _():
                for ref in s_refs:
                    ref[...] = jnp.zeros_like(ref)
            for ref, val in zip(s_refs, S):
                ref[...] += val

    in_specs = [pl.BlockSpec((tm, w), functools.partial(lambda i, cb: (i, cb), cb=cb)) for _, w, cb in rows]
    in_specs += [pl.BlockSpec(b.shape, lambda i: (0, 0)) for b in bcast]
    in_specs += [pl.BlockSpec((SUBLANES, rows[j][1]),
                              functools.partial(lambda i, cb: (jnp.maximum(i * hb - 1, 0), cb), cb=rows[j][2]))
                 for j in prev]
    in_specs += [pl.BlockSpec((SUBLANES, rows[j][1]),
                              functools.partial(lambda i, cb: (jnp.minimum((i + 1) * hb, nh - 1), cb), cb=rows[j][2]))
                 for j in nxt]
    out_specs = [pl.BlockSpec((tm, w), lambda i: (i, 0)) for w, _ in outs]
    out_specs += [pl.BlockSpec((1, w), lambda i: (0, 0)) for w in sums]
    out_shape = [jax.ShapeDtypeStruct((L, w), dt) for w, dt in outs]
    out_shape += [jax.ShapeDtypeStruct((1, w), F32) for w in sums]
    args = [r[0] for r in rows] + list(bcast) + [rows[j][0] for j in prev] + [rows[j][0] for j in nxt]
    return pl.pallas_call(body, name=name, grid=(n,), in_specs=in_specs, out_specs=out_specs,
                          out_shape=out_shape, compiler_params=_params())(*args)


def _mm_expand(name, a, ws, nt, epi, extras, outs, tm=256):
    L, K = a.shape
    tm = _tile(L, tm)
    nw, ne = len(ws), len(extras)
    co = ws[0].shape[1] if nt else ws[0].shape[2]

    def body(a_ref, *refs):
        av = a_ref[...]
        ps = [(_mm_nt if nt else _mm)(av, w[...]) for w in refs[:nw]]
        vals = epi(ps, [e[...] for e in refs[nw:nw + ne]])
        for ref, val in zip(refs[nw + ne:], vals):
            ref[...] = val.astype(ref.dtype)

    in_specs = [pl.BlockSpec((tm, K), lambda j, i: (i, 0))]
    in_specs += [pl.BlockSpec((None,) + w.shape[1:], lambda j, i: (j, 0, 0)) for w in ws]
    in_specs += [pl.BlockSpec((tm, co), lambda j, i: (i, j)) for _ in extras]
    out_specs = [pl.BlockSpec((tm, co), lambda j, i: (i, j)) for _ in outs]
    out_shape = [jax.ShapeDtypeStruct((L, N_CHIPS * co), dt) for dt in outs]
    return pl.pallas_call(body, name=name, grid=(N_CHIPS, L // tm), in_specs=in_specs, out_specs=out_specs,
                          out_shape=out_shape, compiler_params=_params())(a, *ws, *extras)


def _mm_contract(name, as_, ws, nt, epi, extras, bcast, outs, tm=256):
    L = as_[0].shape[0]
    tm = _tile(L, tm)
    na, ne, nb = len(as_), len(extras), len(bcast)
    cb = as_[0].shape[1] // N_CHIPS
    n_out = ws[0].shape[1] if nt else ws[0].shape[2]

    def body(*refs):
        k = pl.program_id(1)
        a_refs, w_refs = refs[:na], refs[na:2 * na]
        e_refs = refs[2 * na:2 * na + ne]
        b_refs = refs[2 * na + ne:2 * na + ne + nb]
        o_refs = refs[2 * na + ne + nb:-1]
        acc = refs[-1]

        @pl.when(k == 0)
        def _():
            acc[...] = jnp.zeros_like(acc)

        part = None
        for a_ref, w_ref in zip(a_refs, w_refs):
            p = (_mm_nt if nt else _mm)(a_ref[...], w_ref[...])
            part = p if part is None else part + p
        acc[...] += part

        @pl.when(k == N_CHIPS - 1)
        def _():
            vals = epi(acc[...], [e[...] for e in e_refs], [b[...] for b in b_refs])
            for ref, val in zip(o_refs, vals):
                ref[...] = val.astype(ref.dtype)

    in_specs = [pl.BlockSpec((tm, cb), lambda i, k: (i, k)) for _ in as_]
    in_specs += [pl.BlockSpec((None,) + w.shape[1:], lambda i, k: (k, 0, 0)) for w in ws]
    in_specs += [pl.BlockSpec((tm, e.shape[1]), lambda i, k: (i, 0)) for e in extras]
    in_specs += [pl.BlockSpec(b.shape, lambda i, k: (0, 0)) for b in bcast]
    out_specs = [pl.BlockSpec((tm, w), lambda i, k: (i, 0)) for w, _ in outs]
    out_shape = [jax.ShapeDtypeStruct((L, w), dt) for w, dt in outs]
    return pl.pallas_call(body, name=name, grid=(L // tm, N_CHIPS), in_specs=in_specs, out_specs=out_specs,
                          out_shape=out_shape, scratch_shapes=[pltpu.VMEM((tm, n_out), F32)],
                          compiler_params=_params())(*as_, *ws, *extras, *bcast)


def _wgrad_cols(name, a, b, tk=512):
    L, K = a.shape
    C = b.shape[1] // N_CHIPS
    tk = _tile(K, tk, LANES)

    def body(a_ref, b_ref, o_ref):
        o_ref[...] = _mm_tn(a_ref[...], b_ref[...]).astype(o_ref.dtype)

    return pl.pallas_call(
        body, name=name, grid=(N_CHIPS, K // tk),
        in_specs=[pl.BlockSpec((L, tk), lambda j, kb: (0, kb)), pl.BlockSpec((L, C), lambda j, kb: (0, j))],
        out_specs=pl.BlockSpec((None, tk, C), lambda j, kb: (j, kb, 0)),
        out_shape=jax.ShapeDtypeStruct((N_CHIPS, K, C), BF16), compiler_params=_params())(a, b)


def _wgrad_rows(name, a, b, tr=512):
    L, N = b.shape
    R = a.shape[1] // N_CHIPS
    tr = _tile(R, tr, LANES)
    nrb = R // tr

    def body(a_ref, b_ref, o_ref):
        o_ref[...] = _mm_tn(a_ref[...], b_ref[...]).astype(o_ref.dtype)

    return pl.pallas_call(
        body, name=name, grid=(N_CHIPS, nrb),
        in_specs=[pl.BlockSpec((L, tr), lambda j, rb: (0, j * nrb + rb)), pl.BlockSpec((L, N), lambda j, rb: (0, 0))],
        out_specs=pl.BlockSpec((None, tr, N), lambda j, rb: (j, rb, 0)),
        out_shape=jax.ShapeDtypeStruct((N_CHIPS, R, N), BF16), compiler_params=_params())(a, b)


def _bd_mm(name, pairs, nt, epi, extras, bex, outs, tm=256):
    L = pairs[0][0].shape[0]
    nblk = pairs[0][1].shape[0]
    tm = _tile(L, tm)
    npair, ne, nx = len(pairs), len(extras), len(bex)
    w0 = pairs[0][1]
    ca, co = (w0.shape[2], w0.shape[1]) if nt else (w0.shape[1], w0.shape[2])

    def body(*refs):
        acc = None
        for q in range(npair):
            p = (_mm_nt if nt else _mm)(refs[2 * q][...], refs[2 * q + 1][...], HIGH)
            acc = p if acc is None else acc + p
        k = 2 * npair
        vals = epi(acc, [e[...] for e in refs[k:k + ne]], [e[...] for e in refs[k + ne:k + ne + nx]])
        for ref, val in zip(refs[k + ne + nx:], vals):
            ref[...] = val.astype(ref.dtype)

    in_specs = []
    args = []
    for a, w in pairs:
        in_specs += [pl.BlockSpec((tm, ca), lambda i, g: (i, g)), pl.BlockSpec((None,) + w.shape[1:], lambda i, g: (g, 0, 0))]
        args += [a, w]
    in_specs += [pl.BlockSpec((tm, co), lambda i, g: (i, g)) for _ in extras]
    in_specs += [pl.BlockSpec((1, co), lambda i, g: (0, g)) for _ in bex]
    out_specs = [pl.BlockSpec((tm, co), lambda i, g: (i, g)) for _ in outs]
    out_shape = [jax.ShapeDtypeStruct((L, nblk * co), dt) for dt in outs]
    return pl.pallas_call(body, name=name, grid=(L // tm, nblk), in_specs=in_specs, out_specs=out_specs,
                          out_shape=out_shape, compiler_params=_params())(*args, *extras, *bex)


def _bd_wgrad(name, a, b, nblk):
    L = a.shape[0]
    ca, cb = a.shape[1] // nblk, b.shape[1] // nblk

    def body(a_ref, b_ref, o_ref):
        o_ref[...] = _mm_tn(a_ref[...], b_ref[...], HIGH)

    return pl.pallas_call(
        body, name=name, grid=(nblk,),
        in_specs=[pl.BlockSpec((L, ca), lambda g: (0, g)), pl.BlockSpec((L, cb), lambda g: (0, g))],
        out_specs=pl.BlockSpec((None, ca, cb), lambda g: (g, 0, 0)),
        out_shape=jax.ShapeDtypeStruct((nblk, ca, cb), F32), compiler_params=_params())(a, b)


def _cmul(ar, ai, br, bi):
    return ar * br - ai * bi, ar * bi + ai * br


def _scan(name, ar, ai, xr, xi, reverse=False, state=None, tc=256):
    L, S = xr.shape
    tc = _tile(S, tc, LANES)
    nt = L // SUBLANES
    with_da = state is not None

    def body(*refs):
        ar_ref, ai_ref, xr_ref, xi_ref = refs[:4]
        if with_da:
            sr_ref, si_ref, yr_ref, yi_ref, dar_ref, dai_ref = refs[4:]
        else:
            yr_ref, yi_ref = refs[4:]
        a1 = (ar_ref[...], ai_ref[...])
        pw = [a1]
        for _ in range(SUBLANES - 1):
            pw.append(_cmul(*pw[-1], *a1))
        row = lax.broadcasted_iota(jnp.int32, (SUBLANES, tc), 0)
        tr = jnp.zeros((SUBLANES, tc), F32)
        ti = jnp.zeros((SUBLANES, tc), F32)
        for t in range(SUBLANES):
            p = pw[SUBLANES - 1 - t] if reverse else pw[t]
            tr = jnp.where(row == t, p[0], tr)
            ti = jnp.where(row == t, p[1], ti)

        def step(n, carry):
            idx = (nt - 1 - n) if reverse else n
            rows = pl.ds(pl.multiple_of(idx * SUBLANES, SUBLANES), SUBLANES)
            vr, vi = xr_ref[rows, :], xi_ref[rows, :]
            for d in (1, 2, 4):
                pr, pi = pw[d - 1]
                if reverse:
                    qr, qi = pltpu.roll(vr, SUBLANES - d, 0), pltpu.roll(vi, SUBLANES - d, 0)
                    keep = row < SUBLANES - d
                else:
                    qr, qi = pltpu.roll(vr, d, 0), pltpu.roll(vi, d, 0)
                    keep = row >= d
                mr, mi = _cmul(pr, pi, qr, qi)
                vr = vr + jnp.where(keep, mr, 0.0)
                vi = vi + jnp.where(keep, mi, 0.0)
            cr, ci = carry[0], carry[1]
            mr, mi = _cmul(tr, ti, cr, ci)
            vr, vi = vr + mr, vi + mi
            yr_ref[rows, :] = vr
            yi_ref[rows, :] = vi
            edge = 0 if reverse else SUBLANES - 1
            new = (vr[edge:edge + 1, :], vi[edge:edge + 1, :])
            if not with_da:
                return new
            pidx = jnp.maximum(idx - 1, 0)
            prows = pl.ds(pl.multiple_of(pidx * SUBLANES, SUBLANES), SUBLANES)
            live = (idx > 0).astype(F32)
            s0r = sr_ref[prows, :][SUBLANES - 1:SUBLANES, :] * live
            s0i = si_ref[prows, :][SUBLANES - 1:SUBLANES, :] * live
            spr = jnp.where(row == 0, s0r, pltpu.roll(sr_ref[rows, :], 1, 0))
            spi = jnp.where(row == 0, s0i, pltpu.roll(si_ref[rows, :], 1, 0))
            return new + (carry[2] + vr * spr + vi * spi, carry[3] + vi * spr - vr * spi)

        zero = jnp.zeros((1, tc), F32)
        init = (zero, zero)
        if with_da:
            acc0 = jnp.zeros((SUBLANES, tc), F32)
            init = init + (acc0, acc0)
        fin = lax.fori_loop(0, nt, step, init)
        if with_da:
            dar_ref[...] = jnp.sum(fin[2], axis=0, keepdims=True)
            dai_ref[...] = jnp.sum(fin[3], axis=0, keepdims=True)

    col = pl.BlockSpec((L, tc), lambda j: (0, j))
    vec = pl.BlockSpec((1, tc), lambda j: (0, j))
    n_in = 6 if with_da else 4
    in_specs = [vec, vec] + [col] * (n_in - 2)
    out_specs = [col, col] + ([vec, vec] if with_da else [])
    out_shape = [jax.ShapeDtypeStruct((L, S), F32)] * 2 + ([jax.ShapeDtypeStruct((1, S), F32)] * 2 if with_da else [])
    args = (ar, ai, xr, xi) + (tuple(state) if with_da else ())
    return pl.pallas_call(body, name=name, grid=(S // tc,), in_specs=in_specs, out_specs=out_specs,
                          out_shape=out_shape, compiler_params=_params())(*args)


def _bd_build(w_gnp):
    G, N, P = w_gnp.shape
    nb = G // S5_GB
    eye = jnp.eye(S5_GB, dtype=F32)
    x = w_gnp.reshape(nb, S5_GB, N, P).transpose(0, 1, 3, 2)
    w = x[:, :, :, None, :] * eye[None, :, None, :, None]
    return w.reshape(nb, S5_GB * P, S5_GB * N)


def _bd_extract(w, G, N, P):
    nb = G // S5_GB
    eye = jnp.eye(S5_GB, dtype=F32)
    w5 = w.reshape(nb, S5_GB, P, S5_GB, N)
    d = jnp.sum(w5 * eye[None, :, None, :, None], axis=3)
    return d.transpose(0, 1, 3, 2).reshape(G, N, P)


def _s5_discretize(lam_re, lam_im, log_dt, b_re, b_im):
    dt = jnp.exp(log_dt)[:, None]
    mag = jnp.exp(lam_re * dt)
    ang = lam_im * dt
    ab_re = mag * jnp.cos(ang)
    ab_im = mag * jnp.sin(ang)
    den = lam_re * lam_re + lam_im * lam_im
    nr = ab_re - 1.0
    ni = ab_im
    q_re = (nr * lam_re + ni * lam_im) / den
    q_im = (ni * lam_re - nr * lam_im) / den
    bb_re = q_re[..., None] * b_re - q_im[..., None] * b_im
    bb_im = q_re[..., None] * b_im + q_im[..., None] * b_re
    return ab_re, ab_im, bb_re, bb_im


HBM_SPEC = pl.BlockSpec(memory_space=pl.ANY)


def _place():
    x, y, c = lax.axis_index("x"), lax.axis_index("y"), lax.axis_index("c")
    others = [(1 - x, y), (x, 1 - y), (1 - x, 1 - y)]
    return x, y, c, 2 * x + y, others


def _half(ref, h, axis):
    n = ref.shape[axis] // 2
    idx = [slice(None)] * len(ref.shape)
    idx[axis] = pl.ds(h * n, n)
    return ref.at[tuple(idx)]


def _chip():
    return 2 * lax.axis_index("x") + lax.axis_index("y")


def _cast_place(name, w, layer):
    _, R, C = w.shape
    tr = _tile(R, max(16, (1 << 19) // C), 16)

    def body(w_ref, o_ref):
        o_ref[...] = w_ref[...].astype(o_ref.dtype)

    return pl.pallas_call(
        body, name=name, grid=(R // tr,),
        in_specs=[pl.BlockSpec((None, tr, C), lambda i: (layer, i, 0))],
        out_specs=pl.BlockSpec((None, tr, C), lambda i: (_chip(), i, 0)),
        out_shape=jax.ShapeDtypeStruct((N_CHIPS, R, C), BF16), compiler_params=_params())(w)


HBM_ONLY = pl.BlockSpec(memory_space=pltpu.HBM)
SEM_SPEC = pl.BlockSpec(memory_space=pltpu.SEMAPHORE)
EFFECT = pltpu.SideEffectType.DATAFLOW_SIDE_EFFECTING


def _split_start(name, copies_fn, n_tensors, arrays, after):
    n = len(arrays)

    def body(*refs):
        ins, send, recv, token = refs[:n], refs[n + 1], refs[n + 2], refs[-1]
        for cp in copies_fn(ins, send, recv):
            cp.start()
        token[...] = jnp.zeros_like(token)

    n_sem = (n_tensors * 3,)
    outs = pl.pallas_call(
        body, name=name,
        out_shape=(pltpu.SemaphoreType.DMA(n_sem), pltpu.SemaphoreType.DMA(n_sem),
                   *[pltpu.HBM(a.shape, a.dtype) for a in arrays], jax.ShapeDtypeStruct((SUBLANES, LANES), F32)),
        in_specs=[HBM_ONLY] * n + [HBM_SPEC],
        out_specs=(SEM_SPEC, SEM_SPEC, *[HBM_ONLY] * n, pl.BlockSpec(memory_space=pltpu.VMEM)),
        input_output_aliases={i: 2 + i for i in range(n)},
        compiler_params=pltpu.CompilerParams(has_side_effects=EFFECT),
    )(*[pltpu.with_memory_space_constraint(a, pltpu.HBM) for a in arrays], after)
    return outs[0], outs[1], list(outs[2:2 + n]), outs[-1]


def _split_wait(name, copies_fn, send, recv, arrays, after):
    n = len(arrays)

    def body(*refs):
        ins, send_ref, recv_ref = refs[:n], refs[n], refs[n + 1]
        for cp in copies_fn(ins, send_ref, recv_ref):
            cp.wait_send()
            cp.wait_recv()

    outs = pl.pallas_call(
        body, name=name, out_shape=tuple(pltpu.HBM(a.shape, a.dtype) for a in arrays),
        in_specs=[HBM_ONLY] * n + [SEM_SPEC, SEM_SPEC, HBM_SPEC], out_specs=(HBM_ONLY,) * n,
        input_output_aliases={i: i for i in range(n)},
        compiler_params=pltpu.CompilerParams(has_side_effects=EFFECT),
    )(*arrays, send, recv, after)
    return list(outs)


def _gather_copies(T):
    def copies(bufs, send, recv):
        x, y, c, me, others = _place()
        cps = []
        for t in range(T):
            mine = _half(bufs[t].at[me], c, 0)
            for k, (ox, oy) in enumerate(others):
                cps.append(pltpu.make_async_remote_copy(
                    src_ref=mine, dst_ref=mine, send_sem=send.at[3 * t + k], recv_sem=recv.at[3 * t + k],
                    device_id=(ox, oy, c), device_id_type=MESH))
        return cps
    return copies


def _xchg_copies(T):
    def copies(arrays, send, recv):
        x, y, c, me, others = _place()
        cps = []
        for t in range(T):
            for k, (ox, oy) in enumerate(others):
                cps.append(pltpu.make_async_remote_copy(
                    src_ref=arrays[t].at[2 * ox + oy], dst_ref=arrays[T + t].at[me], send_sem=send.at[3 * t + k],
                    recv_sem=recv.at[3 * t + k], device_id=(ox, oy, c), device_id_type=MESH))
        return cps
    return copies


def _gather_pass_on(name, bufs, after):
    T = len(bufs)

    def body(*refs):
        outs = refs[T + 1:2 * T + 1]
        send, recv = refs[2 * T + 1:]
        x, y, c, me, others = _place()
        cps = []
        for t in range(T):
            for k, (ox, oy) in enumerate(others):
                got = _half(outs[t].at[2 * ox + oy], c, 0)
                cp = pltpu.make_async_remote_copy(
                    src_ref=got, dst_ref=got, send_sem=send.at[t, k], recv_sem=recv.at[t, k],
                    device_id=(x, y, 1 - c), device_id_type=MESH)
                cp.start()
                cps.append(cp)
        for cp in cps:
            cp.wait()

    return pl.pallas_call(
        body, name=name, in_specs=[HBM_SPEC] * (T + 1), out_specs=[HBM_SPEC] * T,
        out_shape=[jax.ShapeDtypeStruct(b.shape, b.dtype) for b in bufs],
        input_output_aliases={t: t for t in range(T)},
        scratch_shapes=[pltpu.SemaphoreType.DMA((T, 3)), pltpu.SemaphoreType.DMA((T, 3))],
    )(*bufs, after)


def _swap_halves(name, grads):
    T = len(grads)

    def body(*refs):
        ins, theirs = refs[:T], refs[T:2 * T]
        send, recv = refs[2 * T:]
        x, y, c, me, others = _place()
        cps = []
        for t in range(T):
            cp = pltpu.make_async_remote_copy(
                src_ref=_half(ins[t], 1 - c, 1), dst_ref=theirs[t], send_sem=send.at[t], recv_sem=recv.at[t],
                device_id=(x, y, 1 - c), device_id_type=MESH)
            cp.start()
            cps.append(cp)
        for cp in cps:
            cp.wait()

    shp = [jax.ShapeDtypeStruct((g.shape[0], g.shape[1] // 2, g.shape[2]), g.dtype) for g in grads]
    return pl.pallas_call(
        body, name=name, in_specs=[HBM_SPEC] * T, out_specs=[HBM_SPEC] * T, out_shape=shp,
        scratch_shapes=[pltpu.SemaphoreType.DMA((T,)), pltpu.SemaphoreType.DMA((T,))],
    )(*grads)


def _join_halves(name, fulls):
    T = len(fulls)

    def body(*refs):
        outs = refs[T:2 * T]
        send, recv = refs[2 * T:]
        x, y, c, me, others = _place()
        cps = []
        for t in range(T):
            mine = _half(outs[t], c, 0)
            cp = pltpu.make_async_remote_copy(
                src_ref=mine, dst_ref=mine, send_sem=send.at[t], recv_sem=recv.at[t],
                device_id=(x, y, 1 - c), device_id_type=MESH)
            cp.start()
            cps.append(cp)
        for cp in cps:
            cp.wait()

    return pl.pallas_call(
        body, name=name, in_specs=[HBM_SPEC] * T, out_specs=[HBM_SPEC] * T,
        out_shape=[jax.ShapeDtypeStruct(f.shape, f.dtype) for f in fulls],
        input_output_aliases={t: t for t in range(T)},
        scratch_shapes=[pltpu.SemaphoreType.DMA((T,)), pltpu.SemaphoreType.DMA((T,))],
    )(*fulls)


def _all_reduce_small(name, buf):
    _, r, w = buf.shape

    def body(in_ref, out_ref, land, send, recv):
        x, y, c, _, _ = _place()
        me = 4 * x + 2 * y + c

        def peer(k):
            return (1 - x if k & 4 else x, 1 - y if k & 2 else y, 1 - c if k & 1 else c)

        cps = []
        for k in range(1, N_DEV):
            px, py, pc = peer(k)
            cp = pltpu.make_async_remote_copy(
                src_ref=in_ref.at[4 * px + 2 * py + pc], dst_ref=land.at[me], send_sem=send.at[k - 1],
                recv_sem=recv.at[k - 1], device_id=(px, py, pc), device_id_type=MESH)
            cp.start()
            cps.append(cp)
        land[me] = in_ref[me]
        for cp in cps:
            cp.wait()
        total = land[0]
        for d in range(1, N_DEV):
            total = total + land[d]
        out_ref[me] = total
        cps = []
        for k in range(1, N_DEV):
            cp = pltpu.make_async_remote_copy(
                src_ref=out_ref.at[me], dst_ref=out_ref.at[me], send_sem=send.at[N_DEV - 2 + k],
                recv_sem=recv.at[N_DEV - 2 + k], device_id=peer(k), device_id_type=MESH)
            cp.start()
            cps.append(cp)
        for cp in cps:
            cp.wait()

    vm = pl.BlockSpec(memory_space=pltpu.VMEM)
    return pl.pallas_call(
        body, name=name, in_specs=[vm], out_specs=vm, out_shape=jax.ShapeDtypeStruct(buf.shape, F32),
        scratch_shapes=[pltpu.VMEM(buf.shape, F32), pltpu.SemaphoreType.DMA((2 * N_DEV - 2,)),
                        pltpu.SemaphoreType.DMA((2 * N_DEV - 2,))],
        compiler_params=_params())(buf)


def _add_pairs(name, g, theirs):
    nb, H, C = theirs.shape
    th = _tile(H, max(SUBLANES * 2, (1 << 19) // C), 16)
    nh = H // th

    def body(g_ref, t_ref, p_ref, l_ref):
        s = (g_ref[...].astype(F32) + t_ref[...].astype(F32)).astype(p_ref.dtype)
        p_ref[...] = s

        @pl.when(pl.program_id(1) == _chip())
        def _():
            l_ref[...] = s

    blk = (None, th, C)
    return pl.pallas_call(
        body, name=name, grid=(nh, nb),
        in_specs=[pl.BlockSpec(blk, lambda i, j: (j, lax.axis_index("c") * nh + i, 0)),
                  pl.BlockSpec(blk, lambda i, j: (j, i, 0))],
        out_specs=[pl.BlockSpec(blk, lambda i, j: (j, i, 0)), pl.BlockSpec(blk, lambda i, j: (_chip(), i, 0))],
        out_shape=[jax.ShapeDtypeStruct(theirs.shape, theirs.dtype)] * 2, compiler_params=_params())(g, theirs)


def _sum_chips(name, parts):
    nb, H, C = parts.shape
    th = _tile(H, max(SUBLANES * 2, (1 << 19) // C), 16)
    nh = H // th

    def body(p_ref, o_ref):
        tot = p_ref[0].astype(F32)
        for b in range(1, nb):
            tot = tot + p_ref[b].astype(F32)
        o_ref[...] = tot

    return pl.pallas_call(
        body, name=name, grid=(nh,),
        in_specs=[pl.BlockSpec((nb, th, C), lambda i: (0, i, 0))],
        out_specs=pl.BlockSpec((th, C), lambda i: (lax.axis_index("c") * nh + i, 0)),
        out_shape=jax.ShapeDtypeStruct((2 * H, C), F32), compiler_params=_params())(parts)


def _adamw_math(w, g, m, v):
    m = ADAM_B1 * m + (1.0 - ADAM_B1) * g
    v = ADAM_B2 * v + (1.0 - ADAM_B2) * (g * g)
    m_hat = m / (1.0 - ADAM_B1 ** ADAM_STEP)
    v_hat = v / (1.0 - ADAM_B2 ** ADAM_STEP)
    delta = -ADAM_LR * (m_hat / (jnp.sqrt(v_hat) + ADAM_EPS) + ADAM_WD * w)
    return delta, m, v


def _adamw_big(name, w, m, v, gs):
    depth, R, C = w.shape
    tr = _tile(R, max(SUBLANES, (1 << 18) // C))

    def body(w_ref, m_ref, v_ref, *refs):
        g_refs, (go, do, mo, vo) = refs[:depth], refs[depth:]
        li = pl.program_id(0)
        g = g_refs[0][...]
        for l in range(1, depth):
            g = jnp.where(li == l, g_refs[l][...], g)
        delta, mn, vn = _adamw_math(w_ref[...], g, m_ref[...], v_ref[...])
        go[...] = g
        do[...] = delta
        mo[...] = mn
        vo[...] = vn

    spec = pl.BlockSpec((None, tr, C), lambda li, i: (li, i, 0))
    g_specs = [pl.BlockSpec((tr, C), functools.partial(lambda li, i, l: (jnp.where(li == l, i, 0), 0), l=l))
               for l in range(depth)]
    return pl.pallas_call(body, name=name, grid=(depth, R // tr), in_specs=[spec] * 3 + g_specs,
                          out_specs=[spec] * 4, out_shape=[jax.ShapeDtypeStruct(w.shape, F32)] * 4,
                          compiler_params=_params())(w, m, v, *gs)


def _adamw_small(name, w, g, m, v):
    def fn(i, n, R, B, P, N):
        return list(_adamw_math(*R)), []

    return _rowwise(name, fn, [w, g, m, v], outs=[(LANES, F32)] * 3)


def _pack(arrs, rows_mult):
    flat = jnp.concatenate([a.reshape(-1) for a in arrs])
    n = flat.shape[0]
    per = rows_mult * LANES
    pad = (-n) % per
    flat = jnp.pad(flat, (0, pad))
    return flat.reshape(-1, LANES), n


def _unpack(buf, like):
    flat = buf.reshape(-1)
    out, off = [], 0
    for a in like:
        out.append(flat[off:off + a.size].reshape(a.shape))
        off += a.size
    return out


def _ffn_fwd(tag, alpha, xf, xb, wg, wu, wd, g, b):
    def up(ps, _):
        hg, hu = ps
        return hg, hu, hg * _sigmoid(hg) * hu

    hg, hu, act = _mm_expand(f"{tag}_up", xb, [wg, wu], False, up, [], [F32, F32, BF16])

    def down(acc, ex, bc):
        y, xhat, rstd = _layer_norm(alpha * ex[0] + 0.5 * acc, bc[0], bc[1])
        return y, y, xhat, jnp.broadcast_to(rstd, (rstd.shape[0], LANES))

    D = xf.shape[1]
    yf, yb, xhat, rstd = _mm_contract(f"{tag}_down", [act], [wd], False, down, [xf], [g, b],
                                      [(D, F32), (D, BF16), (D, F32), (LANES, F32)])
    return yf, yb, (xb, hg, hu, act, xhat, rstd)


def _ln_bwd(name, dy, xhat, rstd, g, scale):
    D = dy.shape[1]

    def fn(i, n, R, B, P, N):
        d, xh, rs = R
        dr = _layer_norm_bwd(d, xh, rs[:, :1], B[0])
        return [dr, scale * dr], [jnp.sum(d * xh, axis=0, keepdims=True), jnp.sum(d, axis=0, keepdims=True)]

    return _rowwise(name, fn, [dy, xhat, rstd], bcast=[g], outs=[(D, F32), (D, BF16)], sums=[D, D])


def _ffn_bwd(tag, alpha, dy, saved, wg, wu, wd, g):
    xb, hg, hu, act, xhat, rstd = saved
    dr, dfb, dg, db = _ln_bwd(f"{tag}_ln_bwd", dy, xhat, rstd, g, 0.5)

    def dact(ps, ex):
        da, hgv, huv = ps[0], ex[0], ex[1]
        sg = _sigmoid(hgv)
        return da * huv * (sg * (1.0 + hgv * (1.0 - sg))), da * (hgv * sg)

    dhg, dhu = _mm_expand(f"{tag}_dact", dfb, [wd], True, dact, [hg, hu], [BF16, BF16])
    g_wd = _wgrad_rows(f"{tag}_gwd", act, dfb, tr=1408)
    g_wg = _wgrad_cols(f"{tag}_gwg", xb, dhg)
    g_wu = _wgrad_cols(f"{tag}_gwu", xb, dhu)

    def dxin(acc, ex, bc):
        return [alpha * ex[0] + acc]

    D = dy.shape[1]
    dx, = _mm_contract(f"{tag}_dx", [dhg, dhu], [wg, wu], True, dxin, [dr], [], [(D, F32)])
    return dx, (g_wg, g_wu, g_wd), dg, db


def _conv_taps(v, pv, i, w):
    tm = v.shape[0]
    ext = jnp.concatenate([pv * (i > 0).astype(F32), v], axis=0)
    v1 = pltpu.roll(ext, 1, 0)[SUBLANES:SUBLANES + tm]
    v2 = pltpu.roll(ext, 2, 0)[SUBLANES:SUBLANES + tm]
    return w[0:1] * v2 + w[1:2] * v1 + w[2:3] * v, v1, v2


def _mixer_fwd(tag, alpha, xf, xb, p, w_in, w_glu, w_out):
    Dh = w_in.shape[2]
    proj, = _mm_expand(f"{tag}_proj", xb, [w_in], False, lambda ps, _: ps, [], [F32])
    u, gb, gc, h = [(proj, Dh, j) for j in range(4)]
    nblk = p["wb_re"].shape[0]
    S = nblk * p["wb_re"].shape[2]
    ident = lambda acc, ex, bx: [acc]
    bu_re, = _bd_mm(f"{tag}_bu_re", [(proj, p["wb_re"])], False, ident, [], [], [F32])
    bu_im, = _bd_mm(f"{tag}_bu_im", [(proj, p["wb_im"])], False, ident, [], [], [F32])
    s_re, s_im = _scan(f"{tag}_scan", p["a_re"], p["a_im"], bu_re, bu_im)

    def yout(acc, ex, bx):
        ys = acc + bx[0] * ex[0]
        return ys, _gelu(ys), _gelu(ys)

    ys, yg, ygb = _bd_mm(f"{tag}_yout", [(s_re, p["wc_re"]), (s_im, p["wc_imn"])], False, yout, [proj], [p["d"]],
                         [F32, F32, BF16])

    def glu(acc, ex, bc):
        yy = ex[0] * _sigmoid(acc)
        return acc, yy * _rms_inv(yy) * bc[0]

    t, yn = _mm_contract(f"{tag}_glu", [ygb], [w_glu], False, glu, [yg], [p["g_s5"]], [(Dh, F32), (Dh, BF16)])

    def conv(i, n, R, B, P, N):
        gbv, gcv, hv = R
        cw, cb, gcn = B
        cv, _, _ = _conv_taps(gcv * hv, P[0] * P[1], i, cw)
        z = gbv * (cv + cb)
        return [z * _rms_inv(z) * gcn], []

    zn, = _rowwise(f"{tag}_conv", conv, [gb, gc, h], bcast=[p["conv_w"], p["conv_b"], p["g_conv"]],
                   outs=[(Dh, BF16)], prev=[1, 2])
    cat = jnp.concatenate([yn, zn], axis=1)

    def out(acc, ex, bc):
        y, xhat, rstd = _layer_norm(alpha * ex[0] + acc, bc[0], bc[1])
        return y, y, xhat, jnp.broadcast_to(rstd, (rstd.shape[0], LANES))

    D = xf.shape[1]
    yf, yb, xhat, rstd = _mm_contract(f"{tag}_out", [cat], [w_out], False, out, [xf], [p["ln2_g"], p["ln2_b"]],
                                      [(D, F32), (D, BF16), (D, F32), (LANES, F32)])
    return yf, yb, (xb, proj, s_re, s_im, ys, yg, ygb, t, cat, xhat, rstd)


def _mixer_bwd(tag, alpha, dy, saved, p, w_in, w_glu, w_out):
    xb, proj, s_re, s_im, ys, yg, ygb, t, cat, xhat, rstd = saved
    Dh = w_in.shape[2]
    D = dy.shape[1]
    gb, gc, h = [(proj, Dh, j) for j in range(1, 4)]
    dr, dmb, dg2, db2 = _ln_bwd(f"{tag}_ln_bwd", dy, xhat, rstd, p["ln2_g"], 1.0)
    dcat, = _mm_expand(f"{tag}_dcat", dmb, [w_out], True, lambda ps, _: ps, [], [F32])
    g_wout = _wgrad_rows(f"{tag}_gwout", cat, dmb)
    half = dcat.shape[1] // 2

    def conv_b1(i, n, R, B, P, N):
        dzn, gbv, gcv, hv = R
        cw, cb, gcn = B
        v = gcv * hv
        cv, v1, v2 = _conv_taps(v, P[0] * P[1], i, cw)
        cv = cv + cb
        z = gbv * cv
        rinv = _rms_inv(z)
        dz = _rms_bwd(dzn, z, rinv, gcn)
        dcv = dz * gbv
        col = lambda a: jnp.sum(a, axis=0, keepdims=True)
        return [dz * cv, dcv], [col(dzn * z * rinv), col(dcv), col(dcv * v2), col(dcv * v1), col(dcv * v)]

    dgb, dcv, dg_conv, dconv_b, dw0, dw1, dw2 = _rowwise(
        f"{tag}_conv_b1", conv_b1, [(dcat, half, 1), gb, gc, h], bcast=[p["conv_w"], p["conv_b"], p["g_conv"]],
        outs=[(Dh, BF16), (Dh, F32)], sums=[Dh] * 5, prev=[2, 3])

    def conv_b2(i, n, R, B, P, N):
        d, gcv, hv = R
        cw = B[0]
        tm = d.shape[0]
        ext = jnp.concatenate([d, N[0] * (i < n - 1).astype(F32)], axis=0)
        d1 = pltpu.roll(ext, tm + SUBLANES - 1, 0)[:tm]
        d2 = pltpu.roll(ext, tm + SUBLANES - 2, 0)[:tm]
        dv = cw[2:3] * d + cw[1:2] * d1 + cw[0:1] * d2
        return [dv * hv, dv * gcv], []

    dgc, dh = _rowwise(f"{tag}_conv_b2", conv_b2, [dcv, gc, h], bcast=[p["conv_w"]], outs=[(Dh, BF16)] * 2, nxt=[0])

    def glu_b(i, n, R, B, P, N):
        dyn, ygv, tv = R
        sg = _sigmoid(tv)
        yy = ygv * sg
        rinv = _rms_inv(yy)
        dyy = _rms_bwd(dyn, yy, rinv, B[0])
        return [dyy * ygv * sg * (1.0 - sg), dyy * sg], [jnp.sum(dyn * yy * rinv, axis=0, keepdims=True)]

    dtb, dyg0, dg_s5 = _rowwise(f"{tag}_glu_b", glu_b, [(dcat, half, 0), yg, t], bcast=[p["g_s5"]],
                                outs=[(Dh, BF16), (Dh, F32)], sums=[Dh])

    def dys_epi(ps, ex):
        return [(ps[0] + ex[0]) * _gelu_grad(ex[1])]

    dys, = _mm_expand(f"{tag}_dys", dtb, [w_glu], True, dys_epi, [dyg0, ys], [F32])
    g_wglu = _wgrad_rows(f"{tag}_gwglu", ygb, dtb)
    ident = lambda acc, ex, bx: [acc]
    ds_re, = _bd_mm(f"{tag}_ds_re", [(dys, p["wc_re"])], True, ident, [], [], [F32])
    ds_im, = _bd_mm(f"{tag}_ds_im", [(dys, p["wc_imn"])], True, ident, [], [], [F32])
    l_re, l_im, da_re, da_im = _scan(f"{tag}_rscan", p["a_re"], -p["a_im"], ds_re, ds_im, reverse=True,
                                     state=(s_re, s_im))

    def du_epi(acc, ex, bx):
        return [acc + bx[0] * ex[0]]

    du, = _bd_mm(f"{tag}_du", [(l_re, p["wb_re"]), (l_im, p["wb_im"])], True, du_epi, [dys], [p["d"]], [BF16])
    nblk = p["wb_re"].shape[0]
    u = (proj, Dh, 0)
    u_arr = proj[:, :Dh]
    g_wb_re = _bd_wgrad(f"{tag}_gwb_re", u_arr, l_re, nblk)
    g_wb_im = _bd_wgrad(f"{tag}_gwb_im", u_arr, l_im, nblk)
    g_wc_re = _bd_wgrad(f"{tag}_gwc_re", s_re, dys, nblk)
    g_wc_imn = _bd_wgrad(f"{tag}_gwc_im", s_im, dys, nblk)

    def dd_fn(i, n, R, B, P, N):
        return [], [jnp.sum(R[0] * R[1], axis=0, keepdims=True)]

    dd, = _rowwise(f"{tag}_dd", dd_fn, [dys, u], sums=[Dh])

    dproj = jnp.concatenate([du, dgb, dgc, dh], axis=1)

    def dxin(acc, ex, bc):
        return [alpha * ex[0] + acc]

    dx, = _mm_contract(f"{tag}_dx", [dproj], [w_in], True, dxin, [dr], [], [(D, F32)])
    g_win = _wgrad_cols(f"{tag}_gwin", xb, dproj)
    small = dict(ln2_g=dg2, ln2_b=db2, g_conv=dg_conv, conv_b=dconv_b,
                 conv_w=jnp.concatenate([dw0, dw1, dw2], axis=0), g_s5=dg_s5, d=dd,
                 da_re=da_re, da_im=da_im, wb_re=g_wb_re, wb_im=g_wb_im, wc_re=g_wc_re, wc_imn=g_wc_imn)
    return dx, (g_win, g_wglu, g_wout), small


def _reduce_pairs(tag, grads):
    theirs = _swap_halves(f"{tag}_swap", grads)
    added = [_add_pairs(f"{tag}_add{t}", g, b) for t, (g, b) in enumerate(zip(grads, theirs))]
    return [a[0] for a in added] + [a[1] for a in added]


def _reduce_finish(tag, lands):
    fulls = [_sum_chips(f"{tag}_sum{t}", g) for t, g in enumerate(lands)]
    return _join_halves(f"{tag}_join", fulls)


BIG = ["ffn1_gate", "ffn1_up", "ffn1_down", "w_in", "s5_w_glu", "w_out", "ffn2_gate", "ffn2_up", "ffn2_down"]
SMALL = ["ln1_g", "ln1_b", "s5_lam_re", "s5_lam_im", "s5_log_dt", "s5_b_re", "s5_b_im", "s5_c_re", "s5_c_im", "s5_d",
         "conv_w", "conv_b", "g_s5", "g_conv", "ln2_g", "ln2_b", "ln3_g", "ln3_b"]
WEIGHTS = ['ffn1_gate', 'ffn1_up', 'ffn1_down', 'ln1_g', 'ln1_b', 'w_in', 's5_lam_re', 's5_lam_im', 's5_log_dt',
           's5_b_re', 's5_b_im', 's5_c_re', 's5_c_im', 's5_d', 's5_w_glu', 'conv_w', 'conv_b', 'g_s5', 'g_conv',
           'w_out', 'ln2_g', 'ln2_b', 'ffn2_gate', 'ffn2_up', 'ffn2_down', 'ln3_g', 'ln3_b']


def _step(W, M, V, x, target):
    depth = W["ffn1_gate"].shape[0]
    alpha = (2.0 * depth) ** 0.25
    L, D = x.shape
    G, N = W["s5_lam_re"].shape[1:]
    P = W["s5_b_re"].shape[3]
    Dh = G * P
    chip = 2 * lax.axis_index("x") + lax.axis_index("y")
    cw_cols = W["conv_w"].shape[2]

    conv_w_blk = jnp.zeros((depth, CONV_W, N_CHIPS, cw_cols), F32)
    conv_w_blk = lax.dynamic_update_slice(conv_w_blk, W["conv_w"][:, :, None, :] * 0.5, (0, 0, chip, 0))
    pre, _ = _pack([conv_w_blk], SUBLANES * N_DEV)
    pre = _all_reduce_small("conv_w_gather", pre.reshape(N_DEV, -1, LANES))
    conv_w_full = pre.reshape(-1)[:conv_w_blk.size].reshape(depth, CONV_W, N_CHIPS * cw_cols)

    T = len(BIG)
    gather = _gather_copies(T)
    g_state = _split_start("gather0_start", gather, T, [_cast_place(f"place0_{n}", W[n], 0) for n in BIG], x)

    s5_vjps, lp = [], []
    for l in range(depth):
        (a_re, a_im, bb_re, bb_im), vjp = jax.vjp(_s5_discretize, W["s5_lam_re"][l], W["s5_lam_im"][l],
                                                  W["s5_log_dt"][l], W["s5_b_re"][l], W["s5_b_im"][l])
        s5_vjps.append(vjp)
        row = lambda a: a.reshape(1, -1)
        lp.append(dict(
            a_re=row(a_re), a_im=row(a_im), wb_re=_bd_build(bb_re), wb_im=_bd_build(bb_im),
            wc_re=_bd_build(W["s5_c_re"][l].transpose(0, 2, 1)).transpose(0, 2, 1),
            wc_imn=-_bd_build(W["s5_c_im"][l].transpose(0, 2, 1)).transpose(0, 2, 1),
            d=row(W["s5_d"][l]), g_s5=row(W["g_s5"][l]), g_conv=row(W["g_conv"][l]), conv_b=row(W["conv_b"][l]),
            conv_w=conv_w_full[l], ln2_g=row(W["ln2_g"][l]), ln2_b=row(W["ln2_b"][l])))

    def cast(i, n, R, B, P_, N_):
        return [R[0]], []

    xb, = _rowwise("cast_x", cast, [x], outs=[(D, BF16)])
    xf = x
    saved, gathered = [], []
    for l in range(depth):
        bufs = _split_wait(f"gather{l}_wait", gather, g_state[0], g_state[1], g_state[2], xb)
        token = bufs[0]
        if l + 1 < depth:
            nxt = [_cast_place(f"place{l + 1}_{n}", W[n], l + 1) for n in BIG]
            g_state = _split_start(f"gather{l + 1}_start", gather, T, nxt, bufs[0])
            token = g_state[3]
        gathered.append(_gather_pass_on(f"gather{l}_pass", bufs, token))
        wg1, wu1, wd1, w_in, w_glu, w_out, wg2, wu2, wd2 = gathered[l]
        row = lambda a: a.reshape(1, -1)
        xf, xb, s1 = _ffn_fwd(f"l{l}_ffn1", alpha, xf, xb, wg1, wu1, wd1, row(W["ln1_g"][l]), row(W["ln1_b"][l]))
        xf, xb, s2 = _mixer_fwd(f"l{l}_mix", alpha, xf, xb, lp[l], w_in, w_glu, w_out)
        xf, xb, s3 = _ffn_fwd(f"l{l}_ffn2", alpha, xf, xb, wg2, wu2, wd2, row(W["ln3_g"][l]), row(W["ln3_b"][l]))
        saved.append((s1, s2, s3))

    def loss_fn(i, n, R, B, P_, N_):
        err = R[0] - R[1]
        return [err * (1.0 / D)], [jnp.sum(0.5 * err * err * (1.0 / D), axis=0, keepdims=True)]

    dx, loss_cols = _rowwise("loss", loss_fn, [xf, target], outs=[(D, F32)], sums=[D])
    loss = lax.psum(jnp.sum(loss_cols), ("x", "y", "c"))

    big_g = {n: [None] * depth for n in BIG}
    small_g = {n: [None] * depth for n in SMALL}
    xchg = _xchg_copies(T)
    x_state, tok = None, 0.0

    def finish(layer, lands):
        for n, r in zip(BIG, _reduce_finish(f"l{layer}_red", lands)):
            big_g[n][layer] = r

    for l in reversed(range(depth)):
        wg1, wu1, wd1, w_in, w_glu, w_out, wg2, wu2, wd2 = gathered[l]
        s1, s2, s3 = saved[l]
        row = lambda a: a.reshape(1, -1)
        dx, (g_wg2, g_wu2, g_wd2), dg3, db3 = _ffn_bwd(f"l{l}_ffn2", alpha, dx, s3, wg2, wu2, wd2,
                                                      row(W["ln3_g"][l]) + tok)
        dx, (g_win, g_wglu, g_wout), sm = _mixer_bwd(f"l{l}_mix", alpha, dx, s2, lp[l], w_in, w_glu, w_out)
        dx, (g_wg1, g_wu1, g_wd1), dg1, db1 = _ffn_bwd(f"l{l}_ffn1", alpha, dx, s1, wg1, wu1, wd1, row(W["ln1_g"][l]))
        arrays = _reduce_pairs(f"l{l}_red", [g_wg1, g_wu1, g_wd1, g_win, g_wglu, g_wout, g_wg2, g_wu2, g_wd2])
        after, prev = arrays[0], None
        if x_state is not None:
            prev = _split_wait(f"l{l + 1}_xchg_wait", xchg, x_state[0], x_state[1], x_state[2], arrays[0])
            after = prev[T]
        x_state = _split_start(f"l{l}_xchg_start", xchg, T, arrays, after)
        tok = x_state[3][0, 0]
        if prev is not None:
            finish(l + 1, prev[T:])
        d_bb_re = _bd_extract(sm["wb_re"], G, N, P)
        d_bb_im = _bd_extract(sm["wb_im"], G, N, P)
        d_lre, d_lim, d_ldt, d_bre, d_bim = s5_vjps[l]((sm["da_re"].reshape(G, N), sm["da_im"].reshape(G, N),
                                                        d_bb_re, d_bb_im))
        d_cre = _bd_extract(sm["wc_re"].transpose(0, 2, 1), G, N, P).transpose(0, 2, 1)
        d_cim = -_bd_extract(sm["wc_imn"].transpose(0, 2, 1), G, N, P).transpose(0, 2, 1)
        vals = dict(ln1_g=dg1, ln1_b=db1, s5_lam_re=d_lre, s5_lam_im=d_lim, s5_log_dt=d_ldt, s5_b_re=d_bre,
                    s5_b_im=d_bim, s5_c_re=d_cre, s5_c_im=d_cim, s5_d=sm["d"], conv_w=sm["conv_w"],
                    conv_b=sm["conv_b"], g_s5=sm["g_s5"], g_conv=sm["g_conv"], ln2_g=sm["ln2_g"], ln2_b=sm["ln2_b"],
                    ln3_g=dg3, ln3_b=db3)
        for n in SMALL:
            small_g[n][l] = vals[n].reshape((W[n].shape[1:] if n != "conv_w" else (CONV_W, N_CHIPS * cw_cols)))
    grad_x = dx

    sg = [jnp.stack(small_g[n]) for n in SMALL]
    packed, _ = _pack(sg, SUBLANES * N_DEV)
    packed = _all_reduce_small("small_allreduce", packed.reshape(N_DEV, -1, LANES)).reshape(-1, LANES)
    last = _split_wait("l0_xchg_wait", xchg, x_state[0], x_state[1], x_state[2], packed)
    finish(0, last[T:])
    sg = dict(zip(SMALL, _unpack(packed, sg)))
    sg["conv_w"] = lax.dynamic_slice_in_dim(sg["conv_w"], chip * cw_cols, cw_cols, axis=2)
    gp, _ = _pack([sg[n] for n in SMALL], SUBLANES)
    wp, _ = _pack([W[n] for n in SMALL], SUBLANES)
    mp, _ = _pack([M[n] for n in SMALL], SUBLANES)
    vp, _ = _pack([V[n] for n in SMALL], SUBLANES)
    like = [W[n] for n in SMALL]
    dsm, msm, vsm = [dict(zip(SMALL, _unpack(a, like))) for a in _adamw_small("adamw_small", wp, gp, mp, vp)]

    grads, deltas, new_m, new_v = dict(sg), dsm, msm, vsm
    for n in BIG:
        shp = W[n].shape
        flat = lambda a: a.reshape(shp[0], shp[1], shp[2])
        gr, de, mn, vn = _adamw_big(f"adamw_{n}", flat(W[n]), flat(M[n]), flat(V[n]), big_g[n])
        grads[n], deltas[n], new_m[n], new_v[n] = gr, de, mn, vn

    outs = [loss, grad_x[None]]
    for d in (grads, deltas, new_m, new_v):
        outs += [d[n] for n in WEIGHTS]
    return tuple(outs)


def kernel(x, ffn1_gate, ffn1_up, ffn1_down, ln1_g, ln1_b, w_in, s5_lam_re, s5_lam_im, s5_log_dt, s5_b_re, s5_b_im, s5_c_re, s5_c_im, s5_d, s5_w_glu, conv_w, conv_b, g_s5, g_conv, w_out, ln2_g, ln2_b, ffn2_gate, ffn2_up, ffn2_down, ln3_g, ln3_b, loss_target, m_ffn1_gate, m_ffn1_up, m_ffn1_down, m_ln1_g, m_ln1_b, m_w_in, m_s5_lam_re, m_s5_lam_im, m_s5_log_dt, m_s5_b_re, m_s5_b_im, m_s5_c_re, m_s5_c_im, m_s5_d, m_s5_w_glu, m_conv_w, m_conv_b, m_g_s5, m_g_conv, m_w_out, m_ln2_g, m_ln2_b, m_ffn2_gate, m_ffn2_up, m_ffn2_down, m_ln3_g, m_ln3_b, v_ffn1_gate, v_ffn1_up, v_ffn1_down, v_ln1_g, v_ln1_b, v_w_in, v_s5_lam_re, v_s5_lam_im, v_s5_log_dt, v_s5_b_re, v_s5_b_im, v_s5_c_re, v_s5_c_im, v_s5_d, v_s5_w_glu, v_conv_w, v_conv_b, v_g_s5, v_g_conv, v_w_out, v_ln2_g, v_ln2_b, v_ffn2_gate, v_ffn2_up, v_ffn2_down, v_ln3_g, v_ln3_b):
    a = dict(locals())
    W = {n: a[n] for n in WEIGHTS}
    M = {n: a["m_" + n] for n in WEIGHTS}
    V = {n: a["v_" + n] for n in WEIGHTS}
    return _step(W, M, V, x[0], loss_target[0])
```

```python
import functools

import jax
import jax.numpy as jnp
from jax import lax
from jax.experimental import pallas as pl
from jax.experimental.pallas import tpu as pltpu

F32 = jnp.float32
BF16 = jnp.bfloat16
MESH = pl.DeviceIdType.MESH
HIGH = lax.Precision.HIGHEST

N_CHIPS = 4
N_DEV = 8
V7X_VMEM_LIMIT = 56 * 1024 * 1024
SUBLANES = 8
LANES = 128
S5_P = 16
S5_N = 64
S5_GB = 8
CONV_W = 3
LN_EPS = 1e-5
RMS_EPS = 1e-6
ADAM_LR = 0.001
ADAM_B1 = 0.9
ADAM_B2 = 0.999
ADAM_EPS = 1e-08
ADAM_WD = 0.01
ADAM_STEP = 10
GELU_K = 0.7978845608028654
GELU_C = 0.044715


def _params():
    return pltpu.CompilerParams(vmem_limit_bytes=V7X_VMEM_LIMIT)


def _tile(n, pref, mult=SUBLANES):
    best = None
    for t in range(mult, min(n, pref) + 1, mult):
        if n % t == 0:
            best = t
    return best if best is not None else n


def _mm(a, b, precision=None):
    return jnp.dot(a, b, preferred_element_type=F32, precision=precision)


def _mm_nt(a, b, precision=None):
    return lax.dot_general(a, b, (((1,), (1,)), ((), ())), preferred_element_type=F32, precision=precision)


def _mm_tn(a, b, precision=None):
    return lax.dot_general(a, b, (((0,), (0,)), ((), ())), preferred_element_type=F32, precision=precision)


def _sigmoid(x):
    return 1.0 / (1.0 + jnp.exp(-x))


def _gelu(x):
    return 0.5 * x * (1.0 + jnp.tanh(GELU_K * (x + GELU_C * x * x * x)))


def _gelu_grad(x):
    th = jnp.tanh(GELU_K * (x + GELU_C * x * x * x))
    return 0.5 * (1.0 + th) + 0.5 * x * (1.0 - th * th) * GELU_K * (1.0 + 3.0 * GELU_C * x * x)


def _layer_norm(r, g, b):
    mu = jnp.mean(r, axis=-1, keepdims=True)
    xc = r - mu
    rstd = lax.rsqrt(jnp.mean(xc * xc, axis=-1, keepdims=True) + LN_EPS)
    xhat = xc * rstd
    return xhat * g + b, xhat, rstd


def _layer_norm_bwd(dy, xhat, rstd, g):
    dxh = dy * g
    m1 = jnp.mean(dxh, axis=-1, keepdims=True)
    m2 = jnp.mean(dxh * xhat, axis=-1, keepdims=True)
    return rstd * (dxh - m1 - xhat * m2)


def _rms_inv(x):
    return lax.rsqrt(jnp.mean(x * x, axis=-1, keepdims=True) + RMS_EPS)


def _rms_bwd(dy, x, rinv, g):
    dxh = dy * g
    return rinv * dxh - x * (rinv * rinv * rinv) * jnp.mean(dxh * x, axis=-1, keepdims=True)


def _rowwise(name, fn, rows, bcast=(), outs=(), sums=(), prev=(), nxt=(), tm=256):
    rows = [r if isinstance(r, tuple) else (r, r.shape[1], 0) for r in rows]
    L = rows[0][0].shape[0]
    tm = _tile(L, tm)
    n = L // tm
    hb = tm // SUBLANES
    nh = L // SUBLANES
    nr, nb, npv, nnx, no, ns = len(rows), len(bcast), len(prev), len(nxt), len(outs), len(sums)

    def body(*refs):
        i = pl.program_id(0)
        k = 0
        R = [r[...] for r in refs[k:k + nr]]; k += nr
        B = [r[...] for r in refs[k:k + nb]]; k += nb
        P = [r[...] for r in refs[k:k + npv]]; k += npv
        N = [r[...] for r in refs[k:k + nnx]]; k += nnx
        o_refs = refs[k:k + no]; k += no
        s_refs = refs[k:k + ns]
        O, S = fn(i, n, R, B, P, N)
        for ref, val in zip(o_refs, O):
            ref[...] = val.astype(ref.dtype)
        if ns:
            @pl.when(i == 0)
            def _():
                for ref in s_refs:
                    ref[...] = jnp.zeros_like(ref)
            for ref, val in zip(s_refs, S):
                ref[...] += val

    in_specs = [pl.BlockSpec((tm, w), functools.partial(lambda i, cb: (i, cb), cb=cb)) for _, w, cb in rows]
    in_specs += [pl.BlockSpec(b.shape, lambda i: (0, 0)) for b in bcast]
    in_specs += [pl.BlockSpec((SUBLANES, rows[j][1]),
                              functools.partial(lambda i, cb: (jnp.maximum(i * hb - 1, 0), cb), cb=rows[j][2]))
                 for j in prev]
    in_specs += [pl.BlockSpec((SUBLANES, rows[j][1]),
                              functools.partial(lambda i, cb: (jnp.minimum((i + 1) * hb, nh - 1), cb), cb=rows[j][2]))
                 for j in nxt]
    out_specs = [pl.BlockSpec((tm, w), lambda i: (i, 0)) for w, _ in outs]
    out_specs += [pl.BlockSpec((1, w), lambda i: (0, 0)) for w in sums]
    out_shape = [jax.ShapeDtypeStruct((L, w), dt) for w, dt in outs]
    out_shape += [jax.ShapeDtypeStruct((1, w), F32) for w in sums]
    args = [r[0] for r in rows] + list(bcast) + [rows[j][0] for j in prev] + [rows[j][0] for j in nxt]
    return pl.pallas_call(body, name=name, grid=(n,), in_specs=in_specs, out_specs=out_specs,
                          out_shape=out_shape, compiler_params=_params())(*args)


def _mm_expand(name, a, ws, nt, epi, extras, outs, tm=512):
    L, K = a.shape
    tm = _tile(L, tm)
    nw, ne = len(ws), len(extras)
    co = ws[0].shape[1] if nt else ws[0].shape[2]

    def body(a_ref, *refs):
        av = a_ref[...]
        ps = [(_mm_nt if nt else _mm)(av, w[...]) for w in refs[:nw]]
        vals = epi(ps, [e[...] for e in refs[nw:nw + ne]])
        for ref, val in zip(refs[nw + ne:], vals):
            ref[...] = val.astype(ref.dtype)

    in_specs = [pl.BlockSpec((tm, K), lambda j, i: (i, 0))]
    in_specs += [pl.BlockSpec((None,) + w.shape[1:], lambda j, i: (j, 0, 0)) for w in ws]
    in_specs += [pl.BlockSpec((tm, co), lambda j, i: (i, j)) for _ in extras]
    out_specs = [pl.BlockSpec((tm, co), lambda j, i: (i, j)) for _ in outs]
    out_shape = [jax.ShapeDtypeStruct((L, N_CHIPS * co), dt) for dt in outs]
    return pl.pallas_call(body, name=name, grid=(N_CHIPS, L // tm), in_specs=in_specs, out_specs=out_specs,
                          out_shape=out_shape, compiler_params=_params())(a, *ws, *extras)


def _mm_contract(name, as_, ws, nt, epi, extras, bcast, outs, tm=512):
    L = as_[0].shape[0]
    tm = _tile(L, tm)
    na, ne, nb = len(as_), len(extras), len(bcast)
    cb = as_[0].shape[1] // N_CHIPS
    n_out = ws[0].shape[1] if nt else ws[0].shape[2]

    def body(*refs):
        k = pl.program_id(1)
        a_refs, w_refs = refs[:na], refs[na:2 * na]
        e_refs = refs[2 * na:2 * na + ne]
        b_refs = refs[2 * na + ne:2 * na + ne + nb]
        o_refs = refs[2 * na + ne + nb:-1]
        acc = refs[-1]

        @pl.when(k == 0)
        def _():
            acc[...] = jnp.zeros_like(acc)

        part = None
        for a_ref, w_ref in zip(a_refs, w_refs):
            p = (_mm_nt if nt else _mm)(a_ref[...], w_ref[...])
            part = p if part is None else part + p
        acc[...] += part

        @pl.when(k == N_CHIPS - 1)
        def _():
            vals = epi(acc[...], [e[...] for e in e_refs], [b[...] for b in b_refs])
            for ref, val in zip(o_refs, vals):
                ref[...] = val.astype(ref.dtype)

    in_specs = [pl.BlockSpec((tm, cb), lambda i, k: (i, k)) for _ in as_]
    in_specs += [pl.BlockSpec((None,) + w.shape[1:], lambda i, k: (k, 0, 0)) for w in ws]
    in_specs += [pl.BlockSpec((tm, e.shape[1]), lambda i, k: (i, 0)) for e in extras]
    in_specs += [pl.BlockSpec(b.shape, lambda i, k: (0, 0)) for b in bcast]
    out_specs = [pl.BlockSpec((tm, w), lambda i, k: (i, 0)) for w, _ in outs]
    out_shape = [jax.ShapeDtypeStruct((L, w), dt) for w, dt in outs]
    return pl.pallas_call(body, name=name, grid=(L // tm, N_CHIPS), in_specs=in_specs, out_specs=out_specs,
                          out_shape=out_shape, scratch_shapes=[pltpu.VMEM((tm, n_out), F32)],
                          compiler_params=_params())(*as_, *ws, *extras, *bcast)


def _wgrad_cols(name, a, b, tk=1024):
    L, K = a.shape
    C = b.shape[1] // N_CHIPS
    tk = _tile(K, tk, LANES)

    def body(a_ref, b_ref, o_ref):
        o_ref[...] = _mm_tn(a_ref[...], b_ref[...]).astype(o_ref.dtype)

    return pl.pallas_call(
        body, name=name, grid=(N_CHIPS, K // tk),
        in_specs=[pl.BlockSpec((L, tk), lambda j, kb: (0, kb)), pl.BlockSpec((L, C), lambda j, kb: (0, j))],
        out_specs=pl.BlockSpec((None, tk, C), lambda j, kb: (j, kb, 0)),
        out_shape=jax.ShapeDtypeStruct((N_CHIPS, K, C), BF16), compiler_params=_params())(a, b)


def _wgrad_rows(name, a, b, tr=512):
    L, N = b.shape
    R = a.shape[1] // N_CHIPS
    tr = _tile(R, tr, LANES)
    nrb = R // tr

    def body(a_ref, b_ref, o_ref):
        o_ref[...] = _mm_tn(a_ref[...], b_ref[...]).astype(o_ref.dtype)

    return pl.pallas_call(
        body, name=name, grid=(N_CHIPS, nrb),
        in_specs=[pl.BlockSpec((L, tr), lambda j, rb: (0, j * nrb + rb)), pl.BlockSpec((L, N), lambda j, rb: (0, 0))],
        out_specs=pl.BlockSpec((None, tr, N), lambda j, rb: (j, rb, 0)),
        out_shape=jax.ShapeDtypeStruct((N_CHIPS, R, N), BF16), compiler_params=_params())(a, b)


def _bd_mm(name, pairs, nt, epi, extras, bex, outs, tm=1024):
    L = pairs[0][0].shape[0]
    nblk = pairs[0][1].shape[0]
    tm = _tile(L, tm)
    npair, ne, nx = len(pairs), len(extras), len(bex)
    w0 = pairs[0][1]
    ca, co = (w0.shape[2], w0.shape[1]) if nt else (w0.shape[1], w0.shape[2])

    def body(*refs):
        ps = [(_mm_nt if nt else _mm)(refs[2 * q][...], refs[2 * q + 1][...], HIGH) for q in range(npair)]
        k = 2 * npair
        vals = epi(ps, [e[...] for e in refs[k:k + ne]], [e[...] for e in refs[k + ne:k + ne + nx]])
        for ref, val in zip(refs[k + ne + nx:], vals):
            ref[...] = val.astype(ref.dtype)

    in_specs = []
    args = []
    for a, w in pairs:
        in_specs += [pl.BlockSpec((tm, ca), lambda i, g: (i, g)), pl.BlockSpec((None,) + w.shape[1:], lambda i, g: (g, 0, 0))]
        args += [a, w]
    in_specs += [pl.BlockSpec((tm, co), lambda i, g: (i, g)) for _ in extras]
    in_specs += [pl.BlockSpec((1, co), lambda i, g: (0, g)) for _ in bex]
    out_specs = [pl.BlockSpec((tm, co), lambda i, g: (i, g)) for _ in outs]
    out_shape = [jax.ShapeDtypeStruct((L, nblk * co), dt) for dt in outs]
    return pl.pallas_call(body, name=name, grid=(L // tm, nblk), in_specs=in_specs, out_specs=out_specs,
                          out_shape=out_shape, compiler_params=_params())(*args, *extras, *bex)


def _bd_wgrad(name, a, b, nblk):
    L = a.shape[0]
    ca, cb = a.shape[1] // nblk, b.shape[1] // nblk

    def body(a_ref, b_ref, o_ref):
        o_ref[...] = _mm_tn(a_ref[...], b_ref[...], HIGH)

    return pl.pallas_call(
        body, name=name, grid=(nblk,),
        in_specs=[pl.BlockSpec((L, ca), lambda g: (0, g)), pl.BlockSpec((L, cb), lambda g: (0, g))],
        out_specs=pl.BlockSpec((None, ca, cb), lambda g: (g, 0, 0)),
        out_shape=jax.ShapeDtypeStruct((nblk, ca, cb), F32), compiler_params=_params())(a, b)


def _cmul(ar, ai, br, bi):
    return ar * br - ai * bi, ar * bi + ai * br


def _scan(name, ar, ai, xr, xi, reverse=False, state=None, tc=256):
    L, S = xr.shape
    tc = _tile(S, tc, LANES)
    nt = L // SUBLANES
    with_da = state is not None

    def body(*refs):
        ar_ref, ai_ref, xr_ref, xi_ref = refs[:4]
        if with_da:
            sr_ref, si_ref, yr_ref, yi_ref, dar_ref, dai_ref = refs[4:]
        else:
            yr_ref, yi_ref = refs[4:]
        a1 = (ar_ref[...], ai_ref[...])
        pw = [a1]
        for _ in range(SUBLANES - 1):
            pw.append(_cmul(*pw[-1], *a1))
        row = lax.broadcasted_iota(jnp.int32, (SUBLANES, tc), 0)
        tr = jnp.zeros((SUBLANES, tc), F32)
        ti = jnp.zeros((SUBLANES, tc), F32)
        for t in range(SUBLANES):
            p = pw[SUBLANES - 1 - t] if reverse else pw[t]
            tr = jnp.where(row == t, p[0], tr)
            ti = jnp.where(row == t, p[1], ti)

        def step(n, carry):
            idx = (nt - 1 - n) if reverse else n
            rows = pl.ds(pl.multiple_of(idx * SUBLANES, SUBLANES), SUBLANES)
            vr, vi = xr_ref[rows, :], xi_ref[rows, :]
            for d in (1, 2, 4):
                pr, pi = pw[d - 1]
                if reverse:
                    qr, qi = pltpu.roll(vr, SUBLANES - d, 0), pltpu.roll(vi, SUBLANES - d, 0)
                    keep = row < SUBLANES - d
                else:
                    qr, qi = pltpu.roll(vr, d, 0), pltpu.roll(vi, d, 0)
                    keep = row >= d
                mr, mi = _cmul(pr, pi, qr, qi)
                vr = vr + jnp.where(keep, mr, 0.0)
                vi = vi + jnp.where(keep, mi, 0.0)
            cr, ci = carry[0], carry[1]
            mr, mi = _cmul(tr, ti, cr, ci)
            vr, vi = vr + mr, vi + mi
            yr_ref[rows, :] = vr
            yi_ref[rows, :] = vi
            edge = 0 if reverse else SUBLANES - 1
            new = (vr[edge:edge + 1, :], vi[edge:edge + 1, :])
            if not with_da:
                return new
            pidx = jnp.maximum(idx - 1, 0)
            prows = pl.ds(pl.multiple_of(pidx * SUBLANES, SUBLANES), SUBLANES)
            live = (idx > 0).astype(F32)
            s0r = sr_ref[prows, :][SUBLANES - 1:SUBLANES, :] * live
            s0i = si_ref[prows, :][SUBLANES - 1:SUBLANES, :] * live
            spr = jnp.where(row == 0, s0r, pltpu.roll(sr_ref[rows, :], 1, 0))
            spi = jnp.where(row == 0, s0i, pltpu.roll(si_ref[rows, :], 1, 0))
            return new + (carry[2] + vr * spr + vi * spi, carry[3] + vi * spr - vr * spi)

        zero = jnp.zeros((1, tc), F32)
        init = (zero, zero)
        if with_da:
            acc0 = jnp.zeros((SUBLANES, tc), F32)
            init = init + (acc0, acc0)
        fin = lax.fori_loop(0, nt, step, init, unroll=2)
        if with_da:
            dar_ref[...] = jnp.sum(fin[2], axis=0, keepdims=True)
            dai_ref[...] = jnp.sum(fin[3], axis=0, keepdims=True)

    col = pl.BlockSpec((L, tc), lambda j: (0, j))
    vec = pl.BlockSpec((1, tc), lambda j: (0, j))
    n_in = 6 if with_da else 4
    in_specs = [vec, vec] + [col] * (n_in - 2)
    out_specs = [col, col] + ([vec, vec] if with_da else [])
    out_shape = [jax.ShapeDtypeStruct((L, S), F32)] * 2 + ([jax.ShapeDtypeStruct((1, S), F32)] * 2 if with_da else [])
    args = (ar, ai, xr, xi) + (tuple(state) if with_da else ())
    return pl.pallas_call(body, name=name, grid=(S // tc,), in_specs=in_specs, out_specs=out_specs,
                          out_shape=out_shape, compiler_params=_params())(*args)


def _bd_build(w_gnp):
    G, N, P = w_gnp.shape
    nb = G // S5_GB
    eye = jnp.eye(S5_GB, dtype=F32)
    x = w_gnp.reshape(nb, S5_GB, N, P).transpose(0, 1, 3, 2)
    w = x[:, :, :, None, :] * eye[None, :, None, :, None]
    return w.reshape(nb, S5_GB * P, S5_GB * N)


def _bd_extract(w, G, N, P):
    nb = G // S5_GB
    eye = jnp.eye(S5_GB, dtype=F32)
    w5 = w.reshape(nb, S5_GB, P, S5_GB, N)
    d = jnp.sum(w5 * eye[None, :, None, :, None], axis=3)
    return d.transpose(0, 1, 3, 2).reshape(G, N, P)


def _s5_discretize(lam_re, lam_im, log_dt, b_re, b_im):
    dt = jnp.exp(log_dt)[:, None]
    mag = jnp.exp(lam_re * dt)
    ang = lam_im * dt
    ab_re = mag * jnp.cos(ang)
    ab_im = mag * jnp.sin(ang)
    den = lam_re * lam_re + lam_im * lam_im
    nr = ab_re - 1.0
    ni = ab_im
    q_re = (nr * lam_re + ni * lam_im) / den
    q_im = (ni * lam_re - nr * lam_im) / den
    bb_re = q_re[..., None] * b_re - q_im[..., None] * b_im
    bb_im = q_re[..., None] * b_im + q_im[..., None] * b_re
    return ab_re, ab_im, bb_re, bb_im


HBM_SPEC = pl.BlockSpec(memory_space=pl.ANY)


def _place():
    x, y, c = lax.axis_index("x"), lax.axis_index("y"), lax.axis_index("c")
    others = [(1 - x, y), (x, 1 - y), (1 - x, 1 - y)]
    return x, y, c, 2 * x + y, others


def _half(ref, h, axis):
    n = ref.shape[axis] // 2
    idx = [slice(None)] * len(ref.shape)
    idx[axis] = pl.ds(h * n, n)
    return ref.at[tuple(idx)]


def _chip():
    return 2 * lax.axis_index("x") + lax.axis_index("y")


def _cast_place(name, w, layer):
    _, R, C = w.shape
    tr = _tile(R, max(16, (1 << 19) // C), 16)

    def body(w_ref, o_ref):
        o_ref[...] = w_ref[...].astype(o_ref.dtype)

    return pl.pallas_call(
        body, name=name, grid=(R // tr,),
        in_specs=[pl.BlockSpec((None, tr, C), lambda i: (layer, i, 0))],
        out_specs=pl.BlockSpec((None, tr, C), lambda i: (_chip(), i, 0)),
        out_shape=jax.ShapeDtypeStruct((N_CHIPS, R, C), BF16), compiler_params=_params())(w)


HBM_ONLY = pl.BlockSpec(memory_space=pltpu.HBM)
SEM_SPEC = pl.BlockSpec(memory_space=pltpu.SEMAPHORE)
EFFECT = pltpu.SideEffectType.DATAFLOW_SIDE_EFFECTING


def _split_start(name, copies_fn, n_tensors, arrays, after):
    n = len(arrays)

    def body(*refs):
        ins, send, recv, token = refs[:n], refs[n + 1], refs[n + 2], refs[-1]
        for cp in copies_fn(ins, send, recv):
            cp.start()
        token[...] = jnp.zeros_like(token)

    n_sem = (n_tensors * 3,)
    outs = pl.pallas_call(
        body, name=name,
        out_shape=(pltpu.SemaphoreType.DMA(n_sem), pltpu.SemaphoreType.DMA(n_sem),
                   *[pltpu.HBM(a.shape, a.dtype) for a in arrays], jax.ShapeDtypeStruct((SUBLANES, LANES), F32)),
        in_specs=[HBM_ONLY] * n + [HBM_SPEC],
        out_specs=(SEM_SPEC, SEM_SPEC, *[HBM_ONLY] * n, pl.BlockSpec(memory_space=pltpu.VMEM)),
        input_output_aliases={i: 2 + i for i in range(n)},
        compiler_params=pltpu.CompilerParams(has_side_effects=EFFECT),
    )(*[pltpu.with_memory_space_constraint(a, pltpu.HBM) for a in arrays], after)
    return outs[0], outs[1], list(outs[2:2 + n]), outs[-1]


def _split_wait(name, copies_fn, send, recv, arrays, after):
    n = len(arrays)

    def body(*refs):
        ins, send_ref, recv_ref = refs[:n], refs[n], refs[n + 1]
        for cp in copies_fn(ins, send_ref, recv_ref):
            cp.wait_send()
            cp.wait_recv()

    outs = pl.pallas_call(
        body, name=name, out_shape=tuple(pltpu.HBM(a.shape, a.dtype) for a in arrays),
        in_specs=[HBM_ONLY] * n + [SEM_SPEC, SEM_SPEC, HBM_SPEC], out_specs=(HBM_ONLY,) * n,
        input_output_aliases={i: i for i in range(n)},
        compiler_params=pltpu.CompilerParams(has_side_effects=EFFECT),
    )(*arrays, send, recv, after)
    return list(outs)


def _gather_copies(T):
    def copies(bufs, send, recv):
        x, y, c, me, others = _place()
        cps = []
        for t in range(T):
            mine = _half(bufs[t].at[me], c, 0)
            for k, (ox, oy) in enumerate(others):
                cps.append(pltpu.make_async_remote_copy(
                    src_ref=mine, dst_ref=mine, send_sem=send.at[3 * t + k], recv_sem=recv.at[3 * t + k],
                    device_id=(ox, oy, c), device_id_type=MESH))
        return cps
    return copies


def _xchg_copies(T):
    def copies(arrays, send, recv):
        x, y, c, me, others = _place()
        cps = []
        for t in range(T):
            for k, (ox, oy) in enumerate(others):
                cps.append(pltpu.make_async_remote_copy(
                    src_ref=arrays[t].at[2 * ox + oy], dst_ref=arrays[T + t].at[me], send_sem=send.at[3 * t + k],
                    recv_sem=recv.at[3 * t + k], device_id=(ox, oy, c), device_id_type=MESH))
        return cps
    return copies


def _gather_pass_on(name, bufs, after):
    T = len(bufs)

    def body(*refs):
        outs = refs[T + 1:2 * T + 1]
        send, recv = refs[2 * T + 1:]
        x, y, c, me, others = _place()
        cps = []
        for t in range(T):
            for k, (ox, oy) in enumerate(others):
                got = _half(outs[t].at[2 * ox + oy], c, 0)
                cp = pltpu.make_async_remote_copy(
                    src_ref=got, dst_ref=got, send_sem=send.at[t, k], recv_sem=recv.at[t, k],
                    device_id=(x, y, 1 - c), device_id_type=MESH)
                cp.start()
                cps.append(cp)
        for cp in cps:
            cp.wait()

    return pl.pallas_call(
        body, name=name, in_specs=[HBM_SPEC] * (T + 1), out_specs=[HBM_SPEC] * T,
        out_shape=[jax.ShapeDtypeStruct(b.shape, b.dtype) for b in bufs],
        input_output_aliases={t: t for t in range(T)},
        scratch_shapes=[pltpu.SemaphoreType.DMA((T, 3)), pltpu.SemaphoreType.DMA((T, 3))],
    )(*bufs, after)


def _swap_halves(name, grads):
    T = len(grads)

    def body(*refs):
        ins, theirs = refs[:T], refs[T:2 * T]
        send, recv = refs[2 * T:]
        x, y, c, me, others = _place()
        cps = []
        for t in range(T):
            cp = pltpu.make_async_remote_copy(
                src_ref=_half(ins[t], 1 - c, 1), dst_ref=theirs[t], send_sem=send.at[t], recv_sem=recv.at[t],
                device_id=(x, y, 1 - c), device_id_type=MESH)
            cp.start()
            cps.append(cp)
        for cp in cps:
            cp.wait()

    shp = [jax.ShapeDtypeStruct((g.shape[0], g.shape[1] // 2, g.shape[2]), g.dtype) for g in grads]
    return pl.pallas_call(
        body, name=name, in_specs=[HBM_SPEC] * T, out_specs=[HBM_SPEC] * T, out_shape=shp,
        scratch_shapes=[pltpu.SemaphoreType.DMA((T,)), pltpu.SemaphoreType.DMA((T,))],
    )(*grads)


def _join_halves(name, fulls):
    T = len(fulls)

    def body(*refs):
        outs = refs[T:2 * T]
        send, recv = refs[2 * T:]
        x, y, c, me, others = _place()
        cps = []
        for t in range(T):
            mine = _half(outs[t], c, 0)
            cp = pltpu.make_async_remote_copy(
                src_ref=mine, dst_ref=mine, send_sem=send.at[t], recv_sem=recv.at[t],
                device_id=(x, y, 1 - c), device_id_type=MESH)
            cp.start()
            cps.append(cp)
        for cp in cps:
            cp.wait()

    return pl.pallas_call(
        body, name=name, in_specs=[HBM_SPEC] * T, out_specs=[HBM_SPEC] * T,
        out_shape=[jax.ShapeDtypeStruct(f.shape, f.dtype) for f in fulls],
        input_output_aliases={t: t for t in range(T)},
        scratch_shapes=[pltpu.SemaphoreType.DMA((T,)), pltpu.SemaphoreType.DMA((T,))],
    )(*fulls)


def _all_reduce_small(name, buf):
    _, r, w = buf.shape

    def body(in_ref, out_ref, land, send, recv):
        x, y, c, _, _ = _place()
        me = 4 * x + 2 * y + c

        def peer(k):
            return (1 - x if k & 4 else x, 1 - y if k & 2 else y, 1 - c if k & 1 else c)

        cps = []
        for k in range(1, N_DEV):
            px, py, pc = peer(k)
            cp = pltpu.make_async_remote_copy(
                src_ref=in_ref.at[4 * px + 2 * py + pc], dst_ref=land.at[me], send_sem=send.at[k - 1],
                recv_sem=recv.at[k - 1], device_id=(px, py, pc), device_id_type=MESH)
            cp.start()
            cps.append(cp)
        land[me] = in_ref[me]
        for cp in cps:
            cp.wait()
        total = land[0]
        for d in range(1, N_DEV):
            total = total + land[d]
        out_ref[me] = total
        cps = []
        for k in range(1, N_DEV):
            cp = pltpu.make_async_remote_copy(
                src_ref=out_ref.at[me], dst_ref=out_ref.at[me], send_sem=send.at[N_DEV - 2 + k],
                recv_sem=recv.at[N_DEV - 2 + k], device_id=peer(k), device_id_type=MESH)
            cp.start()
            cps.append(cp)
        for cp in cps:
            cp.wait()

    vm = pl.BlockSpec(memory_space=pltpu.VMEM)
    return pl.pallas_call(
        body, name=name, in_specs=[vm], out_specs=vm, out_shape=jax.ShapeDtypeStruct(buf.shape, F32),
        scratch_shapes=[pltpu.VMEM(buf.shape, F32), pltpu.SemaphoreType.DMA((2 * N_DEV - 2,)),
                        pltpu.SemaphoreType.DMA((2 * N_DEV - 2,))],
        compiler_params=_params())(buf)


def _add_pairs(name, g, theirs):
    nb, H, C = theirs.shape
    th = _tile(H, max(SUBLANES * 2, (1 << 19) // C), 16)
    nh = H // th

    def body(g_ref, t_ref, p_ref, l_ref):
        s = (g_ref[...].astype(F32) + t_ref[...].astype(F32)).astype(p_ref.dtype)
        p_ref[...] = s

        @pl.when(pl.program_id(1) == _chip())
        def _():
            l_ref[...] = s

    blk = (None, th, C)
    return pl.pallas_call(
        body, name=name, grid=(nh, nb),
        in_specs=[pl.BlockSpec(blk, lambda i, j: (j, lax.axis_index("c") * nh + i, 0)),
                  pl.BlockSpec(blk, lambda i, j: (j, i, 0))],
        out_specs=[pl.BlockSpec(blk, lambda i, j: (j, i, 0)), pl.BlockSpec(blk, lambda i, j: (_chip(), i, 0))],
        out_shape=[jax.ShapeDtypeStruct(theirs.shape, theirs.dtype)] * 2, compiler_params=_params())(g, theirs)


def _sum_chips(name, parts):
    nb, H, C = parts.shape
    th = _tile(H, max(SUBLANES * 2, (1 << 19) // C), 16)
    nh = H // th

    def body(p_ref, o_ref):
        tot = p_ref[0].astype(F32)
        for b in range(1, nb):
            tot = tot + p_ref[b].astype(F32)
        o_ref[...] = tot

    return pl.pallas_call(
        body, name=name, grid=(nh,),
        in_specs=[pl.BlockSpec((nb, th, C), lambda i: (0, i, 0))],
        out_specs=pl.BlockSpec((th, C), lambda i: (lax.axis_index("c") * nh + i, 0)),
        out_shape=jax.ShapeDtypeStruct((2 * H, C), F32), compiler_params=_params())(parts)


def _adamw_math(w, g, m, v):
    m = ADAM_B1 * m + (1.0 - ADAM_B1) * g
    v = ADAM_B2 * v + (1.0 - ADAM_B2) * (g * g)
    m_hat = m / (1.0 - ADAM_B1 ** ADAM_STEP)
    v_hat = v / (1.0 - ADAM_B2 ** ADAM_STEP)
    delta = -ADAM_LR * (m_hat / (jnp.sqrt(v_hat) + ADAM_EPS) + ADAM_WD * w)
    return delta, m, v


def _adamw_big(name, w, m, v, gs):
    depth, R, C = w.shape
    tr = _tile(R, max(SUBLANES, (1 << 18) // C))

    def body(w_ref, m_ref, v_ref, *refs):
        g_refs, (go, do, mo, vo) = refs[:depth], refs[depth:]
        li = pl.program_id(0)
        g = g_refs[0][...]
        for l in range(1, depth):
            g = jnp.where(li == l, g_refs[l][...], g)
        delta, mn, vn = _adamw_math(w_ref[...], g, m_ref[...], v_ref[...])
        go[...] = g
        do[...] = delta
        mo[...] = mn
        vo[...] = vn

    spec = pl.BlockSpec((None, tr, C), lambda li, i: (li, i, 0))
    g_specs = [pl.BlockSpec((tr, C), functools.partial(lambda li, i, l: (jnp.where(li == l, i, 0), 0), l=l))
               for l in range(depth)]
    return pl.pallas_call(body, name=name, grid=(depth, R // tr), in_specs=[spec] * 3 + g_specs,
                          out_specs=[spec] * 4, out_shape=[jax.ShapeDtypeStruct(w.shape, F32)] * 4,
                          compiler_params=_params())(w, m, v, *gs)


def _adamw_small(name, w, g, m, v):
    def fn(i, n, R, B, P, N):
        return list(_adamw_math(*R)), []

    return _rowwise(name, fn, [w, g, m, v], outs=[(LANES, F32)] * 3)


def _pack(arrs, rows_mult):
    flat = jnp.concatenate([a.reshape(-1) for a in arrs])
    n = flat.shape[0]
    per = rows_mult * LANES
    pad = (-n) % per
    flat = jnp.pad(flat, (0, pad))
    return flat.reshape(-1, LANES), n


def _unpack(buf, like):
    flat = buf.reshape(-1)
    out, off = [], 0
    for a in like:
        out.append(flat[off:off + a.size].reshape(a.shape))
        off += a.size
    return out


def _ffn_fwd(tag, alpha, xf, xb, wg, wu, wd, g, b):
    def up(ps, _):
        hg, hu = ps
        return hg, hu, hg * _sigmoid(hg) * hu

    hg, hu, act = _mm_expand(f"{tag}_up", xb, [wg, wu], False, up, [], [F32, F32, BF16])

    def down(acc, ex, bc):
        y, xhat, rstd = _layer_norm(alpha * ex[0] + 0.5 * acc, bc[0], bc[1])
        return y, y, xhat, jnp.broadcast_to(rstd, (rstd.shape[0], LANES))

    D = xf.shape[1]
    yf, yb, xhat, rstd = _mm_contract(f"{tag}_down", [act], [wd], False, down, [xf], [g, b],
                                      [(D, F32), (D, BF16), (D, F32), (LANES, F32)])
    return yf, yb, (xb, hg, hu, act, xhat, rstd)


def _ln_bwd(name, dy, xhat, rstd, g, scale):
    D = dy.shape[1]

    def fn(i, n, R, B, P, N):
        d, xh, rs = R
        dr = _layer_norm_bwd(d, xh, rs[:, :1], B[0])
        return [dr, scale * dr], [jnp.sum(d * xh, axis=0, keepdims=True), jnp.sum(d, axis=0, keepdims=True)]

    return _rowwise(name, fn, [dy, xhat, rstd], bcast=[g], outs=[(D, F32), (D, BF16)], sums=[D, D])


def _ffn_bwd(tag, alpha, dy, saved, wg, wu, wd, g):
    xb, hg, hu, act, xhat, rstd = saved
    dr, dfb, dg, db = _ln_bwd(f"{tag}_ln_bwd", dy, xhat, rstd, g, 0.5)

    def dact(ps, ex):
        da, hgv, huv = ps[0], ex[0], ex[1]
        sg = _sigmoid(hgv)
        return da * huv * (sg * (1.0 + hgv * (1.0 - sg))), da * (hgv * sg)

    dhg, dhu = _mm_expand(f"{tag}_dact", dfb, [wd], True, dact, [hg, hu], [BF16, BF16])
    g_wd = _wgrad_rows(f"{tag}_gwd", act, dfb, tr=1408)
    g_wg = _wgrad_cols(f"{tag}_gwg", xb, dhg)
    g_wu = _wgrad_cols(f"{tag}_gwu", xb, dhu)

    def dxin(acc, ex, bc):
        return [alpha * ex[0] + acc]

    D = dy.shape[1]
    dx, = _mm_contract(f"{tag}_dx", [dhg, dhu], [wg, wu], True, dxin, [dr], [], [(D, F32)])
    return dx, (g_wg, g_wu, g_wd), dg, db


def _conv_taps(v, pv, i, w):
    tm = v.shape[0]
    ext = jnp.concatenate([pv * (i > 0).astype(F32), v], axis=0)
    v1 = pltpu.roll(ext, 1, 0)[SUBLANES:SUBLANES + tm]
    v2 = pltpu.roll(ext, 2, 0)[SUBLANES:SUBLANES + tm]
    return w[0:1] * v2 + w[1:2] * v1 + w[2:3] * v, v1, v2


def _mixer_fwd(tag, alpha, xf, xb, p, w_in, w_glu, w_out):
    Dh = w_in.shape[2]
    proj, = _mm_expand(f"{tag}_proj", xb, [w_in], False, lambda ps, _: ps, [], [F32])
    u, gb, gc, h = [(proj, Dh, j) for j in range(4)]
    nblk = p["wb_re"].shape[0]
    S = nblk * p["wb_re"].shape[2]
    both = lambda ps, ex, bx: ps
    bu_re, bu_im = _bd_mm(f"{tag}_bu", [(proj, p["wb_re"]), (proj, p["wb_im"])], False, both, [], [], [F32, F32])
    s_re, s_im = _scan(f"{tag}_scan", p["a_re"], p["a_im"], bu_re, bu_im)

    def yout(ps, ex, bx):
        ys = ps[0] + ps[1] + bx[0] * ex[0]
        return ys, _gelu(ys), _gelu(ys)

    ys, yg, ygb = _bd_mm(f"{tag}_yout", [(s_re, p["wc_re"]), (s_im, p["wc_imn"])], False, yout, [proj], [p["d"]],
                         [F32, F32, BF16])

    def glu(acc, ex, bc):
        yy = ex[0] * _sigmoid(acc)
        return acc, yy * _rms_inv(yy) * bc[0]

    t, yn = _mm_contract(f"{tag}_glu", [ygb], [w_glu], False, glu, [yg], [p["g_s5"]], [(Dh, F32), (Dh, BF16)])

    def conv(i, n, R, B, P, N):
        gbv, gcv, hv = R
        cw, cb, gcn = B
        cv, _, _ = _conv_taps(gcv * hv, P[0] * P[1], i, cw)
        z = gbv * (cv + cb)
        return [z * _rms_inv(z) * gcn], []

    zn, = _rowwise(f"{tag}_conv", conv, [gb, gc, h], bcast=[p["conv_w"], p["conv_b"], p["g_conv"]],
                   outs=[(Dh, BF16)], prev=[1, 2])
    cat = jnp.concatenate([yn, zn], axis=1)

    def out(acc, ex, bc):
        y, xhat, rstd = _layer_norm(alpha * ex[0] + acc, bc[0], bc[1])
        return y, y, xhat, jnp.broadcast_to(rstd, (rstd.shape[0], LANES))

    D = xf.shape[1]
    yf, yb, xhat, rstd = _mm_contract(f"{tag}_out", [cat], [w_out], False, out, [xf], [p["ln2_g"], p["ln2_b"]],
                                      [(D, F32), (D, BF16), (D, F32), (LANES, F32)])
    return yf, yb, (xb, proj, s_re, s_im, ys, yg, ygb, t, cat, xhat, rstd)


def _mixer_bwd(tag, alpha, dy, saved, p, w_in, w_glu, w_out):
    xb, proj, s_re, s_im, ys, yg, ygb, t, cat, xhat, rstd = saved
    Dh = w_in.shape[2]
    D = dy.shape[1]
    gb, gc, h = [(proj, Dh, j) for j in range(1, 4)]
    dr, dmb, dg2, db2 = _ln_bwd(f"{tag}_ln_bwd", dy, xhat, rstd, p["ln2_g"], 1.0)
    dcat, = _mm_expand(f"{tag}_dcat", dmb, [w_out], True, lambda ps, _: ps, [], [F32])
    g_wout = _wgrad_rows(f"{tag}_gwout", cat, dmb)
    half = dcat.shape[1] // 2

    def conv_b1(i, n, R, B, P, N):
        dzn, gbv, gcv, hv = R
        cw, cb, gcn = B
        v = gcv * hv
        cv, v1, v2 = _conv_taps(v, P[0] * P[1], i, cw)
        cv = cv + cb
        z = gbv * cv
        rinv = _rms_inv(z)
        dz = _rms_bwd(dzn, z, rinv, gcn)
        dcv = dz * gbv
        col = lambda a: jnp.sum(a, axis=0, keepdims=True)
        return [dz * cv, dcv], [col(dzn * z * rinv), col(dcv), col(dcv * v2), col(dcv * v1), col(dcv * v)]

    dgb, dcv, dg_conv, dconv_b, dw0, dw1, dw2 = _rowwise(
        f"{tag}_conv_b1", conv_b1, [(dcat, half, 1), gb, gc, h], bcast=[p["conv_w"], p["conv_b"], p["g_conv"]],
        outs=[(Dh, BF16), (Dh, F32)], sums=[Dh] * 5, prev=[2, 3])

    def conv_b2(i, n, R, B, P, N):
        d, gcv, hv = R
        cw = B[0]
        tm = d.shape[0]
        ext = jnp.concatenate([d, N[0] * (i < n - 1).astype(F32)], axis=0)
        d1 = pltpu.roll(ext, tm + SUBLANES - 1, 0)[:tm]
        d2 = pltpu.roll(ext, tm + SUBLANES - 2, 0)[:tm]
        dv = cw[2:3] * d + cw[1:2] * d1 + cw[0:1] * d2
        return [dv * hv, dv * gcv], []

    dgc, dh = _rowwise(f"{tag}_conv_b2", conv_b2, [dcv, gc, h], bcast=[p["conv_w"]], outs=[(Dh, BF16)] * 2, nxt=[0])

    def glu_b(i, n, R, B, P, N):
        dyn, ygv, tv = R
        sg = _sigmoid(tv)
        yy = ygv * sg
        rinv = _rms_inv(yy)
        dyy = _rms_bwd(dyn, yy, rinv, B[0])
        return [dyy * ygv * sg * (1.0 - sg), dyy * sg], [jnp.sum(dyn * yy * rinv, axis=0, keepdims=True)]

    dtb, dyg0, dg_s5 = _rowwise(f"{tag}_glu_b", glu_b, [(dcat, half, 0), yg, t], bcast=[p["g_s5"]],
                                outs=[(Dh, BF16), (Dh, F32)], sums=[Dh])

    def dys_epi(ps, ex):
        return [(ps[0] + ex[0]) * _gelu_grad(ex[1])]

    dys, = _mm_expand(f"{tag}_dys", dtb, [w_glu], True, dys_epi, [dyg0, ys], [F32])
    g_wglu = _wgrad_rows(f"{tag}_gwglu", ygb, dtb)
    both = lambda ps, ex, bx: ps
    ds_re, ds_im = _bd_mm(f"{tag}_ds", [(dys, p["wc_re"]), (dys, p["wc_imn"])], True, both, [], [], [F32, F32])
    l_re, l_im, da_re, da_im = _scan(f"{tag}_rscan", p["a_re"], -p["a_im"], ds_re, ds_im, reverse=True,
                                     state=(s_re, s_im))

    def du_epi(ps, ex, bx):
        return [ps[0] + ps[1] + bx[0] * ex[0]]

    du, = _bd_mm(f"{tag}_du", [(l_re, p["wb_re"]), (l_im, p["wb_im"])], True, du_epi, [dys], [p["d"]], [BF16])
    nblk = p["wb_re"].shape[0]
    u = (proj, Dh, 0)
    u_arr = proj[:, :Dh]
    g_wb_re = _bd_wgrad(f"{tag}_gwb_re", u_arr, l_re, nblk)
    g_wb_im = _bd_wgrad(f"{tag}_gwb_im", u_arr, l_im, nblk)
    g_wc_re = _bd_wgrad(f"{tag}_gwc_re", s_re, dys, nblk)
    g_wc_imn = _bd_wgrad(f"{tag}_gwc_im", s_im, dys, nblk)

    def dd_fn(i, n, R, B, P, N):
        return [], [jnp.sum(R[0] * R[1], axis=0, keepdims=True)]

    dd, = _rowwise(f"{tag}_dd", dd_fn, [dys, u], sums=[Dh])

    dproj = jnp.concatenate([du, dgb, dgc, dh], axis=1)

    def dxin(acc, ex, bc):
        return [alpha * ex[0] + acc]

    dx, = _mm_contract(f"{tag}_dx", [dproj], [w_in], True, dxin, [dr], [], [(D, F32)])
    g_win = _wgrad_cols(f"{tag}_gwin", xb, dproj)
    small = dict(ln2_g=dg2, ln2_b=db2, g_conv=dg_conv, conv_b=dconv_b,
                 conv_w=jnp.concatenate([dw0, dw1, dw2], axis=0), g_s5=dg_s5, d=dd,
                 da_re=da_re, da_im=da_im, wb_re=g_wb_re, wb_im=g_wb_im, wc_re=g_wc_re, wc_imn=g_wc_imn)
    return dx, (g_win, g_wglu, g_wout), small


def _reduce_pairs(tag, grads):
    theirs = _swap_halves(f"{tag}_swap", grads)
    added = [_add_pairs(f"{tag}_add{t}", g, b) for t, (g, b) in enumerate(zip(grads, theirs))]
    return [a[0] for a in added] + [a[1] for a in added]


def _reduce_finish(tag, lands):
    fulls = [_sum_chips(f"{tag}_sum{t}", g) for t, g in enumerate(lands)]
    return _join_halves(f"{tag}_join", fulls)


BIG = ["ffn1_gate", "ffn1_up", "ffn1_down", "w_in", "s5_w_glu", "w_out", "ffn2_gate", "ffn2_up", "ffn2_down"]
SMALL = ["ln1_g", "ln1_b", "s5_lam_re", "s5_lam_im", "s5_log_dt", "s5_b_re", "s5_b_im", "s5_c_re", "s5_c_im", "s5_d",
         "conv_w", "conv_b", "g_s5", "g_conv", "ln2_g", "ln2_b", "ln3_g", "ln3_b"]
WEIGHTS = ['ffn1_gate', 'ffn1_up', 'ffn1_down', 'ln1_g', 'ln1_b', 'w_in', 's5_lam_re', 's5_lam_im', 's5_log_dt',
           's5_b_re', 's5_b_im', 's5_c_re', 's5_c_im', 's5_d', 's5_w_glu', 'conv_w', 'conv_b', 'g_s5', 'g_conv',
           'w_out', 'ln2_g', 'ln2_b', 'ffn2_gate', 'ffn2_up', 'ffn2_down', 'ln3_g', 'ln3_b']


def _step(W, M, V, x, target):
    depth = W["ffn1_gate"].shape[0]
    alpha = (2.0 * depth) ** 0.25
    L, D = x.shape
    G, N = W["s5_lam_re"].shape[1:]
    P = W["s5_b_re"].shape[3]
    Dh = G * P
    chip = 2 * lax.axis_index("x") + lax.axis_index("y")
    cw_cols = W["conv_w"].shape[2]

    cw_rows = -(-depth * CONV_W // SUBLANES) * SUBLANES
    cw_buf = jnp.zeros((N_CHIPS, 2 * cw_rows, cw_cols), F32)
    cw_buf = lax.dynamic_update_slice(cw_buf, W["conv_w"].reshape(1, depth * CONV_W, cw_cols), (chip, 0, 0))

    T = len(BIG)
    gather, gather0 = _gather_copies(T), _gather_copies(T + 1)
    g_state = _split_start("gather0_start", gather0, T + 1,
                           [_cast_place(f"place0_{n}", W[n], 0) for n in BIG] + [cw_buf], x)

    s5_vjps, lp = [], []
    for l in range(depth):
        (a_re, a_im, bb_re, bb_im), vjp = jax.vjp(_s5_discretize, W["s5_lam_re"][l], W["s5_lam_im"][l],
                                                  W["s5_log_dt"][l], W["s5_b_re"][l], W["s5_b_im"][l])
        s5_vjps.append(vjp)
        row = lambda a: a.reshape(1, -1)
        lp.append(dict(
            a_re=row(a_re), a_im=row(a_im), wb_re=_bd_build(bb_re), wb_im=_bd_build(bb_im),
            wc_re=_bd_build(W["s5_c_re"][l].transpose(0, 2, 1)).transpose(0, 2, 1),
            wc_imn=-_bd_build(W["s5_c_im"][l].transpose(0, 2, 1)).transpose(0, 2, 1),
            d=row(W["s5_d"][l]), g_s5=row(W["g_s5"][l]), g_conv=row(W["g_conv"][l]), conv_b=row(W["conv_b"][l]),
            ln2_g=row(W["ln2_g"][l]), ln2_b=row(W["ln2_b"][l])))

    def cast(i, n, R, B, P_, N_):
        return [R[0]], []

    xb, = _rowwise("cast_x", cast, [x], outs=[(D, BF16)])
    xf = x
    saved, gathered = [], []
    for l in range(depth):
        bufs = _split_wait(f"gather{l}_wait", gather0 if l == 0 else gather, g_state[0], g_state[1], g_state[2], xb)
        token = bufs[0]
        if l + 1 < depth:
            nxt = [_cast_place(f"place{l + 1}_{n}", W[n], l + 1) for n in BIG]
            g_state = _split_start(f"gather{l + 1}_start", gather, T, nxt, bufs[0])
            token = g_state[3]
        passed = _gather_pass_on(f"gather{l}_pass", bufs, token)
        if l == 0:
            cw = passed[T][:, :depth * CONV_W, :].reshape(N_CHIPS, depth, CONV_W, cw_cols)
            cw = cw.transpose(1, 2, 0, 3).reshape(depth, CONV_W, N_CHIPS * cw_cols)
            for q in range(depth):
                lp[q]["conv_w"] = cw[q]
        gathered.append(passed[:T])
        wg1, wu1, wd1, w_in, w_glu, w_out, wg2, wu2, wd2 = gathered[l]
        row = lambda a: a.reshape(1, -1)
        xf, xb, s1 = _ffn_fwd(f"l{l}_ffn1", alpha, xf, xb, wg1, wu1, wd1, row(W["ln1_g"][l]), row(W["ln1_b"][l]))
        xf, xb, s2 = _mixer_fwd(f"l{l}_mix", alpha, xf, xb, lp[l], w_in, w_glu, w_out)
        xf, xb, s3 = _ffn_fwd(f"l{l}_ffn2", alpha, xf, xb, wg2, wu2, wd2, row(W["ln3_g"][l]), row(W["ln3_b"][l]))
        saved.append((s1, s2, s3))

    def loss_fn(i, n, R, B, P_, N_):
        err = R[0] - R[1]
        return [err * (1.0 / D)], [jnp.sum(0.5 * err * err * (1.0 / D), axis=0, keepdims=True)]

    dx, loss_cols = _rowwise("loss", loss_fn, [xf, target], outs=[(D, F32)], sums=[D])
    loss = lax.psum(jnp.sum(loss_cols), ("x", "y", "c"))

    big_g = {n: [None] * depth for n in BIG}
    small_g = {n: [None] * depth for n in SMALL}
    xchg = _xchg_copies(T)
    x_state, tok = None, 0.0

    def finish(layer, lands):
        for n, r in zip(BIG, _reduce_finish(f"l{layer}_red", lands)):
            big_g[n][layer] = r

    for l in reversed(range(depth)):
        wg1, wu1, wd1, w_in, w_glu, w_out, wg2, wu2, wd2 = gathered[l]
        s1, s2, s3 = saved[l]
        row = lambda a: a.reshape(1, -1)
        dx, (g_wg2, g_wu2, g_wd2), dg3, db3 = _ffn_bwd(f"l{l}_ffn2", alpha, dx, s3, wg2, wu2, wd2,
                                                      row(W["ln3_g"][l]) + tok)
        dx, (g_win, g_wglu, g_wout), sm = _mixer_bwd(f"l{l}_mix", alpha, dx, s2, lp[l], w_in, w_glu, w_out)
        dx, (g_wg1, g_wu1, g_wd1), dg1, db1 = _ffn_bwd(f"l{l}_ffn1", alpha, dx, s1, wg1, wu1, wd1, row(W["ln1_g"][l]))
        arrays = _reduce_pairs(f"l{l}_red", [g_wg1, g_wu1, g_wd1, g_win, g_wglu, g_wout, g_wg2, g_wu2, g_wd2])
        after, prev = arrays[0], None
        if x_state is not None:
            prev = _split_wait(f"l{l + 1}_xchg_wait", xchg, x_state[0], x_state[1], x_state[2], arrays[0])
            after = prev[T]
        x_state = _split_start(f"l{l}_xchg_start", xchg, T, arrays, after)
        tok = x_state[3][0, 0]
        if prev is not None:
            finish(l + 1, prev[T:])
        d_bb_re = _bd_extract(sm["wb_re"], G, N, P)
        d_bb_im = _bd_extract(sm["wb_im"], G, N, P)
        d_lre, d_lim, d_ldt, d_bre, d_bim = s5_vjps[l]((sm["da_re"].reshape(G, N), sm["da_im"].reshape(G, N),
                                                        d_bb_re, d_bb_im))
        d_cre = _bd_extract(sm["wc_re"].transpose(0, 2, 1), G, N, P).transpose(0, 2, 1)
        d_cim = -_bd_extract(sm["wc_imn"].transpose(0, 2, 1), G, N, P).transpose(0, 2, 1)
        vals = dict(ln1_g=dg1, ln1_b=db1, s5_lam_re=d_lre, s5_lam_im=d_lim, s5_log_dt=d_ldt, s5_b_re=d_bre,
                    s5_b_im=d_bim, s5_c_re=d_cre, s5_c_im=d_cim, s5_d=sm["d"], conv_w=sm["conv_w"],
                    conv_b=sm["conv_b"], g_s5=sm["g_s5"], g_conv=sm["g_conv"], ln2_g=sm["ln2_g"], ln2_b=sm["ln2_b"],
                    ln3_g=dg3, ln3_b=db3)
        for n in SMALL:
            small_g[n][l] = vals[n].reshape((W[n].shape[1:] if n != "conv_w" else (CONV_W, N_CHIPS * cw_cols)))
    grad_x = dx

    sg = [jnp.stack(small_g[n]) for n in SMALL]
    packed, _ = _pack(sg, SUBLANES * N_DEV)
    packed = _all_reduce_small("small_allreduce", packed.reshape(N_DEV, -1, LANES)).reshape(-1, LANES)
    last = _split_wait("l0_xchg_wait", xchg, x_state[0], x_state[1], x_state[2], packed)
    finish(0, last[T:])
    sg = dict(zip(SMALL, _unpack(packed, sg)))
    sg["conv_w"] = lax.dynamic_slice_in_dim(sg["conv_w"], chip * cw_cols, cw_cols, axis=2)
    gp, _ = _pack([sg[n] for n in SMALL], SUBLANES)
    wp, _ = _pack([W[n] for n in SMALL], SUBLANES)
    mp, _ = _pack([M[n] for n in SMALL], SUBLANES)
    vp, _ = _pack([V[n] for n in SMALL], SUBLANES)
    like = [W[n] for n in SMALL]
    dsm, msm, vsm = [dict(zip(SMALL, _unpack(a, like))) for a in _adamw_small("adamw_small", wp, gp, mp, vp)]

    grads, deltas, new_m, new_v = dict(sg), dsm, msm, vsm
    for n in BIG:
        shp = W[n].shape
        flat = lambda a: a.reshape(shp[0], shp[1], shp[2])
        gr, de, mn, vn = _adamw_big(f"adamw_{n}", flat(W[n]), flat(M[n]), flat(V[n]), big_g[n])
        grads[n], deltas[n], new_m[n], new_v[n] = gr, de, mn, vn

    outs = [loss, grad_x[None]]
    for d in (grads, deltas, new_m, new_v):
        outs += [d[n] for n in WEIGHTS]
    return tuple(outs)


def kernel(x, ffn1_gate, ffn1_up, ffn1_down, ln1_g, ln1_b, w_in, s5_lam_re, s5_lam_im, s5_log_dt, s5_b_re, s5_b_im, s5_c_re, s5_c_im, s5_d, s5_w_glu, conv_w, conv_b, g_s5, g_conv, w_out, ln2_g, ln2_b, ffn2_gate, ffn2_up, ffn2_down, ln3_g, ln3_b, loss_target, m_ffn1_gate, m_ffn1_up, m_ffn1_down, m_ln1_g, m_ln1_b, m_w_in, m_s5_lam_re, m_s5_lam_im, m_s5_log_dt, m_s5_b_re, m_s5_b_im, m_s5_c_re, m_s5_c_im, m_s5_d, m_s5_w_glu, m_conv_w, m_conv_b, m_g_s5, m_g_conv, m_w_out, m_ln2_g, m_ln2_b, m_ffn2_gate, m_ffn2_up, m_ffn2_down, m_ln3_g, m_ln3_b, v_ffn1_gate, v_ffn1_up, v_ffn1_down, v_ln1_g, v_ln1_b, v_w_in, v_s5_lam_re, v_s5_lam_im, v_s5_log_dt, v_s5_b_re, v_s5_b_im, v_s5_c_re, v_s5_c_im, v_s5_d, v_s5_w_glu, v_conv_w, v_conv_b, v_g_s5, v_g_conv, v_w_out, v_ln2_g, v_ln2_b, v_ffn2_gate, v_ffn2_up, v_ffn2_down, v_ln3_g, v_ln3_b):
    a = dict(locals())
    W = {n: a[n] for n in WEIGHTS}
    M = {n: a["m_" + n] for n in WEIGHTS}
    V = {n: a["v_" + n] for n in WEIGHTS}
    return _step(W, M, V, x[0], loss_target[0])
```

```python
import functools

import jax
import jax.numpy as jnp
from jax import lax
from jax.experimental import pallas as pl
from jax.experimental.pallas import tpu as pltpu

F32 = jnp.float32
BF16 = jnp.bfloat16
MESH = pl.DeviceIdType.MESH
HIGH = lax.Precision.HIGHEST

N_CHIPS = 4
N_DEV = 8
V7X_VMEM_LIMIT = 56 * 1024 * 1024
SUBLANES = 8
LANES = 128
S5_P = 16
S5_N = 64
S5_GB = 8
CONV_W = 3
LN_EPS = 1e-5
RMS_EPS = 1e-6
ADAM_LR = 0.001
ADAM_B1 = 0.9
ADAM_B2 = 0.999
ADAM_EPS = 1e-08
ADAM_WD = 0.01
ADAM_STEP = 10
GELU_K = 0.7978845608028654
GELU_C = 0.044715


def _params():
    return pltpu.CompilerParams(vmem_limit_bytes=V7X_VMEM_LIMIT)


def _tile(n, pref, mult=SUBLANES):
    best = None
    for t in range(mult, min(n, pref) + 1, mult):
        if n % t == 0:
            best = t
    return best if best is not None else n


def _mm(a, b, precision=None):
    return jnp.dot(a, b, preferred_element_type=F32, precision=precision)


def _mm_nt(a, b, precision=None):
    return lax.dot_general(a, b, (((1,), (1,)), ((), ())), preferred_element_type=F32, precision=precision)


def _mm_tn(a, b, precision=None):
    return lax.dot_general(a, b, (((0,), (0,)), ((), ())), preferred_element_type=F32, precision=precision)


def _sigmoid(x):
    return 1.0 / (1.0 + jnp.exp(-x))


def _gelu(x):
    return 0.5 * x * (1.0 + jnp.tanh(GELU_K * (x + GELU_C * x * x * x)))


def _gelu_grad(x):
    th = jnp.tanh(GELU_K * (x + GELU_C * x * x * x))
    return 0.5 * (1.0 + th) + 0.5 * x * (1.0 - th * th) * GELU_K * (1.0 + 3.0 * GELU_C * x * x)


def _layer_norm(r, g, b):
    mu = jnp.mean(r, axis=-1, keepdims=True)
    xc = r - mu
    rstd = lax.rsqrt(jnp.mean(xc * xc, axis=-1, keepdims=True) + LN_EPS)
    xhat = xc * rstd
    return xhat * g + b, xhat, rstd


def _layer_norm_bwd(dy, xhat, rstd, g):
    dxh = dy * g
    m1 = jnp.mean(dxh, axis=-1, keepdims=True)
    m2 = jnp.mean(dxh * xhat, axis=-1, keepdims=True)
    return rstd * (dxh - m1 - xhat * m2)


def _rms_inv(x):
    return lax.rsqrt(jnp.mean(x * x, axis=-1, keepdims=True) + RMS_EPS)


def _rms_bwd(dy, x, rinv, g):
    dxh = dy * g
    return rinv * dxh - x * (rinv * rinv * rinv) * jnp.mean(dxh * x, axis=-1, keepdims=True)


def _rowwise(name, fn, rows, bcast=(), outs=(), sums=(), prev=(), nxt=(), tm=256):
    rows = [r if isinstance(r, tuple) else (r, r.shape[1], 0) for r in rows]
    L = rows[0][0].shape[0]
    tm = _tile(L, tm)
    n = L // tm
    hb = tm // SUBLANES
    nh = L // SUBLANES
    nr, nb, npv, nnx, no, ns = len(rows), len(bcast), len(prev), len(nxt), len(outs), len(sums)

    def body(*refs):
        i = pl.program_id(0)
        k = 0
        R = [r[...] for r in refs[k:k + nr]]; k += nr
        B = [r[...] for r in refs[k:k + nb]]; k += nb
        P = [r[...] for r in refs[k:k + npv]]; k += npv
        N = [r[...] for r in refs[k:k + nnx]]; k += nnx
        o_refs = refs[k:k + no]; k += no
        s_refs = refs[k:k + ns]
        O, S = fn(i, n, R, B, P, N)
        for ref, val in zip(o_refs, O):
            ref[...] = val.astype(ref.dtype)
        if ns:
            @pl.when(i == 0)
            def _():
                for ref in s_refs:
                    ref[...] = jnp.zeros_like(ref)
            for ref, val in zip(s_refs, S):
                ref[...] += val

    in_specs = [pl.BlockSpec((tm, w), functools.partial(lambda i, cb: (i, cb), cb=cb)) for _, w, cb in rows]
    in_specs += [pl.BlockSpec(b.shape, lambda i: (0, 0)) for b in bcast]
    in_specs += [pl.BlockSpec((SUBLANES, rows[j][1]),
                              functools.partial(lambda i, cb: (jnp.maximum(i * hb - 1, 0), cb), cb=rows[j][2]))
                 for j in prev]
    in_specs += [pl.BlockSpec((SUBLANES, rows[j][1]),
                              functools.partial(lambda i, cb: (jnp.minimum((i + 1) * hb, nh - 1), cb), cb=rows[j][2]))
                 for j in nxt]
    out_specs = [pl.BlockSpec((tm, w), lambda i: (i, 0)) for w, _ in outs]
    out_specs += [pl.BlockSpec((1, w), lambda i: (0, 0)) for w in sums]
    out_shape = [jax.ShapeDtypeStruct((L, w), dt) for w, dt in outs]
    out_shape += [jax.ShapeDtypeStruct((1, w), F32) for w in sums]
    args = [r[0] for r in rows] + list(bcast) + [rows[j][0] for j in prev] + [rows[j][0] for j in nxt]
    return pl.pallas_call(body, name=name, grid=(n,), in_specs=in_specs, out_specs=out_specs,
                          out_shape=out_shape, compiler_params=_params())(*args)


def _mm_expand(name, a, ws, nt, epi, extras, outs, tm=512):
    L, K = a.shape
    tm = _tile(L, tm)
    nw, ne = len(ws), len(extras)
    co = ws[0].shape[1] if nt else ws[0].shape[2]

    def body(a_ref, *refs):
        av = a_ref[...]
        ps = [(_mm_nt if nt else _mm)(av, w[...]) for w in refs[:nw]]
        vals = epi(ps, [e[...] for e in refs[nw:nw + ne]])
        for ref, val in zip(refs[nw + ne:], vals):
            ref[...] = val.astype(ref.dtype)

    in_specs = [pl.BlockSpec((tm, K), lambda j, i: (i, 0))]
    in_specs += [pl.BlockSpec((None,) + w.shape[1:], lambda j, i: (j, 0, 0)) for w in ws]
    in_specs += [pl.BlockSpec((tm, co), lambda j, i: (i, j)) for _ in extras]
    out_specs = [pl.BlockSpec((tm, co), lambda j, i: (i, j)) for _ in outs]
    out_shape = [jax.ShapeDtypeStruct((L, N_CHIPS * co), dt) for dt in outs]
    return pl.pallas_call(body, name=name, grid=(N_CHIPS, L // tm), in_specs=in_specs, out_specs=out_specs,
                          out_shape=out_shape, compiler_params=_params())(a, *ws, *extras)


def _mm_contract(name, as_, ws, nt, epi, extras, bcast, outs, tm=512):
    L = as_[0].shape[0]
    tm = _tile(L, tm)
    na, ne, nb = len(as_), len(extras), len(bcast)
    cb = as_[0].shape[1] // N_CHIPS
    n_out = ws[0].shape[1] if nt else ws[0].shape[2]

    def body(*refs):
        k = pl.program_id(1)
        a_refs, w_refs = refs[:na], refs[na:2 * na]
        e_refs = refs[2 * na:2 * na + ne]
        b_refs = refs[2 * na + ne:2 * na + ne + nb]
        o_refs = refs[2 * na + ne + nb:-1]
        acc = refs[-1]

        @pl.when(k == 0)
        def _():
            acc[...] = jnp.zeros_like(acc)

        part = None
        for a_ref, w_ref in zip(a_refs, w_refs):
            p = (_mm_nt if nt else _mm)(a_ref[...], w_ref[...])
            part = p if part is None else part + p
        acc[...] += part

        @pl.when(k == N_CHIPS - 1)
        def _():
            vals = epi(acc[...], [e[...] for e in e_refs], [b[...] for b in b_refs])
            for ref, val in zip(o_refs, vals):
                ref[...] = val.astype(ref.dtype)

    in_specs = [pl.BlockSpec((tm, cb), lambda i, k: (i, k)) for _ in as_]
    in_specs += [pl.BlockSpec((None,) + w.shape[1:], lambda i, k: (k, 0, 0)) for w in ws]
    in_specs += [pl.BlockSpec((tm, e.shape[1]), lambda i, k: (i, 0)) for e in extras]
    in_specs += [pl.BlockSpec(b.shape, lambda i, k: (0, 0)) for b in bcast]
    out_specs = [pl.BlockSpec((tm, w), lambda i, k: (i, 0)) for w, _ in outs]
    out_shape = [jax.ShapeDtypeStruct((L, w), dt) for w, dt in outs]
    return pl.pallas_call(body, name=name, grid=(L // tm, N_CHIPS), in_specs=in_specs, out_specs=out_specs,
                          out_shape=out_shape, scratch_shapes=[pltpu.VMEM((tm, n_out), F32)],
                          compiler_params=_params())(*as_, *ws, *extras, *bcast)


def _wgrad_cols(name, a, b, tk=1024):
    L, K = a.shape
    C = b.shape[1] // N_CHIPS
    tk = _tile(K, tk, LANES)

    def body(a_ref, b_ref, o_ref):
        o_ref[...] = _mm_tn(a_ref[...], b_ref[...]).astype(o_ref.dtype)

    return pl.pallas_call(
        body, name=name, grid=(N_CHIPS, K // tk),
        in_specs=[pl.BlockSpec((L, tk), lambda j, kb: (0, kb)), pl.BlockSpec((L, C), lambda j, kb: (0, j))],
        out_specs=pl.BlockSpec((None, tk, C), lambda j, kb: (j, kb, 0)),
        out_shape=jax.ShapeDtypeStruct((N_CHIPS, K, C), BF16), compiler_params=_params())(a, b)


def _wgrad_rows(name, a, b, tr=512):
    L, N = b.shape
    R = a.shape[1] // N_CHIPS
    tr = _tile(R, tr, LANES)
    nrb = R // tr

    def body(a_ref, b_ref, o_ref):
        o_ref[...] = _mm_tn(a_ref[...], b_ref[...]).astype(o_ref.dtype)

    return pl.pallas_call(
        body, name=name, grid=(N_CHIPS, nrb),
        in_specs=[pl.BlockSpec((L, tr), lambda j, rb: (0, j * nrb + rb)), pl.BlockSpec((L, N), lambda j, rb: (0, 0))],
        out_specs=pl.BlockSpec((None, tr, N), lambda j, rb: (j, rb, 0)),
        out_shape=jax.ShapeDtypeStruct((N_CHIPS, R, N), BF16), compiler_params=_params())(a, b)


def _bd_mm(name, pairs, nt, epi, extras, bex, outs, tm=1024):
    L = pairs[0][0].shape[0]
    nblk = pairs[0][1].shape[0]
    tm = _tile(L, tm)
    npair, ne, nx = len(pairs), len(extras), len(bex)
    w0 = pairs[0][1]
    ca, co = (w0.shape[2], w0.shape[1]) if nt else (w0.shape[1], w0.shape[2])

    def body(*refs):
        ps = [(_mm_nt if nt else _mm)(refs[2 * q][...], refs[2 * q + 1][...], HIGH) for q in range(npair)]
        k = 2 * npair
        vals = epi(ps, [e[...] for e in refs[k:k + ne]], [e[...] for e in refs[k + ne:k + ne + nx]])
        for ref, val in zip(refs[k + ne + nx:], vals):
            ref[...] = val.astype(ref.dtype)

    in_specs = []
    args = []
    for a, w in pairs:
        in_specs += [pl.BlockSpec((tm, ca), lambda i, g: (i, g)), pl.BlockSpec((None,) + w.shape[1:], lambda i, g: (g, 0, 0))]
        args += [a, w]
    in_specs += [pl.BlockSpec((tm, co), lambda i, g: (i, g)) for _ in extras]
    in_specs += [pl.BlockSpec((1, co), lambda i, g: (0, g)) for _ in bex]
    out_specs = [pl.BlockSpec((tm, co), lambda i, g: (i, g)) for _ in outs]
    out_shape = [jax.ShapeDtypeStruct((L, nblk * co), dt) for dt in outs]
    return pl.pallas_call(body, name=name, grid=(L // tm, nblk), in_specs=in_specs, out_specs=out_specs,
                          out_shape=out_shape, compiler_params=_params())(*args, *extras, *bex)


def _bd_wgrad(name, a, b, nblk):
    L = a.shape[0]
    ca, cb = a.shape[1] // nblk, b.shape[1] // nblk

    def body(a_ref, b_ref, o_ref):
        o_ref[...] = _mm_tn(a_ref[...], b_ref[...], HIGH)

    return pl.pallas_call(
        body, name=name, grid=(nblk,),
        in_specs=[pl.BlockSpec((L, ca), lambda g: (0, g)), pl.BlockSpec((L, cb), lambda g: (0, g))],
        out_specs=pl.BlockSpec((None, ca, cb), lambda g: (g, 0, 0)),
        out_shape=jax.ShapeDtypeStruct((nblk, ca, cb), F32), compiler_params=_params())(a, b)


def _cmul(ar, ai, br, bi):
    return ar * br - ai * bi, ar * bi + ai * br


def _scan(name, ar, ai, xr, xi, reverse=False, state=None, tc=512):
    L, S = xr.shape
    tc = _tile(S, tc, LANES)
    nt = L // SUBLANES
    with_da = state is not None

    def body(*refs):
        ar_ref, ai_ref, xr_ref, xi_ref = refs[:4]
        if with_da:
            sr_ref, si_ref, yr_ref, yi_ref, dar_ref, dai_ref = refs[4:]
        else:
            yr_ref, yi_ref = refs[4:]
        a1 = (ar_ref[...], ai_ref[...])
        pw = [a1]
        for _ in range(SUBLANES - 1):
            pw.append(_cmul(*pw[-1], *a1))
        row = lax.broadcasted_iota(jnp.int32, (SUBLANES, tc), 0)
        tr = jnp.zeros((SUBLANES, tc), F32)
        ti = jnp.zeros((SUBLANES, tc), F32)
        for t in range(SUBLANES):
            p = pw[SUBLANES - 1 - t] if reverse else pw[t]
            tr = jnp.where(row == t, p[0], tr)
            ti = jnp.where(row == t, p[1], ti)

        def step(n, carry):
            idx = (nt - 1 - n) if reverse else n
            rows = pl.ds(pl.multiple_of(idx * SUBLANES, SUBLANES), SUBLANES)
            vr, vi = xr_ref[rows, :], xi_ref[rows, :]
            for d in (1, 2, 4):
                pr, pi = pw[d - 1]
                if reverse:
                    qr, qi = pltpu.roll(vr, SUBLANES - d, 0), pltpu.roll(vi, SUBLANES - d, 0)
                    keep = row < SUBLANES - d
                else:
                    qr, qi = pltpu.roll(vr, d, 0), pltpu.roll(vi, d, 0)
                    keep = row >= d
                mr, mi = _cmul(pr, pi, qr, qi)
                vr = vr + jnp.where(keep, mr, 0.0)
                vi = vi + jnp.where(keep, mi, 0.0)
            cr, ci = carry[0], carry[1]
            mr, mi = _cmul(tr, ti, cr, ci)
            vr, vi = vr + mr, vi + mi
            yr_ref[rows, :] = vr
            yi_ref[rows, :] = vi
            edge = 0 if reverse else SUBLANES - 1
            new = (vr[edge:edge + 1, :], vi[edge:edge + 1, :])
            if not with_da:
                return new
            pidx = jnp.maximum(idx - 1, 0)
            prows = pl.ds(pl.multiple_of(pidx * SUBLANES, SUBLANES), SUBLANES)
            live = (idx > 0).astype(F32)
            s0r = sr_ref[prows, :][SUBLANES - 1:SUBLANES, :] * live
            s0i = si_ref[prows, :][SUBLANES - 1:SUBLANES, :] * live
            spr = jnp.where(row == 0, s0r, pltpu.roll(sr_ref[rows, :], 1, 0))
            spi = jnp.where(row == 0, s0i, pltpu.roll(si_ref[rows, :], 1, 0))
            return new + (carry[2] + vr * spr + vi * spi, carry[3] + vi * spr - vr * spi)

        zero = jnp.zeros((1, tc), F32)
        init = (zero, zero)
        if with_da:
            acc0 = jnp.zeros((SUBLANES, tc), F32)
            init = init + (acc0, acc0)
        fin = lax.fori_loop(0, nt, step, init, unroll=2)
        if with_da:
            dar_ref[...] = jnp.sum(fin[2], axis=0, keepdims=True)
            dai_ref[...] = jnp.sum(fin[3], axis=0, keepdims=True)

    col = pl.BlockSpec((L, tc), lambda j: (0, j))
    vec = pl.BlockSpec((1, tc), lambda j: (0, j))
    n_in = 6 if with_da else 4
    in_specs = [vec, vec] + [col] * (n_in - 2)
    out_specs = [col, col] + ([vec, vec] if with_da else [])
    out_shape = [jax.ShapeDtypeStruct((L, S), F32)] * 2 + ([jax.ShapeDtypeStruct((1, S), F32)] * 2 if with_da else [])
    args = (ar, ai, xr, xi) + (tuple(state) if with_da else ())
    return pl.pallas_call(body, name=name, grid=(S // tc,), in_specs=in_specs, out_specs=out_specs,
                          out_shape=out_shape, compiler_params=_params())(*args)


def _bd_build(w_gnp):
    G, N, P = w_gnp.shape
    nb = G // S5_GB
    eye = jnp.eye(S5_GB, dtype=F32)
    x = w_gnp.reshape(nb, S5_GB, N, P).transpose(0, 1, 3, 2)
    w = x[:, :, :, None, :] * eye[None, :, None, :, None]
    return w.reshape(nb, S5_GB * P, S5_GB * N)


def _bd_extract(w, G, N, P):
    nb = G // S5_GB
    eye = jnp.eye(S5_GB, dtype=F32)
    w5 = w.reshape(nb, S5_GB, P, S5_GB, N)
    d = jnp.sum(w5 * eye[None, :, None, :, None], axis=3)
    return d.transpose(0, 1, 3, 2).reshape(G, N, P)


def _s5_discretize(lam_re, lam_im, log_dt, b_re, b_im):
    dt = jnp.exp(log_dt)[:, None]
    mag = jnp.exp(lam_re * dt)
    ang = lam_im * dt
    ab_re = mag * jnp.cos(ang)
    ab_im = mag * jnp.sin(ang)
    den = lam_re * lam_re + lam_im * lam_im
    nr = ab_re - 1.0
    ni = ab_im
    q_re = (nr * lam_re + ni * lam_im) / den
    q_im = (ni * lam_re - nr * lam_im) / den
    bb_re = q_re[..., None] * b_re - q_im[..., None] * b_im
    bb_im = q_re[..., None] * b_im + q_im[..., None] * b_re
    return ab_re, ab_im, bb_re, bb_im


HBM_SPEC = pl.BlockSpec(memory_space=pl.ANY)


def _place():
    x, y, c = lax.axis_index("x"), lax.axis_index("y"), lax.axis_index("c")
    others = [(1 - x, y), (x, 1 - y), (1 - x, 1 - y)]
    return x, y, c, 2 * x + y, others


def _half(ref, h, axis):
    n = ref.shape[axis] // 2
    idx = [slice(None)] * len(ref.shape)
    idx[axis] = pl.ds(h * n, n)
    return ref.at[tuple(idx)]


def _chip():
    return 2 * lax.axis_index("x") + lax.axis_index("y")


def _cast_place(name, w, layer):
    _, R, C = w.shape
    tr = _tile(R, max(16, (1 << 19) // C), 16)

    def body(w_ref, o_ref):
        o_ref[...] = w_ref[...].astype(o_ref.dtype)

    return pl.pallas_call(
        body, name=name, grid=(R // tr,),
        in_specs=[pl.BlockSpec((None, tr, C), lambda i: (layer, i, 0))],
        out_specs=pl.BlockSpec((None, tr, C), lambda i: (_chip(), i, 0)),
        out_shape=jax.ShapeDtypeStruct((N_CHIPS, R, C), BF16), compiler_params=_params())(w)


HBM_ONLY = pl.BlockSpec(memory_space=pltpu.HBM)
SEM_SPEC = pl.BlockSpec(memory_space=pltpu.SEMAPHORE)
EFFECT = pltpu.SideEffectType.DATAFLOW_SIDE_EFFECTING


def _split_start(name, copies_fn, n_copies, arrays, after):
    n, na = len(arrays), len(after)

    def body(*refs):
        ins, send, recv, token = refs[:n], refs[n + na], refs[n + na + 1], refs[-1]
        for cp in copies_fn(ins, send, recv):
            cp.start()
        token[...] = jnp.zeros_like(token)

    n_sem = (n_copies,)
    outs = pl.pallas_call(
        body, name=name,
        out_shape=(pltpu.SemaphoreType.DMA(n_sem), pltpu.SemaphoreType.DMA(n_sem),
                   *[pltpu.HBM(a.shape, a.dtype) for a in arrays], jax.ShapeDtypeStruct((SUBLANES, LANES), F32)),
        in_specs=[HBM_ONLY] * n + [HBM_SPEC] * na,
        out_specs=(SEM_SPEC, SEM_SPEC, *[HBM_ONLY] * n, pl.BlockSpec(memory_space=pltpu.VMEM)),
        input_output_aliases={i: 2 + i for i in range(n)},
        compiler_params=pltpu.CompilerParams(has_side_effects=EFFECT),
    )(*[pltpu.with_memory_space_constraint(a, pltpu.HBM) for a in arrays], *after)
    return outs[0], outs[1], list(outs[2:2 + n]), outs[-1]


def _split_wait(name, copies_fn, send, recv, arrays, after):
    n, na = len(arrays), len(after)

    def body(*refs):
        ins, send_ref, recv_ref = refs[:n], refs[n], refs[n + 1]
        for cp in copies_fn(ins, send_ref, recv_ref):
            cp.wait_send()
            cp.wait_recv()

    outs = pl.pallas_call(
        body, name=name, out_shape=tuple(pltpu.HBM(a.shape, a.dtype) for a in arrays),
        in_specs=[HBM_ONLY] * n + [SEM_SPEC, SEM_SPEC] + [HBM_SPEC] * na, out_specs=(HBM_ONLY,) * n,
        input_output_aliases={i: i for i in range(n)},
        compiler_params=pltpu.CompilerParams(has_side_effects=EFFECT),
    )(*arrays, send, recv, *after)
    return list(outs)


def _gather_copies(T):
    def copies(bufs, send, recv):
        x, y, c, me, others = _place()
        cps = []
        for t in range(T):
            mine = _half(bufs[t].at[me], c, 0)
            for k, (ox, oy) in enumerate(others):
                cps.append(pltpu.make_async_remote_copy(
                    src_ref=mine, dst_ref=mine, send_sem=send.at[3 * t + k], recv_sem=recv.at[3 * t + k],
                    device_id=(ox, oy, c), device_id_type=MESH))
        return cps
    return copies


def _xchg_copies(T):
    def copies(arrays, send, recv):
        x, y, c, me, others = _place()
        cps = []
        for t in range(T):
            for k, (ox, oy) in enumerate(others):
                cps.append(pltpu.make_async_remote_copy(
                    src_ref=arrays[t].at[2 * ox + oy], dst_ref=arrays[T + t].at[me], send_sem=send.at[3 * t + k],
                    recv_sem=recv.at[3 * t + k], device_id=(ox, oy, c), device_id_type=MESH))
        return cps
    return copies


def _gather_pass_on(name, bufs, after):
    T = len(bufs)

    def body(*refs):
        outs = refs[T + 1:2 * T + 1]
        send, recv = refs[2 * T + 1:]
        x, y, c, me, others = _place()
        cps = []
        for t in range(T):
            for k, (ox, oy) in enumerate(others):
                got = _half(outs[t].at[2 * ox + oy], c, 0)
                cp = pltpu.make_async_remote_copy(
                    src_ref=got, dst_ref=got, send_sem=send.at[t, k], recv_sem=recv.at[t, k],
                    device_id=(x, y, 1 - c), device_id_type=MESH)
                cp.start()
                cps.append(cp)
        for cp in cps:
            cp.wait()

    return pl.pallas_call(
        body, name=name, in_specs=[HBM_SPEC] * (T + 1), out_specs=[HBM_SPEC] * T,
        out_shape=[jax.ShapeDtypeStruct(b.shape, b.dtype) for b in bufs],
        input_output_aliases={t: t for t in range(T)},
        scratch_shapes=[pltpu.SemaphoreType.DMA((T, 3)), pltpu.SemaphoreType.DMA((T, 3))],
    )(*bufs, after)


def _swap_copies(T):
    def copies(arrays, send, recv):
        x, y, c, me, others = _place()
        return [pltpu.make_async_remote_copy(
            src_ref=_half(arrays[t], 1 - c, 1), dst_ref=arrays[T + t], send_sem=send.at[t], recv_sem=recv.at[t],
            device_id=(x, y, 1 - c), device_id_type=MESH) for t in range(T)]
    return copies


def _join_copies(T):
    def copies(arrays, send, recv):
        x, y, c, me, others = _place()
        cps = []
        for t in range(T):
            mine = _half(arrays[t], c, 0)
            cps.append(pltpu.make_async_remote_copy(
                src_ref=mine, dst_ref=mine, send_sem=send.at[t], recv_sem=recv.at[t],
                device_id=(x, y, 1 - c), device_id_type=MESH))
        return cps
    return copies


def _all_reduce_small(name, buf):
    _, r, w = buf.shape

    def body(in_ref, out_ref, land, send, recv):
        x, y, c, _, _ = _place()
        me = 4 * x + 2 * y + c

        def peer(k):
            return (1 - x if k & 4 else x, 1 - y if k & 2 else y, 1 - c if k & 1 else c)

        cps = []
        for k in range(1, N_DEV):
            px, py, pc = peer(k)
            cp = pltpu.make_async_remote_copy(
                src_ref=in_ref.at[4 * px + 2 * py + pc], dst_ref=land.at[me], send_sem=send.at[k - 1],
                recv_sem=recv.at[k - 1], device_id=(px, py, pc), device_id_type=MESH)
            cp.start()
            cps.append(cp)
        land[me] = in_ref[me]
        for cp in cps:
            cp.wait()
        total = land[0]
        for d in range(1, N_DEV):
            total = total + land[d]
        out_ref[me] = total
        cps = []
        for k in range(1, N_DEV):
            cp = pltpu.make_async_remote_copy(
                src_ref=out_ref.at[me], dst_ref=out_ref.at[me], send_sem=send.at[N_DEV - 2 + k],
                recv_sem=recv.at[N_DEV - 2 + k], device_id=peer(k), device_id_type=MESH)
            cp.start()
            cps.append(cp)
        for cp in cps:
            cp.wait()

    vm = pl.BlockSpec(memory_space=pltpu.VMEM)
    return pl.pallas_call(
        body, name=name, in_specs=[vm], out_specs=vm, out_shape=jax.ShapeDtypeStruct(buf.shape, F32),
        scratch_shapes=[pltpu.VMEM(buf.shape, F32), pltpu.SemaphoreType.DMA((2 * N_DEV - 2,)),
                        pltpu.SemaphoreType.DMA((2 * N_DEV - 2,))],
        compiler_params=_params())(buf)


def _add_pairs(name, g, theirs):
    nb, H, C = theirs.shape
    th = _tile(H, max(SUBLANES * 2, (1 << 19) // C), 16)
    nh = H // th

    def body(g_ref, t_ref, p_ref, l_ref):
        s = (g_ref[...].astype(F32) + t_ref[...].astype(F32)).astype(p_ref.dtype)
        p_ref[...] = s

        @pl.when(pl.program_id(1) == _chip())
        def _():
            l_ref[...] = s

    blk = (None, th, C)
    return pl.pallas_call(
        body, name=name, grid=(nh, nb),
        in_specs=[pl.BlockSpec(blk, lambda i, j: (j, lax.axis_index("c") * nh + i, 0)),
                  pl.BlockSpec(blk, lambda i, j: (j, i, 0))],
        out_specs=[pl.BlockSpec(blk, lambda i, j: (j, i, 0)), pl.BlockSpec(blk, lambda i, j: (_chip(), i, 0))],
        out_shape=[jax.ShapeDtypeStruct(theirs.shape, theirs.dtype)] * 2, compiler_params=_params())(g, theirs)


def _sum_chips(name, parts):
    nb, H, C = parts.shape
    th = _tile(H, max(SUBLANES * 2, (1 << 19) // C), 16)
    nh = H // th

    def body(p_ref, o_ref):
        tot = p_ref[0].astype(F32)
        for b in range(1, nb):
            tot = tot + p_ref[b].astype(F32)
        o_ref[...] = tot

    return pl.pallas_call(
        body, name=name, grid=(nh,),
        in_specs=[pl.BlockSpec((nb, th, C), lambda i: (0, i, 0))],
        out_specs=pl.BlockSpec((th, C), lambda i: (lax.axis_index("c") * nh + i, 0)),
        out_shape=jax.ShapeDtypeStruct((2 * H, C), F32), compiler_params=_params())(parts)


def _adamw_math(w, g, m, v):
    m = ADAM_B1 * m + (1.0 - ADAM_B1) * g
    v = ADAM_B2 * v + (1.0 - ADAM_B2) * (g * g)
    m_hat = m / (1.0 - ADAM_B1 ** ADAM_STEP)
    v_hat = v / (1.0 - ADAM_B2 ** ADAM_STEP)
    delta = -ADAM_LR * (m_hat / (jnp.sqrt(v_hat) + ADAM_EPS) + ADAM_WD * w)
    return delta, m, v


def _adamw_big(name, w, m, v, gs):
    depth, R, C = w.shape
    tr = _tile(R, max(SUBLANES, (1 << 18) // C))

    def body(w_ref, m_ref, v_ref, *refs):
        g_refs, (go, do, mo, vo) = refs[:depth], refs[depth:]
        li = pl.program_id(0)
        g = g_refs[0][...]
        for l in range(1, depth):
            g = jnp.where(li == l, g_refs[l][...], g)
        delta, mn, vn = _adamw_math(w_ref[...], g, m_ref[...], v_ref[...])
        go[...] = g
        do[...] = delta
        mo[...] = mn
        vo[...] = vn

    spec = pl.BlockSpec((None, tr, C), lambda li, i: (li, i, 0))
    g_specs = [pl.BlockSpec((tr, C), functools.partial(lambda li, i, l: (jnp.where(li == l, i, 0), 0), l=l))
               for l in range(depth)]
    return pl.pallas_call(body, name=name, grid=(depth, R // tr), in_specs=[spec] * 3 + g_specs,
                          out_specs=[spec] * 4, out_shape=[jax.ShapeDtypeStruct(w.shape, F32)] * 4,
                          compiler_params=_params())(w, m, v, *gs)


def _adamw_small(name, w, g, m, v):
    def fn(i, n, R, B, P, N):
        return list(_adamw_math(*R)), []

    return _rowwise(name, fn, [w, g, m, v], outs=[(LANES, F32)] * 3)


def _pack(arrs, rows_mult):
    flat = jnp.concatenate([a.reshape(-1) for a in arrs])
    n = flat.shape[0]
    per = rows_mult * LANES
    pad = (-n) % per
    flat = jnp.pad(flat, (0, pad))
    return flat.reshape(-1, LANES), n


def _unpack(buf, like):
    flat = buf.reshape(-1)
    out, off = [], 0
    for a in like:
        out.append(flat[off:off + a.size].reshape(a.shape))
        off += a.size
    return out


def _ffn_fwd(tag, alpha, xf, xb, wg, wu, wd, g, b):
    def up(ps, _):
        hg, hu = ps
        return hg, hu, hg * _sigmoid(hg) * hu

    hg, hu, act = _mm_expand(f"{tag}_up", xb, [wg, wu], False, up, [], [F32, F32, BF16])

    def down(acc, ex, bc):
        y, xhat, rstd = _layer_norm(alpha * ex[0] + 0.5 * acc, bc[0], bc[1])
        return y, y, xhat, jnp.broadcast_to(rstd, (rstd.shape[0], LANES))

    D = xf.shape[1]
    yf, yb, xhat, rstd = _mm_contract(f"{tag}_down", [act], [wd], False, down, [xf], [g, b],
                                      [(D, F32), (D, BF16), (D, F32), (LANES, F32)])
    return yf, yb, (xb, hg, hu, act, xhat, rstd)


def _ln_bwd(name, dy, xhat, rstd, g, scale):
    D = dy.shape[1]

    def fn(i, n, R, B, P, N):
        d, xh, rs = R
        dr = _layer_norm_bwd(d, xh, rs[:, :1], B[0])
        return [dr, scale * dr], [jnp.sum(d * xh, axis=0, keepdims=True), jnp.sum(d, axis=0, keepdims=True)]

    return _rowwise(name, fn, [dy, xhat, rstd], bcast=[g], outs=[(D, F32), (D, BF16)], sums=[D, D])


def _ffn_bwd(tag, alpha, dy, saved, wg, wu, wd, g):
    xb, hg, hu, act, xhat, rstd = saved
    dr, dfb, dg, db = _ln_bwd(f"{tag}_ln_bwd", dy, xhat, rstd, g, 0.5)

    def dact(ps, ex):
        da, hgv, huv = ps[0], ex[0], ex[1]
        sg = _sigmoid(hgv)
        return da * huv * (sg * (1.0 + hgv * (1.0 - sg))), da * (hgv * sg)

    dhg, dhu = _mm_expand(f"{tag}_dact", dfb, [wd], True, dact, [hg, hu], [BF16, BF16])
    g_wd = _wgrad_rows(f"{tag}_gwd", act, dfb, tr=1408)
    g_wg = _wgrad_cols(f"{tag}_gwg", xb, dhg)
    g_wu = _wgrad_cols(f"{tag}_gwu", xb, dhu)

    def dxin(acc, ex, bc):
        return [alpha * ex[0] + acc]

    D = dy.shape[1]
    dx, = _mm_contract(f"{tag}_dx", [dhg, dhu], [wg, wu], True, dxin, [dr], [], [(D, F32)])
    return dx, (g_wg, g_wu, g_wd), dg, db


def _conv_taps(v, pv, i, w):
    tm = v.shape[0]
    ext = jnp.concatenate([pv * (i > 0).astype(F32), v], axis=0)
    v1 = pltpu.roll(ext, 1, 0)[SUBLANES:SUBLANES + tm]
    v2 = pltpu.roll(ext, 2, 0)[SUBLANES:SUBLANES + tm]
    return w[0:1] * v2 + w[1:2] * v1 + w[2:3] * v, v1, v2


def _mixer_fwd(tag, alpha, xf, xb, p, w_in, w_glu, w_out):
    Dh = w_in.shape[2]
    proj, = _mm_expand(f"{tag}_proj", xb, [w_in], False, lambda ps, _: ps, [], [F32])
    u, gb, gc, h = [(proj, Dh, j) for j in range(4)]
    nblk = p["wb_re"].shape[0]
    S = nblk * p["wb_re"].shape[2]
    both = lambda ps, ex, bx: ps
    bu_re, bu_im = _bd_mm(f"{tag}_bu", [(proj, p["wb_re"]), (proj, p["wb_im"])], False, both, [], [], [F32, F32])
    s_re, s_im = _scan(f"{tag}_scan", p["a_re"], p["a_im"], bu_re, bu_im)

    def yout(ps, ex, bx):
        ys = ps[0] + ps[1] + bx[0] * ex[0]
        return ys, _gelu(ys), _gelu(ys)

    ys, yg, ygb = _bd_mm(f"{tag}_yout", [(s_re, p["wc_re"]), (s_im, p["wc_imn"])], False, yout, [proj], [p["d"]],
                         [F32, F32, BF16])

    def glu(acc, ex, bc):
        yy = ex[0] * _sigmoid(acc)
        return acc, yy * _rms_inv(yy) * bc[0]

    t, yn = _mm_contract(f"{tag}_glu", [ygb], [w_glu], False, glu, [yg], [p["g_s5"]], [(Dh, F32), (Dh, BF16)])

    def conv(i, n, R, B, P, N):
        gbv, gcv, hv = R
        cw, cb, gcn = B
        cv, _, _ = _conv_taps(gcv * hv, P[0] * P[1], i, cw)
        z = gbv * (cv + cb)
        return [z * _rms_inv(z) * gcn], []

    zn, = _rowwise(f"{tag}_conv", conv, [gb, gc, h], bcast=[p["conv_w"], p["conv_b"], p["g_conv"]],
                   outs=[(Dh, BF16)], prev=[1, 2])
    cat = jnp.concatenate([yn, zn], axis=1)

    def out(acc, ex, bc):
        y, xhat, rstd = _layer_norm(alpha * ex[0] + acc, bc[0], bc[1])
        return y, y, xhat, jnp.broadcast_to(rstd, (rstd.shape[0], LANES))

    D = xf.shape[1]
    yf, yb, xhat, rstd = _mm_contract(f"{tag}_out", [cat], [w_out], False, out, [xf], [p["ln2_g"], p["ln2_b"]],
                                      [(D, F32), (D, BF16), (D, F32), (LANES, F32)])
    return yf, yb, (xb, proj, s_re, s_im, ys, yg, ygb, t, cat, xhat, rstd)


def _mixer_bwd(tag, alpha, dy, saved, p, w_in, w_glu, w_out):
    xb, proj, s_re, s_im, ys, yg, ygb, t, cat, xhat, rstd = saved
    Dh = w_in.shape[2]
    D = dy.shape[1]
    gb, gc, h = [(proj, Dh, j) for j in range(1, 4)]
    dr, dmb, dg2, db2 = _ln_bwd(f"{tag}_ln_bwd", dy, xhat, rstd, p["ln2_g"], 1.0)
    dcat, = _mm_expand(f"{tag}_dcat", dmb, [w_out], True, lambda ps, _: ps, [], [F32])
    g_wout = _wgrad_rows(f"{tag}_gwout", cat, dmb)
    half = dcat.shape[1] // 2

    def conv_b1(i, n, R, B, P, N):
        dzn, gbv, gcv, hv = R
        cw, cb, gcn = B
        v = gcv * hv
        cv, v1, v2 = _conv_taps(v, P[0] * P[1], i, cw)
        cv = cv + cb
        z = gbv * cv
        rinv = _rms_inv(z)
        dz = _rms_bwd(dzn, z, rinv, gcn)
        dcv = dz * gbv
        col = lambda a: jnp.sum(a, axis=0, keepdims=True)
        return [dz * cv, dcv], [col(dzn * z * rinv), col(dcv), col(dcv * v2), col(dcv * v1), col(dcv * v)]

    dgb, dcv, dg_conv, dconv_b, dw0, dw1, dw2 = _rowwise(
        f"{tag}_conv_b1", conv_b1, [(dcat, half, 1), gb, gc, h], bcast=[p["conv_w"], p["conv_b"], p["g_conv"]],
        outs=[(Dh, BF16), (Dh, F32)], sums=[Dh] * 5, prev=[2, 3])

    def conv_b2(i, n, R, B, P, N):
        d, gcv, hv = R
        cw = B[0]
        tm = d.shape[0]
        ext = jnp.concatenate([d, N[0] * (i < n - 1).astype(F32)], axis=0)
        d1 = pltpu.roll(ext, tm + SUBLANES - 1, 0)[:tm]
        d2 = pltpu.roll(ext, tm + SUBLANES - 2, 0)[:tm]
        dv = cw[2:3] * d + cw[1:2] * d1 + cw[0:1] * d2
        return [dv * hv, dv * gcv], []

    dgc, dh = _rowwise(f"{tag}_conv_b2", conv_b2, [dcv, gc, h], bcast=[p["conv_w"]], outs=[(Dh, BF16)] * 2, nxt=[0])

    def glu_b(i, n, R, B, P, N):
        dyn, ygv, tv = R
        sg = _sigmoid(tv)
        yy = ygv * sg
        rinv = _rms_inv(yy)
        dyy = _rms_bwd(dyn, yy, rinv, B[0])
        return [dyy * ygv * sg * (1.0 - sg), dyy * sg], [jnp.sum(dyn * yy * rinv, axis=0, keepdims=True)]

    dtb, dyg0, dg_s5 = _rowwise(f"{tag}_glu_b", glu_b, [(dcat, half, 0), yg, t], bcast=[p["g_s5"]],
                                outs=[(Dh, BF16), (Dh, F32)], sums=[Dh])

    def dys_epi(ps, ex):
        return [(ps[0] + ex[0]) * _gelu_grad(ex[1])]

    dys, = _mm_expand(f"{tag}_dys", dtb, [w_glu], True, dys_epi, [dyg0, ys], [F32])
    g_wglu = _wgrad_rows(f"{tag}_gwglu", ygb, dtb)
    both = lambda ps, ex, bx: ps
    ds_re, ds_im = _bd_mm(f"{tag}_ds", [(dys, p["wc_re"]), (dys, p["wc_imn"])], True, both, [], [], [F32, F32])
    l_re, l_im, da_re, da_im = _scan(f"{tag}_rscan", p["a_re"], -p["a_im"], ds_re, ds_im, reverse=True,
                                     state=(s_re, s_im))

    def du_epi(ps, ex, bx):
        return [ps[0] + ps[1] + bx[0] * ex[0]]

    du, = _bd_mm(f"{tag}_du", [(l_re, p["wb_re"]), (l_im, p["wb_im"])], True, du_epi, [dys], [p["d"]], [BF16])
    nblk = p["wb_re"].shape[0]
    u = (proj, Dh, 0)
    u_arr = proj[:, :Dh]
    g_wb_re = _bd_wgrad(f"{tag}_gwb_re", u_arr, l_re, nblk)
    g_wb_im = _bd_wgrad(f"{tag}_gwb_im", u_arr, l_im, nblk)
    g_wc_re = _bd_wgrad(f"{tag}_gwc_re", s_re, dys, nblk)
    g_wc_imn = _bd_wgrad(f"{tag}_gwc_im", s_im, dys, nblk)

    def dd_fn(i, n, R, B, P, N):
        return [], [jnp.sum(R[0] * R[1], axis=0, keepdims=True)]

    dd, = _rowwise(f"{tag}_dd", dd_fn, [dys, u], sums=[Dh])

    dproj = jnp.concatenate([du, dgb, dgc, dh], axis=1)

    def dxin(acc, ex, bc):
        return [alpha * ex[0] + acc]

    dx, = _mm_contract(f"{tag}_dx", [dproj], [w_in], True, dxin, [dr], [], [(D, F32)])
    g_win = _wgrad_cols(f"{tag}_gwin", xb, dproj)
    small = dict(ln2_g=dg2, ln2_b=db2, g_conv=dg_conv, conv_b=dconv_b,
                 conv_w=jnp.concatenate([dw0, dw1, dw2], axis=0), g_s5=dg_s5, d=dd,
                 da_re=da_re, da_im=da_im, wb_re=g_wb_re, wb_im=g_wb_im, wc_re=g_wc_re, wc_imn=g_wc_imn)
    return dx, (g_win, g_wglu, g_wout), small


def _swap_start(name, grads, after):
    n = len(grads)
    lands = [lax.empty((g.shape[0], g.shape[1] // 2, g.shape[2]), g.dtype) for g in grads]
    send, recv, arrays, token = _split_start(name, _swap_copies(n), n, list(grads) + lands, after)
    return (send, recv, arrays, n), token


def _swap_finish(name, tag, state, after):
    send, recv, arrays, n = state
    arrays = _split_wait(name, _swap_copies(n), send, recv, arrays, after)
    added = [_add_pairs(f"{tag}{t}", g, b) for t, (g, b) in enumerate(zip(arrays[:n], arrays[n:]))]
    return [a[0] for a in added], [a[1] for a in added]


BIG = ["ffn1_gate", "ffn1_up", "ffn1_down", "w_in", "s5_w_glu", "w_out", "ffn2_gate", "ffn2_up", "ffn2_down"]
SMALL = ["ln1_g", "ln1_b", "s5_lam_re", "s5_lam_im", "s5_log_dt", "s5_b_re", "s5_b_im", "s5_c_re", "s5_c_im", "s5_d",
         "conv_w", "conv_b", "g_s5", "g_conv", "ln2_g", "ln2_b", "ln3_g", "ln3_b"]
WEIGHTS = ['ffn1_gate', 'ffn1_up', 'ffn1_down', 'ln1_g', 'ln1_b', 'w_in', 's5_lam_re', 's5_lam_im', 's5_log_dt',
           's5_b_re', 's5_b_im', 's5_c_re', 's5_c_im', 's5_d', 's5_w_glu', 'conv_w', 'conv_b', 'g_s5', 'g_conv',
           'w_out', 'ln2_g', 'ln2_b', 'ffn2_gate', 'ffn2_up', 'ffn2_down', 'ln3_g', 'ln3_b']


def _step(W, M, V, x, target):
    depth = W["ffn1_gate"].shape[0]
    alpha = (2.0 * depth) ** 0.25
    L, D = x.shape
    G, N = W["s5_lam_re"].shape[1:]
    P = W["s5_b_re"].shape[3]
    Dh = G * P
    chip = 2 * lax.axis_index("x") + lax.axis_index("y")
    cw_cols = W["conv_w"].shape[2]

    cw_rows = -(-depth * CONV_W // SUBLANES) * SUBLANES
    cw_buf = jnp.zeros((N_CHIPS, 2 * cw_rows, cw_cols), F32)
    cw_buf = lax.dynamic_update_slice(cw_buf, W["conv_w"].reshape(1, depth * CONV_W, cw_cols), (chip, 0, 0))

    T = len(BIG)
    gather, gather0 = _gather_copies(T), _gather_copies(T + 1)
    placed = [[_cast_place(f"place{l}_{n}", W[n], l) for n in BIG] for l in range(depth)]
    g_state = _split_start("gather0_start", gather0, 3 * (T + 1), placed[0] + [cw_buf], [x])

    s5_vjps, lp = [], []
    for l in range(depth):
        (a_re, a_im, bb_re, bb_im), vjp = jax.vjp(_s5_discretize, W["s5_lam_re"][l], W["s5_lam_im"][l],
                                                  W["s5_log_dt"][l], W["s5_b_re"][l], W["s5_b_im"][l])
        s5_vjps.append(vjp)
        row = lambda a: a.reshape(1, -1)
        lp.append(dict(
            a_re=row(a_re), a_im=row(a_im), wb_re=_bd_build(bb_re), wb_im=_bd_build(bb_im),
            wc_re=_bd_build(W["s5_c_re"][l].transpose(0, 2, 1)).transpose(0, 2, 1),
            wc_imn=-_bd_build(W["s5_c_im"][l].transpose(0, 2, 1)).transpose(0, 2, 1),
            d=row(W["s5_d"][l]), g_s5=row(W["g_s5"][l]), g_conv=row(W["g_conv"][l]), conv_b=row(W["conv_b"][l]),
            ln2_g=row(W["ln2_g"][l]), ln2_b=row(W["ln2_b"][l])))

    def cast(i, n, R, B, P_, N_):
        return [R[0]], []

    xb, = _rowwise("cast_x", cast, [x], outs=[(D, BF16)])
    xf = x
    saved, gathered = [], []
    early = [xb] + [a for p in placed[1:] for a in p] + [q[k] for q in lp for k in ("wb_re", "wb_im", "wc_re", "wc_imn")]
    for l in range(depth):
        bufs = _split_wait(f"gather{l}_wait", gather0 if l == 0 else gather, g_state[0], g_state[1], g_state[2],
                           early if l == 0 else [xb])
        token = bufs[0]
        if l + 1 < depth:
            g_state = _split_start(f"gather{l + 1}_start", gather, 3 * T, placed[l + 1], [bufs[0]])
            token = g_state[3]
        passed = _gather_pass_on(f"gather{l}_pass", bufs, token)
        if l == 0:
            cw = passed[T][:, :depth * CONV_W, :].reshape(N_CHIPS, depth, CONV_W, cw_cols)
            cw = cw.transpose(1, 2, 0, 3).reshape(depth, CONV_W, N_CHIPS * cw_cols)
            for q in range(depth):
                lp[q]["conv_w"] = cw[q]
        gathered.append(passed[:T])
        wg1, wu1, wd1, w_in, w_glu, w_out, wg2, wu2, wd2 = gathered[l]
        row = lambda a: a.reshape(1, -1)
        xf, xb, s1 = _ffn_fwd(f"l{l}_ffn1", alpha, xf, xb, wg1, wu1, wd1, row(W["ln1_g"][l]), row(W["ln1_b"][l]))
        xf, xb, s2 = _mixer_fwd(f"l{l}_mix", alpha, xf, xb, lp[l], w_in, w_glu, w_out)
        xf, xb, s3 = _ffn_fwd(f"l{l}_ffn2", alpha, xf, xb, wg2, wu2, wd2, row(W["ln3_g"][l]), row(W["ln3_b"][l]))
        saved.append((s1, s2, s3))

    def loss_fn(i, n, R, B, P_, N_):
        err = R[0] - R[1]
        return [err * (1.0 / D)], [jnp.sum(0.5 * err * err * (1.0 / D), axis=0, keepdims=True)]

    dx, loss_cols = _rowwise("loss", loss_fn, [xf, target], outs=[(D, F32)], sums=[D])
    loss = lax.psum(jnp.sum(loss_cols), ("x", "y", "c"))

    big_g = {n: [None] * depth for n in BIG}
    small_g = {n: [None] * depth for n in SMALL}
    xchg, join = _xchg_copies(T), _join_copies(T)
    x_state, tok, joins = None, 0.0, []

    def finish(layer, lands, after):
        fulls = [_sum_chips(f"l{layer}_red_sum{t}", g) for t, g in enumerate(lands)]
        joins.append((layer, _split_start(f"l{layer}_join_start", join, T, fulls, after)))

    for l in reversed(range(depth)):
        wg1, wu1, wd1, w_in, w_glu, w_out, wg2, wu2, wd2 = gathered[l]
        s1, s2, s3 = saved[l]
        row = lambda a: a.reshape(1, -1)
        dx, g2, dg3, db3 = _ffn_bwd(f"l{l}_ffn2", alpha, dx, s3, wg2, wu2, wd2, row(W["ln3_g"][l]) + tok)
        sw2, tok2 = _swap_start(f"l{l}_swap2_start", g2, [dx])
        dx, gm, sm = _mixer_bwd(f"l{l}_mix", alpha, dx, s2, dict(lp[l], ln2_g=lp[l]["ln2_g"] + tok2[0, 0]),
                                w_in, w_glu, w_out)
        swm, tokm = _swap_start(f"l{l}_swapm_start", gm, [dx])
        dx, g1, dg1, db1 = _ffn_bwd(f"l{l}_ffn1", alpha, dx, s1, wg1, wu1, wd1, row(W["ln1_g"][l]) + tokm[0, 0])
        sw1, _ = _swap_start(f"l{l}_swap1_start", g1, [dx])
        p2, l2 = _swap_finish(f"l{l}_swap2_wait", f"l{l}_add2_", sw2, [dx])
        pm, lm = _swap_finish(f"l{l}_swapm_wait", f"l{l}_addm_", swm, [dx])
        p1, l1 = _swap_finish(f"l{l}_swap1_wait", f"l{l}_add1_", sw1, [dx])
        arrays = p1 + pm + p2 + l1 + lm + l2
        after, prev = [arrays[0]], None
        if x_state is not None:
            prev = _split_wait(f"l{l + 1}_xchg_wait", xchg, x_state[0], x_state[1], x_state[2], [arrays[0]])
            after = [prev[T]]
        x_state = _split_start(f"l{l}_xchg_start", xchg, 3 * T, arrays, after)
        tok = x_state[3][0, 0]
        if prev is not None:
            finish(l + 1, prev[T:], [x_state[3]])
        d_bb_re = _bd_extract(sm["wb_re"], G, N, P)
        d_bb_im = _bd_extract(sm["wb_im"], G, N, P)
        d_lre, d_lim, d_ldt, d_bre, d_bim = s5_vjps[l]((sm["da_re"].reshape(G, N), sm["da_im"].reshape(G, N),
                                                        d_bb_re, d_bb_im))
        d_cre = _bd_extract(sm["wc_re"].transpose(0, 2, 1), G, N, P).transpose(0, 2, 1)
        d_cim = -_bd_extract(sm["wc_imn"].transpose(0, 2, 1), G, N, P).transpose(0, 2, 1)
        vals = dict(ln1_g=dg1, ln1_b=db1, s5_lam_re=d_lre, s5_lam_im=d_lim, s5_log_dt=d_ldt, s5_b_re=d_bre,
                    s5_b_im=d_bim, s5_c_re=d_cre, s5_c_im=d_cim, s5_d=sm["d"], conv_w=sm["conv_w"],
                    conv_b=sm["conv_b"], g_s5=sm["g_s5"], g_conv=sm["g_conv"], ln2_g=sm["ln2_g"], ln2_b=sm["ln2_b"],
                    ln3_g=dg3, ln3_b=db3)
        for n in SMALL:
            small_g[n][l] = vals[n].reshape((W[n].shape[1:] if n != "conv_w" else (CONV_W, N_CHIPS * cw_cols)))
    grad_x = dx

    sg = [jnp.stack(small_g[n]) for n in SMALL]
    packed, _ = _pack(sg, SUBLANES * N_DEV)
    packed = _all_reduce_small("small_allreduce", packed.reshape(N_DEV, -1, LANES)).reshape(-1, LANES)
    last = _split_wait("l0_xchg_wait", xchg, x_state[0], x_state[1], x_state[2], [packed])
    finish(0, last[T:], [last[T]])
    for layer, (send, recv, fulls, _) in joins:
        fulls = _split_wait(f"l{layer}_join_wait", join, send, recv, fulls, [packed])
        for n, r in zip(BIG, fulls):
            big_g[n][layer] = r
    sg = dict(zip(SMALL, _unpack(packed, sg)))
    sg["conv_w"] = lax.dynamic_slice_in_dim(sg["conv_w"], chip * cw_cols, cw_cols, axis=2)
    gp, _ = _pack([sg[n] for n in SMALL], SUBLANES)
    wp, _ = _pack([W[n] for n in SMALL], SUBLANES)
    mp, _ = _pack([M[n] for n in SMALL], SUBLANES)
    vp, _ = _pack([V[n] for n in SMALL], SUBLANES)
    like = [W[n] for n in SMALL]
    dsm, msm, vsm = [dict(zip(SMALL, _unpack(a, like))) for a in _adamw_small("adamw_small", wp, gp, mp, vp)]

    grads, deltas, new_m, new_v = dict(sg), dsm, msm, vsm
    for n in BIG:
        shp = W[n].shape
        flat = lambda a: a.reshape(shp[0], shp[1], shp[2])
        gr, de, mn, vn = _adamw_big(f"adamw_{n}", flat(W[n]), flat(M[n]), flat(V[n]), big_g[n])
        grads[n], deltas[n], new_m[n], new_v[n] = gr, de, mn, vn

    outs = [loss, grad_x[None]]
    for d in (grads, deltas, new_m, new_v):
        outs += [d[n] for n in WEIGHTS]
    return tuple(outs)


def kernel(x, ffn1_gate, ffn1_up, ffn1_down, ln1_g, ln1_b, w_in, s5_lam_re, s5_lam_im, s5_log_dt, s5_b_re, s5_b_im, s5_c_re, s5_c_im, s5_d, s5_w_glu, conv_w, conv_b, g_s5, g_conv, w_out, ln2_g, ln2_b, ffn2_gate, ffn2_up, ffn2_down, ln3_g, ln3_b, loss_target, m_ffn1_gate, m_ffn1_up, m_ffn1_down, m_ln1_g, m_ln1_b, m_w_in, m_s5_lam_re, m_s5_lam_im, m_s5_log_dt, m_s5_b_re, m_s5_b_im, m_s5_c_re, m_s5_c_im, m_s5_d, m_s5_w_glu, m_conv_w, m_conv_b, m_g_s5, m_g_conv, m_w_out, m_ln2_g, m_ln2_b, m_ffn2_gate, m_ffn2_up, m_ffn2_down, m_ln3_g, m_ln3_b, v_ffn1_gate, v_ffn1_up, v_ffn1_down, v_ln1_g, v_ln1_b, v_w_in, v_s5_lam_re, v_s5_lam_im, v_s5_log_dt, v_s5_b_re, v_s5_b_im, v_s5_c_re, v_s5_c_im, v_s5_d, v_s5_w_glu, v_conv_w, v_conv_b, v_g_s5, v_g_conv, v_w_out, v_ln2_g, v_ln2_b, v_ffn2_gate, v_ffn2_up, v_ffn2_down, v_ln3_g, v_ln3_b):
    a = dict(locals())
    W = {n: a[n] for n in WEIGHTS}
    M = {n: a["m_" + n] for n in WEIGHTS}
    V = {n: a["v_" + n] for n in WEIGHTS}
    return _step(W, M, V, x[0], loss_target[0])
```

```python
import functools

import jax
import jax.numpy as jnp
from jax import lax
from jax.experimental import pallas as pl
from jax.experimental.pallas import tpu as pltpu

F32 = jnp.float32
BF16 = jnp.bfloat16
MESH = pl.DeviceIdType.MESH
HIGH = lax.Precision.HIGHEST

N_CHIPS = 4
N_DEV = 8
V7X_VMEM_LIMIT = 56 * 1024 * 1024
SUBLANES = 8
LANES = 128
S5_P = 16
S5_N = 64
S5_GB = 8
CONV_W = 3
LN_EPS = 1e-5
RMS_EPS = 1e-6
ADAM_LR = 0.001
ADAM_B1 = 0.9
ADAM_B2 = 0.999
ADAM_EPS = 1e-08
ADAM_WD = 0.01
ADAM_STEP = 10
GELU_K = 0.7978845608028654
GELU_C = 0.044715


def _params():
    return pltpu.CompilerParams(vmem_limit_bytes=V7X_VMEM_LIMIT)


def _tile(n, pref, mult=SUBLANES):
    best = None
    for t in range(mult, min(n, pref) + 1, mult):
        if n % t == 0:
            best = t
    return best if best is not None else n


def _mm(a, b, precision=None):
    return jnp.dot(a, b, preferred_element_type=F32, precision=precision)


def _mm_nt(a, b, precision=None):
    return lax.dot_general(a, b, (((1,), (1,)), ((), ())), preferred_element_type=F32, precision=precision)


def _mm_tn(a, b, precision=None):
    return lax.dot_general(a, b, (((0,), (0,)), ((), ())), preferred_element_type=F32, precision=precision)


def _sigmoid(x):
    return 1.0 / (1.0 + jnp.exp(-x))


def _gelu(x):
    return 0.5 * x * (1.0 + jnp.tanh(GELU_K * (x + GELU_C * x * x * x)))


def _gelu_grad(x):
    th = jnp.tanh(GELU_K * (x + GELU_C * x * x * x))
    return 0.5 * (1.0 + th) + 0.5 * x * (1.0 - th * th) * GELU_K * (1.0 + 3.0 * GELU_C * x * x)


def _layer_norm(r, g, b):
    mu = jnp.mean(r, axis=-1, keepdims=True)
    xc = r - mu
    rstd = lax.rsqrt(jnp.mean(xc * xc, axis=-1, keepdims=True) + LN_EPS)
    xhat = xc * rstd
    return xhat * g + b, xhat, rstd


def _layer_norm_bwd(dy, xhat, rstd, g):
    dxh = dy * g
    m1 = jnp.mean(dxh, axis=-1, keepdims=True)
    m2 = jnp.mean(dxh * xhat, axis=-1, keepdims=True)
    return rstd * (dxh - m1 - xhat * m2)


def _rms_inv(x):
    return lax.rsqrt(jnp.mean(x * x, axis=-1, keepdims=True) + RMS_EPS)


def _rms_bwd(dy, x, rinv, g):
    dxh = dy * g
    return rinv * dxh - x * (rinv * rinv * rinv) * jnp.mean(dxh * x, axis=-1, keepdims=True)


def _rowwise(name, fn, rows, bcast=(), outs=(), sums=(), prev=(), nxt=(), tm=256):
    rows = [r if isinstance(r, tuple) else (r, r.shape[1], 0) for r in rows]
    L = rows[0][0].shape[0]
    tm = _tile(L, tm)
    n = L // tm
    hb = tm // SUBLANES
    nh = L // SUBLANES
    nr, nb, npv, nnx, no, ns = len(rows), len(bcast), len(prev), len(nxt), len(outs), len(sums)

    def body(*refs):
        i = pl.program_id(0)
        k = 0
        R = [r[...] for r in refs[k:k + nr]]; k += nr
        B = [r[...] for r in refs[k:k + nb]]; k += nb
        P = [r[...] for r in refs[k:k + npv]]; k += npv
        N = [r[...] for r in refs[k:k + nnx]]; k += nnx
        o_refs = refs[k:k + no]; k += no
        s_refs = refs[k:k + ns]
        O, S = fn(i, n, R, B, P, N)
        for ref, val in zip(o_refs, O):
            ref[...] = val.astype(ref.dtype)
        if ns:
            @pl.when(i == 0)
            def _():
                for ref in s_refs:
                    ref[...] = jnp.zeros_like(ref)
            for ref, val in zip(s_refs, S):
                ref[...] += val

    in_specs = [pl.BlockSpec((tm, w), functools.partial(lambda i, cb: (i, cb), cb=cb)) for _, w, cb in rows]
    in_specs += [pl.BlockSpec(b.shape, lambda i: (0, 0)) for b in bcast]
    in_specs += [pl.BlockSpec((SUBLANES, rows[j][1]),
                              functools.partial(lambda i, cb: (jnp.maximum(i * hb - 1, 0), cb), cb=rows[j][2]))
                 for j in prev]
    in_specs += [pl.BlockSpec((SUBLANES, rows[j][1]),
                              functools.partial(lambda i, cb: (jnp.minimum((i + 1) * hb, nh - 1), cb), cb=rows[j][2]))
                 for j in nxt]
    out_specs = [pl.BlockSpec((tm, w), lambda i: (i, 0)) for w, _ in outs]
    out_specs += [pl.BlockSpec((1, w), lambda i: (0, 0)) for w in sums]
    out_shape = [jax.ShapeDtypeStruct((L, w), dt) for w, dt in outs]
    out_shape += [jax.ShapeDtypeStruct((1, w), F32) for w in sums]
    args = [r[0] for r in rows] + list(bcast) + [rows[j][0] for j in prev] + [rows[j][0] for j in nxt]
    return pl.pallas_call(body, name=name, grid=(n,), in_specs=in_specs, out_specs=out_specs,
                          out_shape=out_shape, compiler_params=_params())(*args)


def _mm_expand(name, a, ws, nt, epi, extras, outs, tm=512):
    L, K = a.shape
    tm = _tile(L, tm)
    nw, ne = len(ws), len(extras)
    co = ws[0].shape[1] if nt else ws[0].shape[2]

    def body(a_ref, *refs):
        av = a_ref[...]
        ps = [(_mm_nt if nt else _mm)(av, w[...]) for w in refs[:nw]]
        vals = epi(ps, [e[...] for e in refs[nw:nw + ne]])
        for ref, val in zip(refs[nw + ne:], vals):
            ref[...] = val.astype(ref.dtype)

    in_specs = [pl.BlockSpec((tm, K), lambda j, i: (i, 0))]
    in_specs += [pl.BlockSpec((None,) + w.shape[1:], lambda j, i: (j, 0, 0)) for w in ws]
    in_specs += [pl.BlockSpec((tm, co), lambda j, i: (i, j)) for _ in extras]
    out_specs = [pl.BlockSpec((tm, co), lambda j, i: (i, j)) for _ in outs]
    out_shape = [jax.ShapeDtypeStruct((L, N_CHIPS * co), dt) for dt in outs]
    return pl.pallas_call(body, name=name, grid=(N_CHIPS, L // tm), in_specs=in_specs, out_specs=out_specs,
                          out_shape=out_shape, compiler_params=_params())(a, *ws, *extras)


def _mm_contract(name, as_, ws, nt, epi, extras, bcast, outs, tm=512):
    L = as_[0].shape[0]
    tm = _tile(L, tm)
    na, ne, nb = len(as_), len(extras), len(bcast)
    cb = as_[0].shape[1] // N_CHIPS
    n_out = ws[0].shape[1] if nt else ws[0].shape[2]

    def body(*refs):
        k = pl.program_id(1)
        a_refs, w_refs = refs[:na], refs[na:2 * na]
        e_refs = refs[2 * na:2 * na + ne]
        b_refs = refs[2 * na + ne:2 * na + ne + nb]
        o_refs = refs[2 * na + ne + nb:-1]
        acc = refs[-1]

        @pl.when(k == 0)
        def _():
            acc[...] = jnp.zeros_like(acc)

        part = None
        for a_ref, w_ref in zip(a_refs, w_refs):
            p = (_mm_nt if nt else _mm)(a_ref[...], w_ref[...])
            part = p if part is None else part + p
        acc[...] += part

        @pl.when(k == N_CHIPS - 1)
        def _():
            vals = epi(acc[...], [e[...] for e in e_refs], [b[...] for b in b_refs])
            for ref, val in zip(o_refs, vals):
                ref[...] = val.astype(ref.dtype)

    in_specs = [pl.BlockSpec((tm, cb), lambda i, k: (i, k)) for _ in as_]
    in_specs += [pl.BlockSpec((None,) + w.shape[1:], lambda i, k: (k, 0, 0)) for w in ws]
    in_specs += [pl.BlockSpec((tm, e.shape[1]), lambda i, k: (i, 0)) for e in extras]
    in_specs += [pl.BlockSpec(b.shape, lambda i, k: (0, 0)) for b in bcast]
    out_specs = [pl.BlockSpec((tm, w), lambda i, k: (i, 0)) for w, _ in outs]
    out_shape = [jax.ShapeDtypeStruct((L, w), dt) for w, dt in outs]
    return pl.pallas_call(body, name=name, grid=(L // tm, N_CHIPS), in_specs=in_specs, out_specs=out_specs,
                          out_shape=out_shape, scratch_shapes=[pltpu.VMEM((tm, n_out), F32)],
                          compiler_params=_params())(*as_, *ws, *extras, *bcast)


def _wgrad_cols(name, a, b, tk=1024):
    L, K = a.shape
    C = b.shape[1] // N_CHIPS
    tk = _tile(K, tk, LANES)

    def body(a_ref, b_ref, o_ref):
        o_ref[...] = _mm_tn(a_ref[...], b_ref[...]).astype(o_ref.dtype)

    return pl.pallas_call(
        body, name=name, grid=(N_CHIPS, K // tk),
        in_specs=[pl.BlockSpec((L, tk), lambda j, kb: (0, kb)), pl.BlockSpec((L, C), lambda j, kb: (0, j))],
        out_specs=pl.BlockSpec((None, tk, C), lambda j, kb: (j, kb, 0)),
        out_shape=jax.ShapeDtypeStruct((N_CHIPS, K, C), BF16), compiler_params=_params())(a, b)


def _wgrad_rows(name, a, b, tr=512):
    L, N = b.shape
    R = a.shape[1] // N_CHIPS
    tr = _tile(R, tr, LANES)
    nrb = R // tr

    def body(a_ref, b_ref, o_ref):
        o_ref[...] = _mm_tn(a_ref[...], b_ref[...]).astype(o_ref.dtype)

    return pl.pallas_call(
        body, name=name, grid=(N_CHIPS, nrb),
        in_specs=[pl.BlockSpec((L, tr), lambda j, rb: (0, j * nrb + rb)), pl.BlockSpec((L, N), lambda j, rb: (0, 0))],
        out_specs=pl.BlockSpec((None, tr, N), lambda j, rb: (j, rb, 0)),
        out_shape=jax.ShapeDtypeStruct((N_CHIPS, R, N), BF16), compiler_params=_params())(a, b)


def _bd_mm(name, pairs, nt, epi, extras, bex, outs, tm=1024):
    L = pairs[0][0].shape[0]
    nblk = pairs[0][1].shape[0]
    tm = _tile(L, tm)
    npair, ne, nx = len(pairs), len(extras), len(bex)
    w0 = pairs[0][1]
    ca, co = (w0.shape[2], w0.shape[1]) if nt else (w0.shape[1], w0.shape[2])

    def body(*refs):
        ps = [(_mm_nt if nt else _mm)(refs[2 * q][...], refs[2 * q + 1][...], HIGH) for q in range(npair)]
        k = 2 * npair
        vals = epi(ps, [e[...] for e in refs[k:k + ne]], [e[...] for e in refs[k + ne:k + ne + nx]])
        for ref, val in zip(refs[k + ne + nx:], vals):
            ref[...] = val.astype(ref.dtype)

    in_specs = []
    args = []
    for a, w in pairs:
        in_specs += [pl.BlockSpec((tm, ca), lambda i, g: (i, g)), pl.BlockSpec((None,) + w.shape[1:], lambda i, g: (g, 0, 0))]
        args += [a, w]
    in_specs += [pl.BlockSpec((tm, co), lambda i, g: (i, g)) for _ in extras]
    in_specs += [pl.BlockSpec((1, co), lambda i, g: (0, g)) for _ in bex]
    out_specs = [pl.BlockSpec((tm, co), lambda i, g: (i, g)) for _ in outs]
    out_shape = [jax.ShapeDtypeStruct((L, nblk * co), dt) for dt in outs]
    return pl.pallas_call(body, name=name, grid=(L // tm, nblk), in_specs=in_specs, out_specs=out_specs,
                          out_shape=out_shape, compiler_params=_params())(*args, *extras, *bex)


def _bd_wgrad(name, a, b, nblk):
    L = a.shape[0]
    ca, cb = a.shape[1] // nblk, b.shape[1] // nblk

    def body(a_ref, b_ref, o_ref):
        o_ref[...] = _mm_tn(a_ref[...], b_ref[...], HIGH)

    return pl.pallas_call(
        body, name=name, grid=(nblk,),
        in_specs=[pl.BlockSpec((L, ca), lambda g: (0, g)), pl.BlockSpec((L, cb), lambda g: (0, g))],
        out_specs=pl.BlockSpec((None, ca, cb), lambda g: (g, 0, 0)),
        out_shape=jax.ShapeDtypeStruct((nblk, ca, cb), F32), compiler_params=_params())(a, b)


def _cmul(ar, ai, br, bi):
    return ar * br - ai * bi, ar * bi + ai * br


def _scan(name, ar, ai, xr, xi, reverse=False, state=None, tc=512):
    L, S = xr.shape
    tc = _tile(S, tc, LANES)
    nt = L // SUBLANES
    with_da = state is not None

    def body(*refs):
        ar_ref, ai_ref, xr_ref, xi_ref = refs[:4]
        if with_da:
            sr_ref, si_ref, yr_ref, yi_ref, dar_ref, dai_ref = refs[4:]
        else:
            yr_ref, yi_ref = refs[4:]
        a1 = (ar_ref[...], ai_ref[...])
        pw = [a1]
        for _ in range(SUBLANES - 1):
            pw.append(_cmul(*pw[-1], *a1))
        row = lax.broadcasted_iota(jnp.int32, (SUBLANES, tc), 0)
        tr = jnp.zeros((SUBLANES, tc), F32)
        ti = jnp.zeros((SUBLANES, tc), F32)
        for t in range(SUBLANES):
            p = pw[SUBLANES - 1 - t] if reverse else pw[t]
            tr = jnp.where(row == t, p[0], tr)
            ti = jnp.where(row == t, p[1], ti)

        def step(n, carry):
            idx = (nt - 1 - n) if reverse else n
            rows = pl.ds(pl.multiple_of(idx * SUBLANES, SUBLANES), SUBLANES)
            vr, vi = xr_ref[rows, :], xi_ref[rows, :]
            for d in (1, 2, 4):
                pr, pi = pw[d - 1]
                if reverse:
                    qr, qi = pltpu.roll(vr, SUBLANES - d, 0), pltpu.roll(vi, SUBLANES - d, 0)
                    keep = row < SUBLANES - d
                else:
                    qr, qi = pltpu.roll(vr, d, 0), pltpu.roll(vi, d, 0)
                    keep = row >= d
                mr, mi = _cmul(pr, pi, qr, qi)
                vr = vr + jnp.where(keep, mr, 0.0)
                vi = vi + jnp.where(keep, mi, 0.0)
            cr, ci = carry[0], carry[1]
            mr, mi = _cmul(tr, ti, cr, ci)
            vr, vi = vr + mr, vi + mi
            yr_ref[rows, :] = vr
            yi_ref[rows, :] = vi
            edge = 0 if reverse else SUBLANES - 1
            new = (vr[edge:edge + 1, :], vi[edge:edge + 1, :])
            if not with_da:
                return new
            pidx = jnp.maximum(idx - 1, 0)
            prows = pl.ds(pl.multiple_of(pidx * SUBLANES, SUBLANES), SUBLANES)
            live = (idx > 0).astype(F32)
            s0r = sr_ref[prows, :][SUBLANES - 1:SUBLANES, :] * live
            s0i = si_ref[prows, :][SUBLANES - 1:SUBLANES, :] * live
            spr = jnp.where(row == 0, s0r, pltpu.roll(sr_ref[rows, :], 1, 0))
            spi = jnp.where(row == 0, s0i, pltpu.roll(si_ref[rows, :], 1, 0))
            return new + (carry[2] + vr * spr + vi * spi, carry[3] + vi * spr - vr * spi)

        zero = jnp.zeros((1, tc), F32)
        init = (zero, zero)
        if with_da:
            acc0 = jnp.zeros((SUBLANES, tc), F32)
            init = init + (acc0, acc0)
        fin = lax.fori_loop(0, nt, step, init, unroll=2)
        if with_da:
            dar_ref[...] = jnp.sum(fin[2], axis=0, keepdims=True)
            dai_ref[...] = jnp.sum(fin[3], axis=0, keepdims=True)

    col = pl.BlockSpec((L, tc), lambda j: (0, j))
    vec = pl.BlockSpec((1, tc), lambda j: (0, j))
    n_in = 6 if with_da else 4
    in_specs = [vec, vec] + [col] * (n_in - 2)
    out_specs = [col, col] + ([vec, vec] if with_da else [])
    out_shape = [jax.ShapeDtypeStruct((L, S), F32)] * 2 + ([jax.ShapeDtypeStruct((1, S), F32)] * 2 if with_da else [])
    args = (ar, ai, xr, xi) + (tuple(state) if with_da else ())
    return pl.pallas_call(body, name=name, grid=(S // tc,), in_specs=in_specs, out_specs=out_specs,
                          out_shape=out_shape, compiler_params=_params())(*args)


def _bd_build(w_gnp):
    G, N, P = w_gnp.shape
    nb = G // S5_GB
    eye = jnp.eye(S5_GB, dtype=F32)
    x = w_gnp.reshape(nb, S5_GB, N, P).transpose(0, 1, 3, 2)
    w = x[:, :, :, None, :] * eye[None, :, None, :, None]
    return w.reshape(nb, S5_GB * P, S5_GB * N)


def _bd_extract(w, G, N, P):
    nb = G // S5_GB
    eye = jnp.eye(S5_GB, dtype=F32)
    w5 = w.reshape(nb, S5_GB, P, S5_GB, N)
    d = jnp.sum(w5 * eye[None, :, None, :, None], axis=3)
    return d.transpose(0, 1, 3, 2).reshape(G, N, P)


def _s5_discretize(lam_re, lam_im, log_dt, b_re, b_im):
    dt = jnp.exp(log_dt)[:, None]
    mag = jnp.exp(lam_re * dt)
    ang = lam_im * dt
    ab_re = mag * jnp.cos(ang)
    ab_im = mag * jnp.sin(ang)
    den = lam_re * lam_re + lam_im * lam_im
    nr = ab_re - 1.0
    ni = ab_im
    q_re = (nr * lam_re + ni * lam_im) / den
    q_im = (ni * lam_re - nr * lam_im) / den
    bb_re = q_re[..., None] * b_re - q_im[..., None] * b_im
    bb_im = q_re[..., None] * b_im + q_im[..., None] * b_re
    return ab_re, ab_im, bb_re, bb_im


HBM_SPEC = pl.BlockSpec(memory_space=pl.ANY)


def _place():
    x, y, c = lax.axis_index("x"), lax.axis_index("y"), lax.axis_index("c")
    others = [(1 - x, y), (x, 1 - y), (1 - x, 1 - y)]
    return x, y, c, 2 * x + y, others


def _half(ref, h, axis):
    n = ref.shape[axis] // 2
    idx = [slice(None)] * len(ref.shape)
    idx[axis] = pl.ds(h * n, n)
    return ref.at[tuple(idx)]


def _chip():
    return 2 * lax.axis_index("x") + lax.axis_index("y")


def _cast_place(name, w, layer):
    _, R, C = w.shape
    tr = _tile(R, max(16, (1 << 19) // C), 16)

    def body(w_ref, o_ref):
        o_ref[...] = w_ref[...].astype(o_ref.dtype)

    return pl.pallas_call(
        body, name=name, grid=(R // tr,),
        in_specs=[pl.BlockSpec((None, tr, C), lambda i: (layer, i, 0))],
        out_specs=pl.BlockSpec((None, tr, C), lambda i: (_chip(), i, 0)),
        out_shape=jax.ShapeDtypeStruct((N_CHIPS, R, C), BF16), compiler_params=_params())(w)


HBM_ONLY = pl.BlockSpec(memory_space=pltpu.HBM)
SEM_SPEC = pl.BlockSpec(memory_space=pltpu.SEMAPHORE)
EFFECT = pltpu.SideEffectType.DATAFLOW_SIDE_EFFECTING


def _split_start(name, copies_fn, n_copies, arrays, after):
    n, na = len(arrays), len(after)

    def body(*refs):
        ins, send, recv, token = refs[:n], refs[n + na], refs[n + na + 1], refs[-1]
        for cp in copies_fn(ins, send, recv):
            cp.start()
        token[...] = jnp.zeros_like(token)

    n_sem = (n_copies,)
    outs = pl.pallas_call(
        body, name=name,
        out_shape=(pltpu.SemaphoreType.DMA(n_sem), pltpu.SemaphoreType.DMA(n_sem),
                   *[pltpu.HBM(a.shape, a.dtype) for a in arrays], jax.ShapeDtypeStruct((SUBLANES, LANES), F32)),
        in_specs=[HBM_ONLY] * n + [HBM_SPEC] * na,
        out_specs=(SEM_SPEC, SEM_SPEC, *[HBM_ONLY] * n, pl.BlockSpec(memory_space=pltpu.VMEM)),
        input_output_aliases={i: 2 + i for i in range(n)},
        compiler_params=pltpu.CompilerParams(has_side_effects=EFFECT),
    )(*[pltpu.with_memory_space_constraint(a, pltpu.HBM) for a in arrays], *after)
    return outs[0], outs[1], list(outs[2:2 + n]), outs[-1]


def _split_wait(name, copies_fn, send, recv, arrays, after):
    n, na = len(arrays), len(after)

    def body(*refs):
        ins, send_ref, recv_ref = refs[:n], refs[n], refs[n + 1]
        for cp in copies_fn(ins, send_ref, recv_ref):
            cp.wait_send()
            cp.wait_recv()

    outs = pl.pallas_call(
        body, name=name, out_shape=tuple(pltpu.HBM(a.shape, a.dtype) for a in arrays),
        in_specs=[HBM_ONLY] * n + [SEM_SPEC, SEM_SPEC] + [HBM_SPEC] * na, out_specs=(HBM_ONLY,) * n,
        input_output_aliases={i: i for i in range(n)},
        compiler_params=pltpu.CompilerParams(has_side_effects=EFFECT),
    )(*arrays, send, recv, *after)
    return list(outs)


def _gather_copies(T):
    def copies(bufs, send, recv):
        x, y, c, me, others = _place()
        cps = []
        for t in range(T):
            mine = _half(bufs[t].at[me], c, 0)
            for k, (ox, oy) in enumerate(others):
                cps.append(pltpu.make_async_remote_copy(
                    src_ref=mine, dst_ref=mine, send_sem=send.at[3 * t + k], recv_sem=recv.at[3 * t + k],
                    device_id=(ox, oy, c), device_id_type=MESH))
        return cps
    return copies


def _xchg_copies(T):
    def copies(arrays, send, recv):
        x, y, c, me, others = _place()
        cps = []
        for t in range(T):
            for k, (ox, oy) in enumerate(others):
                cps.append(pltpu.make_async_remote_copy(
                    src_ref=arrays[t].at[2 * ox + oy], dst_ref=arrays[T + t].at[me], send_sem=send.at[3 * t + k],
                    recv_sem=recv.at[3 * t + k], device_id=(ox, oy, c), device_id_type=MESH))
        return cps
    return copies


def _gather_pass_on(name, bufs, after):
    T = len(bufs)

    def body(*refs):
        outs = refs[T + 1:2 * T + 1]
        send, recv = refs[2 * T + 1:]
        x, y, c, me, others = _place()
        cps = []
        for t in range(T):
            for k, (ox, oy) in enumerate(others):
                got = _half(outs[t].at[2 * ox + oy], c, 0)
                cp = pltpu.make_async_remote_copy(
                    src_ref=got, dst_ref=got, send_sem=send.at[t, k], recv_sem=recv.at[t, k],
                    device_id=(x, y, 1 - c), device_id_type=MESH)
                cp.start()
                cps.append(cp)
        for cp in cps:
            cp.wait()

    return pl.pallas_call(
        body, name=name, in_specs=[HBM_SPEC] * (T + 1), out_specs=[HBM_SPEC] * T,
        out_shape=[jax.ShapeDtypeStruct(b.shape, b.dtype) for b in bufs],
        input_output_aliases={t: t for t in range(T)},
        scratch_shapes=[pltpu.SemaphoreType.DMA((T, 3)), pltpu.SemaphoreType.DMA((T, 3))],
    )(*bufs, after)


def _swap_copies(T):
    def copies(arrays, send, recv):
        x, y, c, me, others = _place()
        return [pltpu.make_async_remote_copy(
            src_ref=_half(arrays[t], 1 - c, 1), dst_ref=arrays[T + t], send_sem=send.at[t], recv_sem=recv.at[t],
            device_id=(x, y, 1 - c), device_id_type=MESH) for t in range(T)]
    return copies


def _join_copies(T):
    def copies(arrays, send, recv):
        x, y, c, me, others = _place()
        cps = []
        for t in range(T):
            mine = _half(arrays[t], c, 0)
            cps.append(pltpu.make_async_remote_copy(
                src_ref=mine, dst_ref=mine, send_sem=send.at[t], recv_sem=recv.at[t],
                device_id=(x, y, 1 - c), device_id_type=MESH))
        return cps
    return copies


def _all_reduce_small(name, buf):
    _, r, w = buf.shape

    def body(in_ref, out_ref, land, send, recv):
        x, y, c, _, _ = _place()
        me = 4 * x + 2 * y + c

        def peer(k):
            return (1 - x if k & 4 else x, 1 - y if k & 2 else y, 1 - c if k & 1 else c)

        cps = []
        for k in range(1, N_DEV):
            px, py, pc = peer(k)
            cp = pltpu.make_async_remote_copy(
                src_ref=in_ref.at[4 * px + 2 * py + pc], dst_ref=land.at[me], send_sem=send.at[k - 1],
                recv_sem=recv.at[k - 1], device_id=(px, py, pc), device_id_type=MESH)
            cp.start()
            cps.append(cp)
        land[me] = in_ref[me]
        for cp in cps:
            cp.wait()
        total = land[0]
        for d in range(1, N_DEV):
            total = total + land[d]
        out_ref[me] = total
        cps = []
        for k in range(1, N_DEV):
            cp = pltpu.make_async_remote_copy(
                src_ref=out_ref.at[me], dst_ref=out_ref.at[me], send_sem=send.at[N_DEV - 2 + k],
                recv_sem=recv.at[N_DEV - 2 + k], device_id=peer(k), device_id_type=MESH)
            cp.start()
            cps.append(cp)
        for cp in cps:
            cp.wait()

    vm = pl.BlockSpec(memory_space=pltpu.VMEM)
    return pl.pallas_call(
        body, name=name, in_specs=[vm], out_specs=vm, out_shape=jax.ShapeDtypeStruct(buf.shape, F32),
        scratch_shapes=[pltpu.VMEM(buf.shape, F32), pltpu.SemaphoreType.DMA((2 * N_DEV - 2,)),
                        pltpu.SemaphoreType.DMA((2 * N_DEV - 2,))],
        compiler_params=_params())(buf)


def _add_pairs(name, g, theirs):
    nb, H, C = theirs.shape
    th = _tile(H, max(SUBLANES * 2, (1 << 19) // C), 16)
    nh = H // th

    def body(g_ref, t_ref, p_ref, l_ref):
        s = (g_ref[...].astype(F32) + t_ref[...].astype(F32)).astype(p_ref.dtype)
        p_ref[...] = s

        @pl.when(pl.program_id(1) == _chip())
        def _():
            l_ref[...] = s

    blk = (None, th, C)
    return pl.pallas_call(
        body, name=name, grid=(nh, nb),
        in_specs=[pl.BlockSpec(blk, lambda i, j: (j, lax.axis_index("c") * nh + i, 0)),
                  pl.BlockSpec(blk, lambda i, j: (j, i, 0))],
        out_specs=[pl.BlockSpec(blk, lambda i, j: (j, i, 0)), pl.BlockSpec(blk, lambda i, j: (_chip(), i, 0))],
        out_shape=[jax.ShapeDtypeStruct(theirs.shape, theirs.dtype)] * 2, compiler_params=_params())(g, theirs)


def _sum_chips(name, parts):
    nb, H, C = parts.shape
    th = _tile(H, max(SUBLANES * 2, (1 << 19) // C), 16)
    nh = H // th

    def body(p_ref, o_ref):
        tot = p_ref[0].astype(F32)
        for b in range(1, nb):
            tot = tot + p_ref[b].astype(F32)
        o_ref[...] = tot

    return pl.pallas_call(
        body, name=name, grid=(nh,),
        in_specs=[pl.BlockSpec((nb, th, C), lambda i: (0, i, 0))],
        out_specs=pl.BlockSpec((th, C), lambda i: (lax.axis_index("c") * nh + i, 0)),
        out_shape=jax.ShapeDtypeStruct((2 * H, C), F32), compiler_params=_params())(parts)


def _adamw_math(w, g, m, v):
    m = ADAM_B1 * m + (1.0 - ADAM_B1) * g
    v = ADAM_B2 * v + (1.0 - ADAM_B2) * (g * g)
    m_hat = m / (1.0 - ADAM_B1 ** ADAM_STEP)
    v_hat = v / (1.0 - ADAM_B2 ** ADAM_STEP)
    delta = -ADAM_LR * (m_hat / (jnp.sqrt(v_hat) + ADAM_EPS) + ADAM_WD * w)
    return delta, m, v


def _adamw_big(name, w, m, v, gs):
    depth, R, C = w.shape
    tr = _tile(R, max(SUBLANES, (1 << 18) // C))

    def body(w_ref, m_ref, v_ref, *refs):
        g_refs, (go, do, mo, vo) = refs[:depth], refs[depth:]
        li = pl.program_id(0)
        g = g_refs[0][...]
        for l in range(1, depth):
            g = jnp.where(li == l, g_refs[l][...], g)
        delta, mn, vn = _adamw_math(w_ref[...], g, m_ref[...], v_ref[...])
        go[...] = g
        do[...] = delta
        mo[...] = mn
        vo[...] = vn

    spec = pl.BlockSpec((None, tr, C), lambda li, i: (li, i, 0))
    g_specs = [pl.BlockSpec((tr, C), functools.partial(lambda li, i, l: (jnp.where(li == l, i, 0), 0), l=l))
               for l in range(depth)]
    return pl.pallas_call(body, name=name, grid=(depth, R // tr), in_specs=[spec] * 3 + g_specs,
                          out_specs=[spec] * 4, out_shape=[jax.ShapeDtypeStruct(w.shape, F32)] * 4,
                          compiler_params=_params())(w, m, v, *gs)


def _adamw_small(name, w, g, m, v):
    def fn(i, n, R, B, P, N):
        return list(_adamw_math(*R)), []

    return _rowwise(name, fn, [w, g, m, v], outs=[(LANES, F32)] * 3)


def _pack(arrs, rows_mult):
    flat = jnp.concatenate([a.reshape(-1) for a in arrs])
    n = flat.shape[0]
    per = rows_mult * LANES
    pad = (-n) % per
    flat = jnp.pad(flat, (0, pad))
    return flat.reshape(-1, LANES), n


def _unpack(buf, like):
    flat = buf.reshape(-1)
    out, off = [], 0
    for a in like:
        out.append(flat[off:off + a.size].reshape(a.shape))
        off += a.size
    return out


def _ffn_fwd(tag, alpha, xf, xb, wg, wu, wd, g, b):
    def up(ps, _):
        hg, hu = ps
        return hg, hu, hg * _sigmoid(hg) * hu

    hg, hu, act = _mm_expand(f"{tag}_up", xb, [wg, wu], False, up, [], [F32, F32, BF16])

    def down(acc, ex, bc):
        y, xhat, rstd = _layer_norm(alpha * ex[0] + 0.5 * acc, bc[0], bc[1])
        return y, y, xhat, jnp.broadcast_to(rstd, (rstd.shape[0], LANES))

    D = xf.shape[1]
    yf, yb, xhat, rstd = _mm_contract(f"{tag}_down", [act], [wd], False, down, [xf], [g, b],
                                      [(D, F32), (D, BF16), (D, F32), (LANES, F32)])
    return yf, yb, (xb, hg, hu, act, xhat, rstd)


def _ln_bwd(name, dy, xhat, rstd, g, scale):
    D = dy.shape[1]

    def fn(i, n, R, B, P, N):
        d, xh, rs = R
        dr = _layer_norm_bwd(d, xh, rs[:, :1], B[0])
        return [dr, scale * dr], [jnp.sum(d * xh, axis=0, keepdims=True), jnp.sum(d, axis=0, keepdims=True)]

    return _rowwise(name, fn, [dy, xhat, rstd], bcast=[g], outs=[(D, F32), (D, BF16)], sums=[D, D])


def _ffn_bwd(tag, alpha, dy, saved, wg, wu, wd, g, on_grads=None):
    xb, hg, hu, act, xhat, rstd = saved
    dr, dfb, dg, db = _ln_bwd(f"{tag}_ln_bwd", dy, xhat, rstd, g, 0.5)

    def dact(ps, ex):
        da, hgv, huv = ps[0], ex[0], ex[1]
        sg = _sigmoid(hgv)
        return da * huv * (sg * (1.0 + hgv * (1.0 - sg))), da * (hgv * sg)

    dhg, dhu = _mm_expand(f"{tag}_dact", dfb, [wd], True, dact, [hg, hu], [BF16, BF16])
    g_wd = _wgrad_rows(f"{tag}_gwd", act, dfb, tr=1408)
    g_wg = _wgrad_cols(f"{tag}_gwg", xb, dhg)
    g_wu = _wgrad_cols(f"{tag}_gwu", xb, dhu)

    def dxin(acc, ex, bc):
        return [alpha * ex[0] + acc]

    D = dy.shape[1]
    hold = [] if on_grads is None else [on_grads((g_wg, g_wu, g_wd))]
    dx, = _mm_contract(f"{tag}_dx", [dhg, dhu], [wg, wu], True, dxin, [dr], hold, [(D, F32)])
    return dx, (g_wg, g_wu, g_wd), dg, db


def _conv_taps(v, pv, i, w):
    tm = v.shape[0]
    ext = jnp.concatenate([pv * (i > 0).astype(F32), v], axis=0)
    v1 = pltpu.roll(ext, 1, 0)[SUBLANES:SUBLANES + tm]
    v2 = pltpu.roll(ext, 2, 0)[SUBLANES:SUBLANES + tm]
    return w[0:1] * v2 + w[1:2] * v1 + w[2:3] * v, v1, v2


def _mixer_fwd(tag, alpha, xf, xb, p, w_in, w_glu, w_out):
    Dh = w_in.shape[2]
    proj, = _mm_expand(f"{tag}_proj", xb, [w_in], False, lambda ps, _: ps, [], [F32])
    u, gb, gc, h = [(proj, Dh, j) for j in range(4)]
    nblk = p["wb_re"].shape[0]
    S = nblk * p["wb_re"].shape[2]
    both = lambda ps, ex, bx: ps
    bu_re, bu_im = _bd_mm(f"{tag}_bu", [(proj, p["wb_re"]), (proj, p["wb_im"])], False, both, [], [], [F32, F32])
    s_re, s_im = _scan(f"{tag}_scan", p["a_re"], p["a_im"], bu_re, bu_im)

    def yout(ps, ex, bx):
        ys = ps[0] + ps[1] + bx[0] * ex[0]
        return ys, _gelu(ys), _gelu(ys)

    ys, yg, ygb = _bd_mm(f"{tag}_yout", [(s_re, p["wc_re"]), (s_im, p["wc_imn"])], False, yout, [proj], [p["d"]],
                         [F32, F32, BF16])

    def glu(acc, ex, bc):
        yy = ex[0] * _sigmoid(acc)
        return acc, yy * _rms_inv(yy) * bc[0]

    t, yn = _mm_contract(f"{tag}_glu", [ygb], [w_glu], False, glu, [yg], [p["g_s5"]], [(Dh, F32), (Dh, BF16)])

    def conv(i, n, R, B, P, N):
        gbv, gcv, hv = R
        cw, cb, gcn = B
        cv, _, _ = _conv_taps(gcv * hv, P[0] * P[1], i, cw)
        z = gbv * (cv + cb)
        return [z * _rms_inv(z) * gcn], []

    zn, = _rowwise(f"{tag}_conv", conv, [gb, gc, h], bcast=[p["conv_w"], p["conv_b"], p["g_conv"]],
                   outs=[(Dh, BF16)], prev=[1, 2])
    cat = jnp.concatenate([yn, zn], axis=1)

    def out(acc, ex, bc):
        y, xhat, rstd = _layer_norm(alpha * ex[0] + acc, bc[0], bc[1])
        return y, y, xhat, jnp.broadcast_to(rstd, (rstd.shape[0], LANES))

    D = xf.shape[1]
    yf, yb, xhat, rstd = _mm_contract(f"{tag}_out", [cat], [w_out], False, out, [xf], [p["ln2_g"], p["ln2_b"]],
                                      [(D, F32), (D, BF16), (D, F32), (LANES, F32)])
    return yf, yb, (xb, proj, s_re, s_im, ys, yg, ygb, t, cat, xhat, rstd)


def _mixer_bwd(tag, alpha, dy, saved, p, w_in, w_glu, w_out):
    xb, proj, s_re, s_im, ys, yg, ygb, t, cat, xhat, rstd = saved
    Dh = w_in.shape[2]
    D = dy.shape[1]
    gb, gc, h = [(proj, Dh, j) for j in range(1, 4)]
    dr, dmb, dg2, db2 = _ln_bwd(f"{tag}_ln_bwd", dy, xhat, rstd, p["ln2_g"], 1.0)
    dcat, = _mm_expand(f"{tag}_dcat", dmb, [w_out], True, lambda ps, _: ps, [], [F32])
    g_wout = _wgrad_rows(f"{tag}_gwout", cat, dmb)
    half = dcat.shape[1] // 2

    def conv_b1(i, n, R, B, P, N):
        dzn, gbv, gcv, hv = R
        cw, cb, gcn = B
        v = gcv * hv
        cv, v1, v2 = _conv_taps(v, P[0] * P[1], i, cw)
        cv = cv + cb
        z = gbv * cv
        rinv = _rms_inv(z)
        dz = _rms_bwd(dzn, z, rinv, gcn)
        dcv = dz * gbv
        col = lambda a: jnp.sum(a, axis=0, keepdims=True)
        return [dz * cv, dcv], [col(dzn * z * rinv), col(dcv), col(dcv * v2), col(dcv * v1), col(dcv * v)]

    dgb, dcv, dg_conv, dconv_b, dw0, dw1, dw2 = _rowwise(
        f"{tag}_conv_b1", conv_b1, [(dcat, half, 1), gb, gc, h], bcast=[p["conv_w"], p["conv_b"], p["g_conv"]],
        outs=[(Dh, BF16), (Dh, F32)], sums=[Dh] * 5, prev=[2, 3])

    def conv_b2(i, n, R, B, P, N):
        d, gcv, hv = R
        cw = B[0]
        tm = d.shape[0]
        ext = jnp.concatenate([d, N[0] * (i < n - 1).astype(F32)], axis=0)
        d1 = pltpu.roll(ext, tm + SUBLANES - 1, 0)[:tm]
        d2 = pltpu.roll(ext, tm + SUBLANES - 2, 0)[:tm]
        dv = cw[2:3] * d + cw[1:2] * d1 + cw[0:1] * d2
        return [dv * hv, dv * gcv], []

    dgc, dh = _rowwise(f"{tag}_conv_b2", conv_b2, [dcv, gc, h], bcast=[p["conv_w"]], outs=[(Dh, BF16)] * 2, nxt=[0])

    def glu_b(i, n, R, B, P, N):
        dyn, ygv, tv = R
        sg = _sigmoid(tv)
        yy = ygv * sg
        rinv = _rms_inv(yy)
        dyy = _rms_bwd(dyn, yy, rinv, B[0])
        return [dyy * ygv * sg * (1.0 - sg), dyy * sg], [jnp.sum(dyn * yy * rinv, axis=0, keepdims=True)]

    dtb, dyg0, dg_s5 = _rowwise(f"{tag}_glu_b", glu_b, [(dcat, half, 0), yg, t], bcast=[p["g_s5"]],
                                outs=[(Dh, BF16), (Dh, F32)], sums=[Dh])

    def dys_epi(ps, ex):
        return [(ps[0] + ex[0]) * _gelu_grad(ex[1])]

    dys, = _mm_expand(f"{tag}_dys", dtb, [w_glu], True, dys_epi, [dyg0, ys], [F32])
    g_wglu = _wgrad_rows(f"{tag}_gwglu", ygb, dtb)
    both = lambda ps, ex, bx: ps
    ds_re, ds_im = _bd_mm(f"{tag}_ds", [(dys, p["wc_re"]), (dys, p["wc_imn"])], True, both, [], [], [F32, F32])
    l_re, l_im, da_re, da_im = _scan(f"{tag}_rscan", p["a_re"], -p["a_im"], ds_re, ds_im, reverse=True,
                                     state=(s_re, s_im))

    def du_epi(ps, ex, bx):
        return [ps[0] + ps[1] + bx[0] * ex[0]]

    du, = _bd_mm(f"{tag}_du", [(l_re, p["wb_re"]), (l_im, p["wb_im"])], True, du_epi, [dys], [p["d"]], [BF16])
    nblk = p["wb_re"].shape[0]
    u = (proj, Dh, 0)
    u_arr = proj[:, :Dh]
    g_wb_re = _bd_wgrad(f"{tag}_gwb_re", u_arr, l_re, nblk)
    g_wb_im = _bd_wgrad(f"{tag}_gwb_im", u_arr, l_im, nblk)
    g_wc_re = _bd_wgrad(f"{tag}_gwc_re", s_re, dys, nblk)
    g_wc_imn = _bd_wgrad(f"{tag}_gwc_im", s_im, dys, nblk)

    def dd_fn(i, n, R, B, P, N):
        return [], [jnp.sum(R[0] * R[1], axis=0, keepdims=True)]

    dd, = _rowwise(f"{tag}_dd", dd_fn, [dys, u], sums=[Dh])

    dproj = jnp.concatenate([du, dgb, dgc, dh], axis=1)

    def dxin(acc, ex, bc):
        return [alpha * ex[0] + acc]

    dx, = _mm_contract(f"{tag}_dx", [dproj], [w_in], True, dxin, [dr], [], [(D, F32)])
    g_win = _wgrad_cols(f"{tag}_gwin", xb, dproj)
    small = dict(ln2_g=dg2, ln2_b=db2, g_conv=dg_conv, conv_b=dconv_b,
                 conv_w=jnp.concatenate([dw0, dw1, dw2], axis=0), g_s5=dg_s5, d=dd,
                 da_re=da_re, da_im=da_im, wb_re=g_wb_re, wb_im=g_wb_im, wc_re=g_wc_re, wc_imn=g_wc_imn)
    return dx, (g_win, g_wglu, g_wout), small


def _swap_start(name, grads, after):
    n = len(grads)
    lands = [lax.empty((g.shape[0], g.shape[1] // 2, g.shape[2]), g.dtype) for g in grads]
    send, recv, arrays, token = _split_start(name, _swap_copies(n), n, list(grads) + lands, after)
    return (send, recv, arrays, n), token


def _swap_finish(name, tag, state, after):
    send, recv, arrays, n = state
    arrays = _split_wait(name, _swap_copies(n), send, recv, arrays, after)
    added = [_add_pairs(f"{tag}{t}", g, b) for t, (g, b) in enumerate(zip(arrays[:n], arrays[n:]))]
    return [a[0] for a in added], [a[1] for a in added]


BIG = ["ffn1_gate", "ffn1_up", "ffn1_down", "w_in", "s5_w_glu", "w_out", "ffn2_gate", "ffn2_up", "ffn2_down"]
SMALL = ["ln1_g", "ln1_b", "s5_lam_re", "s5_lam_im", "s5_log_dt", "s5_b_re", "s5_b_im", "s5_c_re", "s5_c_im", "s5_d",
         "conv_w", "conv_b", "g_s5", "g_conv", "ln2_g", "ln2_b", "ln3_g", "ln3_b"]
WEIGHTS = ['ffn1_gate', 'ffn1_up', 'ffn1_down', 'ln1_g', 'ln1_b', 'w_in', 's5_lam_re', 's5_lam_im', 's5_log_dt',
           's5_b_re', 's5_b_im', 's5_c_re', 's5_c_im', 's5_d', 's5_w_glu', 'conv_w', 'conv_b', 'g_s5', 'g_conv',
           'w_out', 'ln2_g', 'ln2_b', 'ffn2_gate', 'ffn2_up', 'ffn2_down', 'ln3_g', 'ln3_b']


def _step(W, M, V, x, target):
    depth = W["ffn1_gate"].shape[0]
    alpha = (2.0 * depth) ** 0.25
    L, D = x.shape
    G, N = W["s5_lam_re"].shape[1:]
    P = W["s5_b_re"].shape[3]
    Dh = G * P
    chip = 2 * lax.axis_index("x") + lax.axis_index("y")
    cw_cols = W["conv_w"].shape[2]

    cw_rows = -(-depth * CONV_W // SUBLANES) * SUBLANES
    cw_buf = jnp.zeros((N_CHIPS, 2 * cw_rows, cw_cols), F32)
    cw_buf = lax.dynamic_update_slice(cw_buf, W["conv_w"].reshape(1, depth * CONV_W, cw_cols), (chip, 0, 0))

    T = len(BIG)
    gather, gather0 = _gather_copies(T), _gather_copies(T + 1)
    placed = [[_cast_place(f"place{l}_{n}", W[n], l) for n in BIG] for l in range(depth)]
    g_state = _split_start("gather0_start", gather0, 3 * (T + 1), placed[0] + [cw_buf], [x])

    s5_vjps, lp = [], []
    for l in range(depth):
        (a_re, a_im, bb_re, bb_im), vjp = jax.vjp(_s5_discretize, W["s5_lam_re"][l], W["s5_lam_im"][l],
                                                  W["s5_log_dt"][l], W["s5_b_re"][l], W["s5_b_im"][l])
        s5_vjps.append(vjp)
        row = lambda a: a.reshape(1, -1)
        lp.append(dict(
            a_re=row(a_re), a_im=row(a_im), wb_re=_bd_build(bb_re), wb_im=_bd_build(bb_im),
            wc_re=_bd_build(W["s5_c_re"][l].transpose(0, 2, 1)).transpose(0, 2, 1),
            wc_imn=-_bd_build(W["s5_c_im"][l].transpose(0, 2, 1)).transpose(0, 2, 1),
            d=row(W["s5_d"][l]), g_s5=row(W["g_s5"][l]), g_conv=row(W["g_conv"][l]), conv_b=row(W["conv_b"][l]),
            ln2_g=row(W["ln2_g"][l]), ln2_b=row(W["ln2_b"][l])))

    def cast(i, n, R, B, P_, N_):
        return [R[0]], []

    xb, = _rowwise("cast_x", cast, [x], outs=[(D, BF16)])
    xf = x
    saved, gathered = [], []
    early = [xb] + [a for p in placed[1:] for a in p] + [q[k] for q in lp for k in ("wb_re", "wb_im", "wc_re", "wc_imn")]
    for l in range(depth):
        bufs = _split_wait(f"gather{l}_wait", gather0 if l == 0 else gather, g_state[0], g_state[1], g_state[2],
                           early if l == 0 else [xb])
        token = bufs[0]
        if l + 1 < depth:
            g_state = _split_start(f"gather{l + 1}_start", gather, 3 * T, placed[l + 1], [bufs[0]])
            token = g_state[3]
        passed = _gather_pass_on(f"gather{l}_pass", bufs, token)
        if l == 0:
            cw = passed[T][:, :depth * CONV_W, :].reshape(N_CHIPS, depth, CONV_W, cw_cols)
            cw = cw.transpose(1, 2, 0, 3).reshape(depth, CONV_W, N_CHIPS * cw_cols)
            for q in range(depth):
                lp[q]["conv_w"] = cw[q]
        gathered.append(passed[:T])
        wg1, wu1, wd1, w_in, w_glu, w_out, wg2, wu2, wd2 = gathered[l]
        row = lambda a: a.reshape(1, -1)
        xf, xb, s1 = _ffn_fwd(f"l{l}_ffn1", alpha, xf, xb, wg1, wu1, wd1, row(W["ln1_g"][l]), row(W["ln1_b"][l]))
        xf, xb, s2 = _mixer_fwd(f"l{l}_mix", alpha, xf, xb, lp[l], w_in, w_glu, w_out)
        xf, xb, s3 = _ffn_fwd(f"l{l}_ffn2", alpha, xf, xb, wg2, wu2, wd2, row(W["ln3_g"][l]), row(W["ln3_b"][l]))
        saved.append((s1, s2, s3))

    def loss_fn(i, n, R, B, P_, N_):
        err = R[0] - R[1]
        return [err * (1.0 / D)], [jnp.sum(0.5 * err * err * (1.0 / D), axis=0, keepdims=True)]

    dx, loss_cols = _rowwise("loss", loss_fn, [xf, target], outs=[(D, F32)], sums=[D])
    loss = lax.psum(jnp.sum(loss_cols), ("x", "y", "c"))

    big_g = {n: [None] * depth for n in BIG}
    small_g = {n: [None] * depth for n in SMALL}
    xchg, join = _xchg_copies(T), _join_copies(T)
    x_state, tok, joins = None, 0.0, []

    def finish(layer, lands, after):
        fulls = [_sum_chips(f"l{layer}_red_sum{t}", g) for t, g in enumerate(lands)]
        joins.append((layer, _split_start(f"l{layer}_join_start", join, T, fulls, after)))

    for l in reversed(range(depth)):
        wg1, wu1, wd1, w_in, w_glu, w_out, wg2, wu2, wd2 = gathered[l]
        s1, s2, s3 = saved[l]
        row = lambda a: a.reshape(1, -1)
        dx, g2, dg3, db3 = _ffn_bwd(f"l{l}_ffn2", alpha, dx, s3, wg2, wu2, wd2, row(W["ln3_g"][l]) + tok)
        sw2, tok2 = _swap_start(f"l{l}_swap2_start", g2, [dx])
        dx, gm, sm = _mixer_bwd(f"l{l}_mix", alpha, dx, s2, dict(lp[l], ln2_g=lp[l]["ln2_g"] + tok2[0, 0]),
                                w_in, w_glu, w_out)
        swm, tokm = _swap_start(f"l{l}_swapm_start", gm, [dx])
        box = []

        def swap1(grads, l=l, box=box):
            state, token = _swap_start(f"l{l}_swap1_start", grads, [])
            box.append(state)
            return token

        dx, g1, dg1, db1 = _ffn_bwd(f"l{l}_ffn1", alpha, dx, s1, wg1, wu1, wd1, row(W["ln1_g"][l]) + tokm[0, 0], swap1)
        sw1 = box[0]
        p2, l2 = _swap_finish(f"l{l}_swap2_wait", f"l{l}_add2_", sw2, [dx])
        pm, lm = _swap_finish(f"l{l}_swapm_wait", f"l{l}_addm_", swm, [dx])
        p1, l1 = _swap_finish(f"l{l}_swap1_wait", f"l{l}_add1_", sw1, [dx])
        arrays = p1 + pm + p2 + l1 + lm + l2
        after, prev = [arrays[0]], None
        if x_state is not None:
            prev = _split_wait(f"l{l + 1}_xchg_wait", xchg, x_state[0], x_state[1], x_state[2], [arrays[0]])
            after = [prev[T]]
        x_state = _split_start(f"l{l}_xchg_start", xchg, 3 * T, arrays, after)
        tok = x_state[3][0, 0]
        if prev is not None:
            finish(l + 1, prev[T:], [x_state[3]])
            tok = tok + joins[-1][1][3][0, 0]
        d_bb_re = _bd_extract(sm["wb_re"], G, N, P)
        d_bb_im = _bd_extract(sm["wb_im"], G, N, P)
        d_lre, d_lim, d_ldt, d_bre, d_bim = s5_vjps[l]((sm["da_re"].reshape(G, N), sm["da_im"].reshape(G, N),
                                                        d_bb_re, d_bb_im))
        d_cre = _bd_extract(sm["wc_re"].transpose(0, 2, 1), G, N, P).transpose(0, 2, 1)
        d_cim = -_bd_extract(sm["wc_imn"].transpose(0, 2, 1), G, N, P).transpose(0, 2, 1)
        vals = dict(ln1_g=dg1, ln1_b=db1, s5_lam_re=d_lre, s5_lam_im=d_lim, s5_log_dt=d_ldt, s5_b_re=d_bre,
                    s5_b_im=d_bim, s5_c_re=d_cre, s5_c_im=d_cim, s5_d=sm["d"], conv_w=sm["conv_w"],
                    conv_b=sm["conv_b"], g_s5=sm["g_s5"], g_conv=sm["g_conv"], ln2_g=sm["ln2_g"], ln2_b=sm["ln2_b"],
                    ln3_g=dg3, ln3_b=db3)
        for n in SMALL:
            small_g[n][l] = vals[n].reshape((W[n].shape[1:] if n != "conv_w" else (CONV_W, N_CHIPS * cw_cols)))
    grad_x = dx

    sg = [jnp.stack(small_g[n]) for n in SMALL]
    packed, _ = _pack(sg, SUBLANES * N_DEV)
    packed = _all_reduce_small("small_allreduce", packed.reshape(N_DEV, -1, LANES)).reshape(-1, LANES)
    last = _split_wait("l0_xchg_wait", xchg, x_state[0], x_state[1], x_state[2], [packed])
    finish(0, last[T:], [last[T]])
    for layer, (send, recv, fulls, _) in joins:
        fulls = _split_wait(f"l{layer}_join_wait", join, send, recv, fulls, [packed])
        for n, r in zip(BIG, fulls):
            big_g[n][layer] = r
    sg = dict(zip(SMALL, _unpack(packed, sg)))
    sg["conv_w"] = lax.dynamic_slice_in_dim(sg["conv_w"], chip * cw_cols, cw_cols, axis=2)
    gp, _ = _pack([sg[n] for n in SMALL], SUBLANES)
    wp, _ = _pack([W[n] for n in SMALL], SUBLANES)
    mp, _ = _pack([M[n] for n in SMALL], SUBLANES)
    vp, _ = _pack([V[n] for n in SMALL], SUBLANES)
    like = [W[n] for n in SMALL]
    dsm, msm, vsm = [dict(zip(SMALL, _unpack(a, like))) for a in _adamw_small("adamw_small", wp, gp, mp, vp)]

    grads, deltas, new_m, new_v = dict(sg), dsm, msm, vsm
    for n in BIG:
        shp = W[n].shape
        flat = lambda a: a.reshape(shp[0], shp[1], shp[2])
        gr, de, mn, vn = _adamw_big(f"adamw_{n}", flat(W[n]), flat(M[n]), flat(V[n]), big_g[n])
        grads[n], deltas[n], new_m[n], new_v[n] = gr, de, mn, vn

    outs = [loss, grad_x[None]]
    for d in (grads, deltas, new_m, new_v):
        outs += [d[n] for n in WEIGHTS]
    return tuple(outs)


def kernel(x, ffn1_gate, ffn1_up, ffn1_down, ln1_g, ln1_b, w_in, s5_lam_re, s5_lam_im, s5_log_dt, s5_b_re, s5_b_im, s5_c_re, s5_c_im, s5_d, s5_w_glu, conv_w, conv_b, g_s5, g_conv, w_out, ln2_g, ln2_b, ffn2_gate, ffn2_up, ffn2_down, ln3_g, ln3_b, loss_target, m_ffn1_gate, m_ffn1_up, m_ffn1_down, m_ln1_g, m_ln1_b, m_w_in, m_s5_lam_re, m_s5_lam_im, m_s5_log_dt, m_s5_b_re, m_s5_b_im, m_s5_c_re, m_s5_c_im, m_s5_d, m_s5_w_glu, m_conv_w, m_conv_b, m_g_s5, m_g_conv, m_w_out, m_ln2_g, m_ln2_b, m_ffn2_gate, m_ffn2_up, m_ffn2_down, m_ln3_g, m_ln3_b, v_ffn1_gate, v_ffn1_up, v_ffn1_down, v_ln1_g, v_ln1_b, v_w_in, v_s5_lam_re, v_s5_lam_im, v_s5_log_dt, v_s5_b_re, v_s5_b_im, v_s5_c_re, v_s5_c_im, v_s5_d, v_s5_w_glu, v_conv_w, v_conv_b, v_g_s5, v_g_conv, v_w_out, v_ln2_g, v_ln2_b, v_ffn2_gate, v_ffn2_up, v_ffn2_down, v_ln3_g, v_ln3_b):
    a = dict(locals())
    W = {n: a[n] for n in WEIGHTS}
    M = {n: a["m_" + n] for n in WEIGHTS}
    V = {n: a["v_" + n] for n in WEIGHTS}
    return _step(W, M, V, x[0], loss_target[0])
```

```python
import functools

import jax
import jax.numpy as jnp
from jax import lax
from jax.experimental import pallas as pl
from jax.experimental.pallas import tpu as pltpu

F32 = jnp.float32
BF16 = jnp.bfloat16
MESH = pl.DeviceIdType.MESH
HIGH = lax.Precision.HIGHEST

N_CHIPS = 4
N_DEV = 8
V7X_VMEM_LIMIT = 56 * 1024 * 1024
SUBLANES = 8
LANES = 128
S5_P = 16
S5_N = 64
S5_GB = 8
CONV_W = 3
LN_EPS = 1e-5
RMS_EPS = 1e-6
ADAM_LR = 0.001
ADAM_B1 = 0.9
ADAM_B2 = 0.999
ADAM_EPS = 1e-08
ADAM_WD = 0.01
ADAM_STEP = 10
GELU_K = 0.7978845608028654
GELU_C = 0.044715


def _params():
    return pltpu.CompilerParams(vmem_limit_bytes=V7X_VMEM_LIMIT)


def _tile(n, pref, mult=SUBLANES):
    best = None
    for t in range(mult, min(n, pref) + 1, mult):
        if n % t == 0:
            best = t
    return best if best is not None else n


def _mm(a, b, precision=None):
    return jnp.dot(a, b, preferred_element_type=F32, precision=precision)


def _mm_nt(a, b, precision=None):
    return lax.dot_general(a, b, (((1,), (1,)), ((), ())), preferred_element_type=F32, precision=precision)


def _mm_tn(a, b, precision=None):
    return lax.dot_general(a, b, (((0,), (0,)), ((), ())), preferred_element_type=F32, precision=precision)


def _sigmoid(x):
    return 1.0 / (1.0 + jnp.exp(-x))


def _gelu(x):
    return 0.5 * x * (1.0 + jnp.tanh(GELU_K * (x + GELU_C * x * x * x)))


def _gelu_grad(x):
    th = jnp.tanh(GELU_K * (x + GELU_C * x * x * x))
    return 0.5 * (1.0 + th) + 0.5 * x * (1.0 - th * th) * GELU_K * (1.0 + 3.0 * GELU_C * x * x)


def _layer_norm(r, g, b):
    mu = jnp.mean(r, axis=-1, keepdims=True)
    xc = r - mu
    rstd = lax.rsqrt(jnp.mean(xc * xc, axis=-1, keepdims=True) + LN_EPS)
    xhat = xc * rstd
    return xhat * g + b, xhat, rstd


def _layer_norm_bwd(dy, xhat, rstd, g):
    dxh = dy * g
    m1 = jnp.mean(dxh, axis=-1, keepdims=True)
    m2 = jnp.mean(dxh * xhat, axis=-1, keepdims=True)
    return rstd * (dxh - m1 - xhat * m2)


def _rms_inv(x):
    return lax.rsqrt(jnp.mean(x * x, axis=-1, keepdims=True) + RMS_EPS)


def _rms_bwd(dy, x, rinv, g):
    dxh = dy * g
    return rinv * dxh - x * (rinv * rinv * rinv) * jnp.mean(dxh * x, axis=-1, keepdims=True)


def _rowwise(name, fn, rows, bcast=(), outs=(), sums=(), prev=(), nxt=(), tm=256):
    rows = [r if isinstance(r, tuple) else (r, r.shape[1], 0) for r in rows]
    L = rows[0][0].shape[0]
    tm = _tile(L, tm)
    n = L // tm
    hb = tm // SUBLANES
    nh = L // SUBLANES
    nr, nb, npv, nnx, no, ns = len(rows), len(bcast), len(prev), len(nxt), len(outs), len(sums)

    def body(*refs):
        i = pl.program_id(0)
        k = 0
        R = [r[...] for r in refs[k:k + nr]]; k += nr
        B = [r[...] for r in refs[k:k + nb]]; k += nb
        P = [r[...] for r in refs[k:k + npv]]; k += npv
        N = [r[...] for r in refs[k:k + nnx]]; k += nnx
        o_refs = refs[k:k + no]; k += no
        s_refs = refs[k:k + ns]
        O, S = fn(i, n, R, B, P, N)
        for ref, val in zip(o_refs, O):
            ref[...] = val.astype(ref.dtype)
        if ns:
            @pl.when(i == 0)
            def _():
                for ref in s_refs:
                    ref[...] = jnp.zeros_like(ref)
            for ref, val in zip(s_refs, S):
                ref[...] += val

    in_specs = [pl.BlockSpec((tm, w), functools.partial(lambda i, cb: (i, cb), cb=cb)) for _, w, cb in rows]
    in_specs += [pl.BlockSpec(b.shape, lambda i: (0, 0)) for b in bcast]
    in_specs += [pl.BlockSpec((SUBLANES, rows[j][1]),
                              functools.partial(lambda i, cb: (jnp.maximum(i * hb - 1, 0), cb), cb=rows[j][2]))
                 for j in prev]
    in_specs += [pl.BlockSpec((SUBLANES, rows[j][1]),
                              functools.partial(lambda i, cb: (jnp.minimum((i + 1) * hb, nh - 1), cb), cb=rows[j][2]))
                 for j in nxt]
    out_specs = [pl.BlockSpec((tm, w), lambda i: (i, 0)) for w, _ in outs]
    out_specs += [pl.BlockSpec((1, w), lambda i: (0, 0)) for w in sums]
    out_shape = [jax.ShapeDtypeStruct((L, w), dt) for w, dt in outs]
    out_shape += [jax.ShapeDtypeStruct((1, w), F32) for w in sums]
    args = [r[0] for r in rows] + list(bcast) + [rows[j][0] for j in prev] + [rows[j][0] for j in nxt]
    return pl.pallas_call(body, name=name, grid=(n,), in_specs=in_specs, out_specs=out_specs,
                          out_shape=out_shape, compiler_params=_params())(*args)


def _mm_expand(name, a, ws, nt, epi, extras, outs, tm=512):
    L, K = a.shape
    tm = _tile(L, tm)
    nw, ne = len(ws), len(extras)
    co = ws[0].shape[1] if nt else ws[0].shape[2]

    def body(a_ref, *refs):
        av = a_ref[...]
        ps = [(_mm_nt if nt else _mm)(av, w[...]) for w in refs[:nw]]
        vals = epi(ps, [e[...] for e in refs[nw:nw + ne]])
        for ref, val in zip(refs[nw + ne:], vals):
            ref[...] = val.astype(ref.dtype)

    in_specs = [pl.BlockSpec((tm, K), lambda j, i: (i, 0))]
    in_specs += [pl.BlockSpec((None,) + w.shape[1:], lambda j, i: (j, 0, 0)) for w in ws]
    in_specs += [pl.BlockSpec((tm, co), lambda j, i: (i, j)) for _ in extras]
    out_specs = [pl.BlockSpec((tm, co), lambda j, i: (i, j)) for _ in outs]
    out_shape = [jax.ShapeDtypeStruct((L, N_CHIPS * co), dt) for dt in outs]
    return pl.pallas_call(body, name=name, grid=(N_CHIPS, L // tm), in_specs=in_specs, out_specs=out_specs,
                          out_shape=out_shape, compiler_params=_params())(a, *ws, *extras)


def _mm_contract(name, as_, ws, nt, epi, extras, bcast, outs, tm=512):
    L = as_[0].shape[0]
    tm = _tile(L, tm)
    na, ne, nb = len(as_), len(extras), len(bcast)
    cb = as_[0].shape[1] // N_CHIPS
    n_out = ws[0].shape[1] if nt else ws[0].shape[2]

    def body(*refs):
        k = pl.program_id(1)
        a_refs, w_refs = refs[:na], refs[na:2 * na]
        e_refs = refs[2 * na:2 * na + ne]
        b_refs = refs[2 * na + ne:2 * na + ne + nb]
        o_refs = refs[2 * na + ne + nb:-1]
        acc = refs[-1]

        @pl.when(k == 0)
        def _():
            acc[...] = jnp.zeros_like(acc)

        part = None
        for a_ref, w_ref in zip(a_refs, w_refs):
            p = (_mm_nt if nt else _mm)(a_ref[...], w_ref[...])
            part = p if part is None else part + p
        acc[...] += part

        @pl.when(k == N_CHIPS - 1)
        def _():
            vals = epi(acc[...], [e[...] for e in e_refs], [b[...] for b in b_refs])
            for ref, val in zip(o_refs, vals):
                ref[...] = val.astype(ref.dtype)

    in_specs = [pl.BlockSpec((tm, cb), lambda i, k: (i, k)) for _ in as_]
    in_specs += [pl.BlockSpec((None,) + w.shape[1:], lambda i, k: (k, 0, 0)) for w in ws]
    in_specs += [pl.BlockSpec((tm, e.shape[1]), lambda i, k: (i, 0)) for e in extras]
    in_specs += [pl.BlockSpec(b.shape, lambda i, k: (0, 0)) for b in bcast]
    out_specs = [pl.BlockSpec((tm, w), lambda i, k: (i, 0)) for w, _ in outs]
    out_shape = [jax.ShapeDtypeStruct((L, w), dt) for w, dt in outs]
    return pl.pallas_call(body, name=name, grid=(L // tm, N_CHIPS), in_specs=in_specs, out_specs=out_specs,
                          out_shape=out_shape, scratch_shapes=[pltpu.VMEM((tm, n_out), F32)],
                          compiler_params=_params())(*as_, *ws, *extras, *bcast)


def _wgrad_cols(name, a, b, tk=1024):
    L, K = a.shape
    C = b.shape[1] // N_CHIPS
    tk = _tile(K, tk, LANES)

    def body(a_ref, b_ref, o_ref):
        o_ref[...] = _mm_tn(a_ref[...], b_ref[...]).astype(o_ref.dtype)

    return pl.pallas_call(
        body, name=name, grid=(N_CHIPS, K // tk),
        in_specs=[pl.BlockSpec((L, tk), lambda j, kb: (0, kb)), pl.BlockSpec((L, C), lambda j, kb: (0, j))],
        out_specs=pl.BlockSpec((None, tk, C), lambda j, kb: (j, kb, 0)),
        out_shape=jax.ShapeDtypeStruct((N_CHIPS, K, C), BF16), compiler_params=_params())(a, b)


def _wgrad_rows(name, a, b, tr=512):
    L, N = b.shape
    R = a.shape[1] // N_CHIPS
    tr = _tile(R, tr, LANES)
    nrb = R // tr

    def body(a_ref, b_ref, o_ref):
        o_ref[...] = _mm_tn(a_ref[...], b_ref[...]).astype(o_ref.dtype)

    return pl.pallas_call(
        body, name=name, grid=(N_CHIPS, nrb),
        in_specs=[pl.BlockSpec((L, tr), lambda j, rb: (0, j * nrb + rb)), pl.BlockSpec((L, N), lambda j, rb: (0, 0))],
        out_specs=pl.BlockSpec((None, tr, N), lambda j, rb: (j, rb, 0)),
        out_shape=jax.ShapeDtypeStruct((N_CHIPS, R, N), BF16), compiler_params=_params())(a, b)


def _bd_mm(name, pairs, nt, epi, extras, bex, outs, tm=2048):
    L = pairs[0][0].shape[0]
    nblk = pairs[0][1].shape[0]
    tm = _tile(L, tm)
    npair, ne, nx = len(pairs), len(extras), len(bex)
    w0 = pairs[0][1]
    ca, co = (w0.shape[2], w0.shape[1]) if nt else (w0.shape[1], w0.shape[2])

    def body(*refs):
        ps = [(_mm_nt if nt else _mm)(refs[2 * q][...], refs[2 * q + 1][...], HIGH) for q in range(npair)]
        k = 2 * npair
        vals = epi(ps, [e[...] for e in refs[k:k + ne]], [e[...] for e in refs[k + ne:k + ne + nx]])
        for ref, val in zip(refs[k + ne + nx:], vals):
            ref[...] = val.astype(ref.dtype)

    in_specs = []
    args = []
    for a, w in pairs:
        in_specs += [pl.BlockSpec((tm, ca), lambda i, g: (i, g)), pl.BlockSpec((None,) + w.shape[1:], lambda i, g: (g, 0, 0))]
        args += [a, w]
    in_specs += [pl.BlockSpec((tm, co), lambda i, g: (i, g)) for _ in extras]
    in_specs += [pl.BlockSpec((1, co), lambda i, g: (0, g)) for _ in bex]
    out_specs = [pl.BlockSpec((tm, co), lambda i, g: (i, g)) for _ in outs]
    out_shape = [jax.ShapeDtypeStruct((L, nblk * co), dt) for dt in outs]
    return pl.pallas_call(body, name=name, grid=(L // tm, nblk), in_specs=in_specs, out_specs=out_specs,
                          out_shape=out_shape, compiler_params=_params())(*args, *extras, *bex)


def _bd_wgrad(name, a, b, nblk):
    L = a.shape[0]
    ca, cb = a.shape[1] // nblk, b.shape[1] // nblk

    def body(a_ref, b_ref, o_ref):
        o_ref[...] = _mm_tn(a_ref[...], b_ref[...], HIGH)

    return pl.pallas_call(
        body, name=name, grid=(nblk,),
        in_specs=[pl.BlockSpec((L, ca), lambda g: (0, g)), pl.BlockSpec((L, cb), lambda g: (0, g))],
        out_specs=pl.BlockSpec((None, ca, cb), lambda g: (g, 0, 0)),
        out_shape=jax.ShapeDtypeStruct((nblk, ca, cb), F32), compiler_params=_params())(a, b)


def _cmul(ar, ai, br, bi):
    return ar * br - ai * bi, ar * bi + ai * br


def _scan(name, ar, ai, xr, xi, reverse=False, state=None, tc=512):
    L, S = xr.shape
    tc = _tile(S, tc, LANES)
    nt = L // SUBLANES
    with_da = state is not None

    def body(*refs):
        ar_ref, ai_ref, xr_ref, xi_ref = refs[:4]
        if with_da:
            sr_ref, si_ref, yr_ref, yi_ref, dar_ref, dai_ref = refs[4:]
        else:
            yr_ref, yi_ref = refs[4:]
        a1 = (ar_ref[...], ai_ref[...])
        pw = [a1]
        for _ in range(SUBLANES - 1):
            pw.append(_cmul(*pw[-1], *a1))
        row = lax.broadcasted_iota(jnp.int32, (SUBLANES, tc), 0)
        tr = jnp.zeros((SUBLANES, tc), F32)
        ti = jnp.zeros((SUBLANES, tc), F32)
        for t in range(SUBLANES):
            p = pw[SUBLANES - 1 - t] if reverse else pw[t]
            tr = jnp.where(row == t, p[0], tr)
            ti = jnp.where(row == t, p[1], ti)

        def step(n, carry):
            idx = (nt - 1 - n) if reverse else n
            rows = pl.ds(pl.multiple_of(idx * SUBLANES, SUBLANES), SUBLANES)
            vr, vi = xr_ref[rows, :], xi_ref[rows, :]
            for d in (1, 2, 4):
                pr, pi = pw[d - 1]
                if reverse:
                    qr, qi = pltpu.roll(vr, SUBLANES - d, 0), pltpu.roll(vi, SUBLANES - d, 0)
                    keep = row < SUBLANES - d
                else:
                    qr, qi = pltpu.roll(vr, d, 0), pltpu.roll(vi, d, 0)
                    keep = row >= d
                mr, mi = _cmul(pr, pi, qr, qi)
                vr = vr + jnp.where(keep, mr, 0.0)
                vi = vi + jnp.where(keep, mi, 0.0)
            cr, ci = carry[0], carry[1]
            mr, mi = _cmul(tr, ti, cr, ci)
            vr, vi = vr + mr, vi + mi
            yr_ref[rows, :] = vr
            yi_ref[rows, :] = vi
            edge = 0 if reverse else SUBLANES - 1
            new = (vr[edge:edge + 1, :], vi[edge:edge + 1, :])
            if not with_da:
                return new
            pidx = jnp.maximum(idx - 1, 0)
            prows = pl.ds(pl.multiple_of(pidx * SUBLANES, SUBLANES), SUBLANES)
            live = (idx > 0).astype(F32)
            s0r = sr_ref[prows, :][SUBLANES - 1:SUBLANES, :] * live
            s0i = si_ref[prows, :][SUBLANES - 1:SUBLANES, :] * live
            spr = jnp.where(row == 0, s0r, pltpu.roll(sr_ref[rows, :], 1, 0))
            spi = jnp.where(row == 0, s0i, pltpu.roll(si_ref[rows, :], 1, 0))
            return new + (carry[2] + vr * spr + vi * spi, carry[3] + vi * spr - vr * spi)

        zero = jnp.zeros((1, tc), F32)
        init = (zero, zero)
        if with_da:
            acc0 = jnp.zeros((SUBLANES, tc), F32)
            init = init + (acc0, acc0)
        fin = lax.fori_loop(0, nt, step, init, unroll=2)
        if with_da:
            dar_ref[...] = jnp.sum(fin[2], axis=0, keepdims=True)
            dai_ref[...] = jnp.sum(fin[3], axis=0, keepdims=True)

    col = pl.BlockSpec((L, tc), lambda j: (0, j))
    vec = pl.BlockSpec((1, tc), lambda j: (0, j))
    n_in = 6 if with_da else 4
    in_specs = [vec, vec] + [col] * (n_in - 2)
    out_specs = [col, col] + ([vec, vec] if with_da else [])
    out_shape = [jax.ShapeDtypeStruct((L, S), F32)] * 2 + ([jax.ShapeDtypeStruct((1, S), F32)] * 2 if with_da else [])
    args = (ar, ai, xr, xi) + (tuple(state) if with_da else ())
    return pl.pallas_call(body, name=name, grid=(S // tc,), in_specs=in_specs, out_specs=out_specs,
                          out_shape=out_shape, compiler_params=_params())(*args)


def _bd_build(w_gnp):
    G, N, P = w_gnp.shape
    nb = G // S5_GB
    eye = jnp.eye(S5_GB, dtype=F32)
    x = w_gnp.reshape(nb, S5_GB, N, P).transpose(0, 1, 3, 2)
    w = x[:, :, :, None, :] * eye[None, :, None, :, None]
    return w.reshape(nb, S5_GB * P, S5_GB * N)


def _bd_extract(w, G, N, P):
    nb = G // S5_GB
    eye = jnp.eye(S5_GB, dtype=F32)
    w5 = w.reshape(nb, S5_GB, P, S5_GB, N)
    d = jnp.sum(w5 * eye[None, :, None, :, None], axis=3)
    return d.transpose(0, 1, 3, 2).reshape(G, N, P)


def _s5_discretize(lam_re, lam_im, log_dt, b_re, b_im):
    dt = jnp.exp(log_dt)[:, None]
    mag = jnp.exp(lam_re * dt)
    ang = lam_im * dt
    ab_re = mag * jnp.cos(ang)
    ab_im = mag * jnp.sin(ang)
    den = lam_re * lam_re + lam_im * lam_im
    nr = ab_re - 1.0
    ni = ab_im
    q_re = (nr * lam_re + ni * lam_im) / den
    q_im = (ni * lam_re - nr * lam_im) / den
    bb_re = q_re[..., None] * b_re - q_im[..., None] * b_im
    bb_im = q_re[..., None] * b_im + q_im[..., None] * b_re
    return ab_re, ab_im, bb_re, bb_im


HBM_SPEC = pl.BlockSpec(memory_space=pl.ANY)


def _place():
    x, y, c = lax.axis_index("x"), lax.axis_index("y"), lax.axis_index("c")
    others = [(1 - x, y), (x, 1 - y), (1 - x, 1 - y)]
    return x, y, c, 2 * x + y, others


def _half(ref, h, axis):
    n = ref.shape[axis] // 2
    idx = [slice(None)] * len(ref.shape)
    idx[axis] = pl.ds(h * n, n)
    return ref.at[tuple(idx)]


def _chip():
    return 2 * lax.axis_index("x") + lax.axis_index("y")


def _cast_place(name, w, layer):
    _, R, C = w.shape
    tr = _tile(R, max(16, (1 << 19) // C), 16)

    def body(w_ref, o_ref):
        o_ref[...] = w_ref[...].astype(o_ref.dtype)

    return pl.pallas_call(
        body, name=name, grid=(R // tr,),
        in_specs=[pl.BlockSpec((None, tr, C), lambda i: (layer, i, 0))],
        out_specs=pl.BlockSpec((None, tr, C), lambda i: (_chip(), i, 0)),
        out_shape=jax.ShapeDtypeStruct((N_CHIPS, R, C), BF16), compiler_params=_params())(w)


HBM_ONLY = pl.BlockSpec(memory_space=pltpu.HBM)
SEM_SPEC = pl.BlockSpec(memory_space=pltpu.SEMAPHORE)
EFFECT = pltpu.SideEffectType.DATAFLOW_SIDE_EFFECTING


def _split_start(name, copies_fn, n_copies, arrays, after):
    n, na = len(arrays), len(after)

    def body(*refs):
        ins, send, recv, token = refs[:n], refs[n + na], refs[n + na + 1], refs[-1]
        for cp in copies_fn(ins, send, recv):
            cp.start()
        token[...] = jnp.zeros_like(token)

    n_sem = (n_copies,)
    outs = pl.pallas_call(
        body, name=name,
        out_shape=(pltpu.SemaphoreType.DMA(n_sem), pltpu.SemaphoreType.DMA(n_sem),
                   *[pltpu.HBM(a.shape, a.dtype) for a in arrays], jax.ShapeDtypeStruct((SUBLANES, LANES), F32)),
        in_specs=[HBM_ONLY] * n + [HBM_SPEC] * na,
        out_specs=(SEM_SPEC, SEM_SPEC, *[HBM_ONLY] * n, pl.BlockSpec(memory_space=pltpu.VMEM)),
        input_output_aliases={i: 2 + i for i in range(n)},
        compiler_params=pltpu.CompilerParams(has_side_effects=EFFECT),
    )(*[pltpu.with_memory_space_constraint(a, pltpu.HBM) for a in arrays], *after)
    return outs[0], outs[1], list(outs[2:2 + n]), outs[-1]


def _split_wait(name, copies_fn, send, recv, arrays, after):
    n, na = len(arrays), len(after)

    def body(*refs):
        ins, send_ref, recv_ref = refs[:n], refs[n], refs[n + 1]
        for cp in copies_fn(ins, send_ref, recv_ref):
            cp.wait_send()
            cp.wait_recv()

    outs = pl.pallas_call(
        body, name=name, out_shape=tuple(pltpu.HBM(a.shape, a.dtype) for a in arrays),
        in_specs=[HBM_ONLY] * n + [SEM_SPEC, SEM_SPEC] + [HBM_SPEC] * na, out_specs=(HBM_ONLY,) * n,
        input_output_aliases={i: i for i in range(n)},
        compiler_params=pltpu.CompilerParams(has_side_effects=EFFECT),
    )(*arrays, send, recv, *after)
    return list(outs)


def _gather_copies(T):
    def copies(bufs, send, recv):
        x, y, c, me, others = _place()
        cps = []
        for t in range(T):
            mine = _half(bufs[t].at[me], c, 0)
            for k, (ox, oy) in enumerate(others):
                cps.append(pltpu.make_async_remote_copy(
                    src_ref=mine, dst_ref=mine, send_sem=send.at[3 * t + k], recv_sem=recv.at[3 * t + k],
                    device_id=(ox, oy, c), device_id_type=MESH))
        return cps
    return copies


def _xchg_copies(T):
    def copies(arrays, send, recv):
        x, y, c, me, others = _place()
        cps = []
        for t in range(T):
            for k, (ox, oy) in enumerate(others):
                cps.append(pltpu.make_async_remote_copy(
                    src_ref=arrays[t].at[2 * ox + oy], dst_ref=arrays[T + t].at[me], send_sem=send.at[3 * t + k],
                    recv_sem=recv.at[3 * t + k], device_id=(ox, oy, c), device_id_type=MESH))
        return cps
    return copies


def _gather_pass_on(name, bufs, after):
    T = len(bufs)

    def body(*refs):
        outs = refs[T + 1:2 * T + 1]
        send, recv = refs[2 * T + 1:]
        x, y, c, me, others = _place()
        cps = []
        for t in range(T):
            for k, (ox, oy) in enumerate(others):
                got = _half(outs[t].at[2 * ox + oy], c, 0)
                cp = pltpu.make_async_remote_copy(
                    src_ref=got, dst_ref=got, send_sem=send.at[t, k], recv_sem=recv.at[t, k],
                    device_id=(x, y, 1 - c), device_id_type=MESH)
                cp.start()
                cps.append(cp)
        for cp in cps:
            cp.wait()

    return pl.pallas_call(
        body, name=name, in_specs=[HBM_SPEC] * (T + 1), out_specs=[HBM_SPEC] * T,
        out_shape=[jax.ShapeDtypeStruct(b.shape, b.dtype) for b in bufs],
        input_output_aliases={t: t for t in range(T)},
        scratch_shapes=[pltpu.SemaphoreType.DMA((T, 3)), pltpu.SemaphoreType.DMA((T, 3))],
    )(*bufs, after)


def _swap_copies(T):
    def copies(arrays, send, recv):
        x, y, c, me, others = _place()
        return [pltpu.make_async_remote_copy(
            src_ref=_half(arrays[t], 1 - c, 1), dst_ref=arrays[T + t], send_sem=send.at[t], recv_sem=recv.at[t],
            device_id=(x, y, 1 - c), device_id_type=MESH) for t in range(T)]
    return copies


def _join_copies(T):
    def copies(arrays, send, recv):
        x, y, c, me, others = _place()
        cps = []
        for t in range(T):
            mine = _half(arrays[t], c, 0)
            cps.append(pltpu.make_async_remote_copy(
                src_ref=mine, dst_ref=mine, send_sem=send.at[t], recv_sem=recv.at[t],
                device_id=(x, y, 1 - c), device_id_type=MESH))
        return cps
    return copies


def _all_reduce_small(name, buf):
    _, r, w = buf.shape

    def body(in_ref, out_ref, land, send, recv):
        x, y, c, _, _ = _place()
        me = 4 * x + 2 * y + c

        def peer(k):
            return (1 - x if k & 4 else x, 1 - y if k & 2 else y, 1 - c if k & 1 else c)

        cps = []
        for k in range(1, N_DEV):
            px, py, pc = peer(k)
            cp = pltpu.make_async_remote_copy(
                src_ref=in_ref.at[4 * px + 2 * py + pc], dst_ref=land.at[me], send_sem=send.at[k - 1],
                recv_sem=recv.at[k - 1], device_id=(px, py, pc), device_id_type=MESH)
            cp.start()
            cps.append(cp)
        land[me] = in_ref[me]
        for cp in cps:
            cp.wait()
        total = land[0]
        for d in range(1, N_DEV):
            total = total + land[d]
        out_ref[me] = total
        cps = []
        for k in range(1, N_DEV):
            cp = pltpu.make_async_remote_copy(
                src_ref=out_ref.at[me], dst_ref=out_ref.at[me], send_sem=send.at[N_DEV - 2 + k],
                recv_sem=recv.at[N_DEV - 2 + k], device_id=peer(k), device_id_type=MESH)
            cp.start()
            cps.append(cp)
        for cp in cps:
            cp.wait()

    vm = pl.BlockSpec(memory_space=pltpu.VMEM)
    return pl.pallas_call(
        body, name=name, in_specs=[vm], out_specs=vm, out_shape=jax.ShapeDtypeStruct(buf.shape, F32),
        scratch_shapes=[pltpu.VMEM(buf.shape, F32), pltpu.SemaphoreType.DMA((2 * N_DEV - 2,)),
                        pltpu.SemaphoreType.DMA((2 * N_DEV - 2,))],
        compiler_params=_params())(buf)


def _add_pairs(name, g, theirs):
    nb, H, C = theirs.shape
    th = _tile(H, max(SUBLANES * 2, (1 << 19) // C), 16)
    nh = H // th

    def body(g_ref, t_ref, p_ref, l_ref):
        s = (g_ref[...].astype(F32) + t_ref[...].astype(F32)).astype(p_ref.dtype)
        p_ref[...] = s

        @pl.when(pl.program_id(1) == _chip())
        def _():
            l_ref[...] = s

    blk = (None, th, C)
    return pl.pallas_call(
        body, name=name, grid=(nh, nb),
        in_specs=[pl.BlockSpec(blk, lambda i, j: (j, lax.axis_index("c") * nh + i, 0)),
                  pl.BlockSpec(blk, lambda i, j: (j, i, 0))],
        out_specs=[pl.BlockSpec(blk, lambda i, j: (j, i, 0)), pl.BlockSpec(blk, lambda i, j: (_chip(), i, 0))],
        out_shape=[jax.ShapeDtypeStruct(theirs.shape, theirs.dtype)] * 2, compiler_params=_params())(g, theirs)


def _sum_chips(name, parts):
    nb, H, C = parts.shape
    th = _tile(H, max(SUBLANES * 2, (1 << 19) // C), 16)
    nh = H // th

    def body(p_ref, o_ref):
        tot = p_ref[0].astype(F32)
        for b in range(1, nb):
            tot = tot + p_ref[b].astype(F32)
        o_ref[...] = tot

    return pl.pallas_call(
        body, name=name, grid=(nh,),
        in_specs=[pl.BlockSpec((nb, th, C), lambda i: (0, i, 0))],
        out_specs=pl.BlockSpec((th, C), lambda i: (lax.axis_index("c") * nh + i, 0)),
        out_shape=jax.ShapeDtypeStruct((2 * H, C), F32), compiler_params=_params())(parts)


def _adamw_math(w, g, m, v):
    m = ADAM_B1 * m + (1.0 - ADAM_B1) * g
    v = ADAM_B2 * v + (1.0 - ADAM_B2) * (g * g)
    m_hat = m / (1.0 - ADAM_B1 ** ADAM_STEP)
    v_hat = v / (1.0 - ADAM_B2 ** ADAM_STEP)
    delta = -ADAM_LR * (m_hat / (jnp.sqrt(v_hat) + ADAM_EPS) + ADAM_WD * w)
    return delta, m, v


def _adamw_big(name, w, m, v, gs):
    depth, R, C = w.shape
    tr = _tile(R, max(SUBLANES, (1 << 18) // C))

    def body(w_ref, m_ref, v_ref, *refs):
        g_refs, (go, do, mo, vo) = refs[:depth], refs[depth:]
        li = pl.program_id(0)
        g = g_refs[0][...]
        for l in range(1, depth):
            g = jnp.where(li == l, g_refs[l][...], g)
        delta, mn, vn = _adamw_math(w_ref[...], g, m_ref[...], v_ref[...])
        go[...] = g
        do[...] = delta
        mo[...] = mn
        vo[...] = vn

    spec = pl.BlockSpec((None, tr, C), lambda li, i: (li, i, 0))
    g_specs = [pl.BlockSpec((tr, C), functools.partial(lambda li, i, l: (jnp.where(li == l, i, 0), 0), l=l))
               for l in range(depth)]
    return pl.pallas_call(body, name=name, grid=(depth, R // tr), in_specs=[spec] * 3 + g_specs,
                          out_specs=[spec] * 4, out_shape=[jax.ShapeDtypeStruct(w.shape, F32)] * 4,
                          compiler_params=_params())(w, m, v, *gs)


def _adamw_small(name, w, g, m, v):
    def fn(i, n, R, B, P, N):
        return list(_adamw_math(*R)), []

    return _rowwise(name, fn, [w, g, m, v], outs=[(LANES, F32)] * 3)


def _pack(arrs, rows_mult):
    flat = jnp.concatenate([a.reshape(-1) for a in arrs])
    n = flat.shape[0]
    per = rows_mult * LANES
    pad = (-n) % per
    flat = jnp.pad(flat, (0, pad))
    return flat.reshape(-1, LANES), n


def _unpack(buf, like):
    flat = buf.reshape(-1)
    out, off = [], 0
    for a in like:
        out.append(flat[off:off + a.size].reshape(a.shape))
        off += a.size
    return out


def _ffn_fwd(tag, alpha, xf, xb, wg, wu, wd, g, b):
    def up(ps, _):
        hg, hu = ps
        return hg, hu, hg * _sigmoid(hg) * hu

    hg, hu, act = _mm_expand(f"{tag}_up", xb, [wg, wu], False, up, [], [F32, F32, BF16])

    def down(acc, ex, bc):
        y, xhat, rstd = _layer_norm(alpha * ex[0] + 0.5 * acc, bc[0], bc[1])
        return y, y, xhat, jnp.broadcast_to(rstd, (rstd.shape[0], LANES))

    D = xf.shape[1]
    yf, yb, xhat, rstd = _mm_contract(f"{tag}_down", [act], [wd], False, down, [xf], [g, b],
                                      [(D, F32), (D, BF16), (D, F32), (LANES, F32)])
    return yf, yb, (xb, hg, hu, act, xhat, rstd)


def _ln_bwd(name, dy, xhat, rstd, g, scale):
    D = dy.shape[1]

    def fn(i, n, R, B, P, N):
        d, xh, rs = R
        dr = _layer_norm_bwd(d, xh, rs[:, :1], B[0])
        return [dr, scale * dr], [jnp.sum(d * xh, axis=0, keepdims=True), jnp.sum(d, axis=0, keepdims=True)]

    return _rowwise(name, fn, [dy, xhat, rstd], bcast=[g], outs=[(D, F32), (D, BF16)], sums=[D, D])


def _ffn_bwd(tag, alpha, dy, saved, wg, wu, wd, g, on_grads=None):
    xb, hg, hu, act, xhat, rstd = saved
    dr, dfb, dg, db = _ln_bwd(f"{tag}_ln_bwd", dy, xhat, rstd, g, 0.5)

    def dact(ps, ex):
        da, hgv, huv = ps[0], ex[0], ex[1]
        sg = _sigmoid(hgv)
        return da * huv * (sg * (1.0 + hgv * (1.0 - sg))), da * (hgv * sg)

    dhg, dhu = _mm_expand(f"{tag}_dact", dfb, [wd], True, dact, [hg, hu], [BF16, BF16])
    g_wd = _wgrad_rows(f"{tag}_gwd", act, dfb, tr=1408)
    g_wg = _wgrad_cols(f"{tag}_gwg", xb, dhg)
    g_wu = _wgrad_cols(f"{tag}_gwu", xb, dhu)

    def dxin(acc, ex, bc):
        return [alpha * ex[0] + acc]

    D = dy.shape[1]
    hold = [] if on_grads is None else [on_grads((g_wg, g_wu, g_wd))]
    dx, = _mm_contract(f"{tag}_dx", [dhg, dhu], [wg, wu], True, dxin, [dr], hold, [(D, F32)])
    return dx, (g_wg, g_wu, g_wd), dg, db


def _conv_taps(v, pv, i, w):
    tm = v.shape[0]
    ext = jnp.concatenate([pv * (i > 0).astype(F32), v], axis=0)
    v1 = pltpu.roll(ext, 1, 0)[SUBLANES:SUBLANES + tm]
    v2 = pltpu.roll(ext, 2, 0)[SUBLANES:SUBLANES + tm]
    return w[0:1] * v2 + w[1:2] * v1 + w[2:3] * v, v1, v2


def _mixer_fwd(tag, alpha, xf, xb, p, w_in, w_glu, w_out):
    Dh = w_in.shape[2]
    proj, = _mm_expand(f"{tag}_proj", xb, [w_in], False, lambda ps, _: ps, [], [F32])
    u, gb, gc, h = [(proj, Dh, j) for j in range(4)]
    nblk = p["wb_re"].shape[0]
    S = nblk * p["wb_re"].shape[2]
    both = lambda ps, ex, bx: ps
    bu_re, bu_im = _bd_mm(f"{tag}_bu", [(proj, p["wb_re"]), (proj, p["wb_im"])], False, both, [], [], [F32, F32])
    s_re, s_im = _scan(f"{tag}_scan", p["a_re"], p["a_im"], bu_re, bu_im)

    def yout(ps, ex, bx):
        ys = ps[0] + ps[1] + bx[0] * ex[0]
        return ys, _gelu(ys), _gelu(ys)

    ys, yg, ygb = _bd_mm(f"{tag}_yout", [(s_re, p["wc_re"]), (s_im, p["wc_imn"])], False, yout, [proj], [p["d"]],
                         [F32, F32, BF16])

    def glu(acc, ex, bc):
        yy = ex[0] * _sigmoid(acc)
        return acc, yy * _rms_inv(yy) * bc[0]

    t, yn = _mm_contract(f"{tag}_glu", [ygb], [w_glu], False, glu, [yg], [p["g_s5"]], [(Dh, F32), (Dh, BF16)])

    def conv(i, n, R, B, P, N):
        gbv, gcv, hv = R
        cw, cb, gcn = B
        cv, _, _ = _conv_taps(gcv * hv, P[0] * P[1], i, cw)
        z = gbv * (cv + cb)
        return [z * _rms_inv(z) * gcn], []

    zn, = _rowwise(f"{tag}_conv", conv, [gb, gc, h], bcast=[p["conv_w"], p["conv_b"], p["g_conv"]],
                   outs=[(Dh, BF16)], prev=[1, 2])
    cat = jnp.concatenate([yn, zn], axis=1)

    def out(acc, ex, bc):
        y, xhat, rstd = _layer_norm(alpha * ex[0] + acc, bc[0], bc[1])
        return y, y, xhat, jnp.broadcast_to(rstd, (rstd.shape[0], LANES))

    D = xf.shape[1]
    yf, yb, xhat, rstd = _mm_contract(f"{tag}_out", [cat], [w_out], False, out, [xf], [p["ln2_g"], p["ln2_b"]],
                                      [(D, F32), (D, BF16), (D, F32), (LANES, F32)])
    return yf, yb, (xb, proj, s_re, s_im, ys, yg, ygb, t, cat, xhat, rstd)


def _mixer_bwd(tag, alpha, dy, saved, p, w_in, w_glu, w_out):
    xb, proj, s_re, s_im, ys, yg, ygb, t, cat, xhat, rstd = saved
    Dh = w_in.shape[2]
    D = dy.shape[1]
    gb, gc, h = [(proj, Dh, j) for j in range(1, 4)]
    dr, dmb, dg2, db2 = _ln_bwd(f"{tag}_ln_bwd", dy, xhat, rstd, p["ln2_g"], 1.0)
    dcat, = _mm_expand(f"{tag}_dcat", dmb, [w_out], True, lambda ps, _: ps, [], [F32])
    g_wout = _wgrad_rows(f"{tag}_gwout", cat, dmb)
    half = dcat.shape[1] // 2

    def conv_b1(i, n, R, B, P, N):
        dzn, gbv, gcv, hv = R
        cw, cb, gcn = B
        v = gcv * hv
        cv, v1, v2 = _conv_taps(v, P[0] * P[1], i, cw)
        cv = cv + cb
        z = gbv * cv
        rinv = _rms_inv(z)
        dz = _rms_bwd(dzn, z, rinv, gcn)
        dcv = dz * gbv
        col = lambda a: jnp.sum(a, axis=0, keepdims=True)
        return [dz * cv, dcv], [col(dzn * z * rinv), col(dcv), col(dcv * v2), col(dcv * v1), col(dcv * v)]

    dgb, dcv, dg_conv, dconv_b, dw0, dw1, dw2 = _rowwise(
        f"{tag}_conv_b1", conv_b1, [(dcat, half, 1), gb, gc, h], bcast=[p["conv_w"], p["conv_b"], p["g_conv"]],
        outs=[(Dh, BF16), (Dh, F32)], sums=[Dh] * 5, prev=[2, 3])

    def conv_b2(i, n, R, B, P, N):
        d, gcv, hv = R
        cw = B[0]
        tm = d.shape[0]
        ext = jnp.concatenate([d, N[0] * (i < n - 1).astype(F32)], axis=0)
        d1 = pltpu.roll(ext, tm + SUBLANES - 1, 0)[:tm]
        d2 = pltpu.roll(ext, tm + SUBLANES - 2, 0)[:tm]
        dv = cw[2:3] * d + cw[1:2] * d1 + cw[0:1] * d2
        return [dv * hv, dv * gcv], []

    dgc, dh = _rowwise(f"{tag}_conv_b2", conv_b2, [dcv, gc, h], bcast=[p["conv_w"]], outs=[(Dh, BF16)] * 2, nxt=[0])

    def glu_b(i, n, R, B, P, N):
        dyn, ygv, tv = R
        sg = _sigmoid(tv)
        yy = ygv * sg
        rinv = _rms_inv(yy)
        dyy = _rms_bwd(dyn, yy, rinv, B[0])
        return [dyy * ygv * sg * (1.0 - sg), dyy * sg], [jnp.sum(dyn * yy * rinv, axis=0, keepdims=True)]

    dtb, dyg0, dg_s5 = _rowwise(f"{tag}_glu_b", glu_b, [(dcat, half, 0), yg, t], bcast=[p["g_s5"]],
                                outs=[(Dh, BF16), (Dh, F32)], sums=[Dh])

    def dys_epi(ps, ex):
        return [(ps[0] + ex[0]) * _gelu_grad(ex[1])]

    dys, = _mm_expand(f"{tag}_dys", dtb, [w_glu], True, dys_epi, [dyg0, ys], [F32])
    g_wglu = _wgrad_rows(f"{tag}_gwglu", ygb, dtb)
    both = lambda ps, ex, bx: ps
    ds_re, ds_im = _bd_mm(f"{tag}_ds", [(dys, p["wc_re"]), (dys, p["wc_imn"])], True, both, [], [], [F32, F32])
    l_re, l_im, da_re, da_im = _scan(f"{tag}_rscan", p["a_re"], -p["a_im"], ds_re, ds_im, reverse=True,
                                     state=(s_re, s_im))

    def du_epi(ps, ex, bx):
        return [ps[0] + ps[1] + bx[0] * ex[0]]

    du, = _bd_mm(f"{tag}_du", [(l_re, p["wb_re"]), (l_im, p["wb_im"])], True, du_epi, [dys], [p["d"]], [BF16])
    nblk = p["wb_re"].shape[0]
    u = (proj, Dh, 0)
    u_arr = proj[:, :Dh]
    g_wb_re = _bd_wgrad(f"{tag}_gwb_re", u_arr, l_re, nblk)
    g_wb_im = _bd_wgrad(f"{tag}_gwb_im", u_arr, l_im, nblk)
    g_wc_re = _bd_wgrad(f"{tag}_gwc_re", s_re, dys, nblk)
    g_wc_imn = _bd_wgrad(f"{tag}_gwc_im", s_im, dys, nblk)

    def dd_fn(i, n, R, B, P, N):
        return [], [jnp.sum(R[0] * R[1], axis=0, keepdims=True)]

    dd, = _rowwise(f"{tag}_dd", dd_fn, [dys, u], sums=[Dh])

    dproj = jnp.concatenate([du, dgb, dgc, dh], axis=1)

    def dxin(acc, ex, bc):
        return [alpha * ex[0] + acc]

    dx, = _mm_contract(f"{tag}_dx", [dproj], [w_in], True, dxin, [dr], [], [(D, F32)])
    g_win = _wgrad_cols(f"{tag}_gwin", xb, dproj)
    small = dict(ln2_g=dg2, ln2_b=db2, g_conv=dg_conv, conv_b=dconv_b,
                 conv_w=jnp.concatenate([dw0, dw1, dw2], axis=0), g_s5=dg_s5, d=dd,
                 da_re=da_re, da_im=da_im, wb_re=g_wb_re, wb_im=g_wb_im, wc_re=g_wc_re, wc_imn=g_wc_imn)
    return dx, (g_win, g_wglu, g_wout), small


def _swap_start(name, grads, after):
    n = len(grads)
    lands = [lax.empty((g.shape[0], g.shape[1] // 2, g.shape[2]), g.dtype) for g in grads]
    send, recv, arrays, token = _split_start(name, _swap_copies(n), n, list(grads) + lands, after)
    return (send, recv, arrays, n), token


def _swap_finish(name, tag, state, after):
    send, recv, arrays, n = state
    arrays = _split_wait(name, _swap_copies(n), send, recv, arrays, after)
    added = [_add_pairs(f"{tag}{t}", g, b) for t, (g, b) in enumerate(zip(arrays[:n], arrays[n:]))]
    return [a[0] for a in added], [a[1] for a in added]


BIG = ["ffn1_gate", "ffn1_up", "ffn1_down", "w_in", "s5_w_glu", "w_out", "ffn2_gate", "ffn2_up", "ffn2_down"]
SMALL = ["ln1_g", "ln1_b", "s5_lam_re", "s5_lam_im", "s5_log_dt", "s5_b_re", "s5_b_im", "s5_c_re", "s5_c_im", "s5_d",
         "conv_w", "conv_b", "g_s5", "g_conv", "ln2_g", "ln2_b", "ln3_g", "ln3_b"]
WEIGHTS = ['ffn1_gate', 'ffn1_up', 'ffn1_down', 'ln1_g', 'ln1_b', 'w_in', 's5_lam_re', 's5_lam_im', 's5_log_dt',
           's5_b_re', 's5_b_im', 's5_c_re', 's5_c_im', 's5_d', 's5_w_glu', 'conv_w', 'conv_b', 'g_s5', 'g_conv',
           'w_out', 'ln2_g', 'ln2_b', 'ffn2_gate', 'ffn2_up', 'ffn2_down', 'ln3_g', 'ln3_b']


def _step(W, M, V, x, target):
    depth = W["ffn1_gate"].shape[0]
    alpha = (2.0 * depth) ** 0.25
    L, D = x.shape
    G, N = W["s5_lam_re"].shape[1:]
    P = W["s5_b_re"].shape[3]
    Dh = G * P
    chip = 2 * lax.axis_index("x") + lax.axis_index("y")
    cw_cols = W["conv_w"].shape[2]

    cw_rows = -(-depth * CONV_W // SUBLANES) * SUBLANES
    cw_buf = jnp.zeros((N_CHIPS, 2 * cw_rows, cw_cols), F32)
    cw_buf = lax.dynamic_update_slice(cw_buf, W["conv_w"].reshape(1, depth * CONV_W, cw_cols), (chip, 0, 0))

    T = len(BIG)
    gather, gather0 = _gather_copies(T), _gather_copies(T + 1)
    placed = [[_cast_place(f"place{l}_{n}", W[n], l) for n in BIG] for l in range(depth)]
    g_state = _split_start("gather0_start", gather0, 3 * (T + 1), placed[0] + [cw_buf], [x])

    s5_vjps, lp = [], []
    for l in range(depth):
        (a_re, a_im, bb_re, bb_im), vjp = jax.vjp(_s5_discretize, W["s5_lam_re"][l], W["s5_lam_im"][l],
                                                  W["s5_log_dt"][l], W["s5_b_re"][l], W["s5_b_im"][l])
        s5_vjps.append(vjp)
        row = lambda a: a.reshape(1, -1)
        lp.append(dict(
            a_re=row(a_re), a_im=row(a_im), wb_re=_bd_build(bb_re), wb_im=_bd_build(bb_im),
            wc_re=_bd_build(W["s5_c_re"][l].transpose(0, 2, 1)).transpose(0, 2, 1),
            wc_imn=-_bd_build(W["s5_c_im"][l].transpose(0, 2, 1)).transpose(0, 2, 1),
            d=row(W["s5_d"][l]), g_s5=row(W["g_s5"][l]), g_conv=row(W["g_conv"][l]), conv_b=row(W["conv_b"][l]),
            ln2_g=row(W["ln2_g"][l]), ln2_b=row(W["ln2_b"][l])))

    def cast(i, n, R, B, P_, N_):
        return [R[0]], []

    xb, = _rowwise("cast_x", cast, [x], outs=[(D, BF16)])
    xf = x
    saved, gathered = [], []
    early = [xb] + [a for p in placed[1:] for a in p] + [q[k] for q in lp for k in ("wb_re", "wb_im", "wc_re", "wc_imn")]
    for l in range(depth):
        bufs = _split_wait(f"gather{l}_wait", gather0 if l == 0 else gather, g_state[0], g_state[1], g_state[2],
                           early if l == 0 else [xb])
        token = bufs[0]
        if l + 1 < depth:
            g_state = _split_start(f"gather{l + 1}_start", gather, 3 * T, placed[l + 1], [bufs[0]])
            token = g_state[3]
        passed = _gather_pass_on(f"gather{l}_pass", bufs, token)
        if l == 0:
            cw = passed[T][:, :depth * CONV_W, :].reshape(N_CHIPS, depth, CONV_W, cw_cols)
            cw = cw.transpose(1, 2, 0, 3).reshape(depth, CONV_W, N_CHIPS * cw_cols)
            for q in range(depth):
                lp[q]["conv_w"] = cw[q]
        gathered.append(passed[:T])
        wg1, wu1, wd1, w_in, w_glu, w_out, wg2, wu2, wd2 = gathered[l]
        row = lambda a: a.reshape(1, -1)
        xf, xb, s1 = _ffn_fwd(f"l{l}_ffn1", alpha, xf, xb, wg1, wu1, wd1, row(W["ln1_g"][l]), row(W["ln1_b"][l]))
        xf, xb, s2 = _mixer_fwd(f"l{l}_mix", alpha, xf, xb, lp[l], w_in, w_glu, w_out)
        xf, xb, s3 = _ffn_fwd(f"l{l}_ffn2", alpha, xf, xb, wg2, wu2, wd2, row(W["ln3_g"][l]), row(W["ln3_b"][l]))
        saved.append((s1, s2, s3))

    def loss_fn(i, n, R, B, P_, N_):
        err = R[0] - R[1]
        return [err * (1.0 / D)], [jnp.sum(0.5 * err * err * (1.0 / D), axis=0, keepdims=True)]

    dx, loss_cols = _rowwise("loss", loss_fn, [xf, target], outs=[(D, F32)], sums=[D])
    loss = lax.psum(jnp.sum(loss_cols), ("x", "y", "c"))

    big_g = {n: [None] * depth for n in BIG}
    small_g = {n: [None] * depth for n in SMALL}
    xchg, join = _xchg_copies(T), _join_copies(T)
    x_state, tok, joins = None, 0.0, []

    def finish(layer, lands, after):
        fulls = [_sum_chips(f"l{layer}_red_sum{t}", g) for t, g in enumerate(lands)]
        joins.append((layer, _split_start(f"l{layer}_join_start", join, T, fulls, after)))

    for l in reversed(range(depth)):
        wg1, wu1, wd1, w_in, w_glu, w_out, wg2, wu2, wd2 = gathered[l]
        s1, s2, s3 = saved[l]
        row = lambda a: a.reshape(1, -1)
        dx, g2, dg3, db3 = _ffn_bwd(f"l{l}_ffn2", alpha, dx, s3, wg2, wu2, wd2, row(W["ln3_g"][l]) + tok)
        sw2, tok2 = _swap_start(f"l{l}_swap2_start", g2, [dx])
        dx, gm, sm = _mixer_bwd(f"l{l}_mix", alpha, dx, s2, dict(lp[l], ln2_g=lp[l]["ln2_g"] + tok2[0, 0]),
                                w_in, w_glu, w_out)
        swm, tokm = _swap_start(f"l{l}_swapm_start", gm, [dx])
        box = []

        def swap1(grads, l=l, box=box):
            state, token = _swap_start(f"l{l}_swap1_start", grads, [])
            box.append(state)
            return token

        dx, g1, dg1, db1 = _ffn_bwd(f"l{l}_ffn1", alpha, dx, s1, wg1, wu1, wd1, row(W["ln1_g"][l]) + tokm[0, 0], swap1)
        sw1 = box[0]
        p2, l2 = _swap_finish(f"l{l}_swap2_wait", f"l{l}_add2_", sw2, [dx])
        pm, lm = _swap_finish(f"l{l}_swapm_wait", f"l{l}_addm_", swm, [dx])
        p1, l1 = _swap_finish(f"l{l}_swap1_wait", f"l{l}_add1_", sw1, [dx])
        arrays = p1 + pm + p2 + l1 + lm + l2
        after, prev = [arrays[0]], None
        if x_state is not None:
            prev = _split_wait(f"l{l + 1}_xchg_wait", xchg, x_state[0], x_state[1], x_state[2], [arrays[0]])
            after = [prev[T]]
        x_state = _split_start(f"l{l}_xchg_start", xchg, 3 * T, arrays, after)
        tok = x_state[3][0, 0]
        if prev is not None:
            finish(l + 1, prev[T:], [x_state[3]])
            tok = tok + joins[-1][1][3][0, 0]
        d_bb_re = _bd_extract(sm["wb_re"], G, N, P)
        d_bb_im = _bd_extract(sm["wb_im"], G, N, P)
        d_lre, d_lim, d_ldt, d_bre, d_bim = s5_vjps[l]((sm["da_re"].reshape(G, N), sm["da_im"].reshape(G, N),
                                                        d_bb_re, d_bb_im))
        d_cre = _bd_extract(sm["wc_re"].transpose(0, 2, 1), G, N, P).transpose(0, 2, 1)
        d_cim = -_bd_extract(sm["wc_imn"].transpose(0, 2, 1), G, N, P).transpose(0, 2, 1)
        vals = dict(ln1_g=dg1, ln1_b=db1, s5_lam_re=d_lre, s5_lam_im=d_lim, s5_log_dt=d_ldt, s5_b_re=d_bre,
                    s5_b_im=d_bim, s5_c_re=d_cre, s5_c_im=d_cim, s5_d=sm["d"], conv_w=sm["conv_w"],
                    conv_b=sm["conv_b"], g_s5=sm["g_s5"], g_conv=sm["g_conv"], ln2_g=sm["ln2_g"], ln2_b=sm["ln2_b"],
                    ln3_g=dg3, ln3_b=db3)
        for n in SMALL:
            small_g[n][l] = vals[n].reshape((W[n].shape[1:] if n != "conv_w" else (CONV_W, N_CHIPS * cw_cols)))
    grad_x = dx

    sg = [jnp.stack(small_g[n]) for n in SMALL]
    packed, _ = _pack(sg, SUBLANES * N_DEV)
    packed = packed + tok
    packed = _all_reduce_small("small_allreduce", packed.reshape(N_DEV, -1, LANES)).reshape(-1, LANES)
    last = _split_wait("l0_xchg_wait", xchg, x_state[0], x_state[1], x_state[2], [packed])
    finish(0, last[T:], [last[T]])
    sg = dict(zip(SMALL, _unpack(packed, sg)))
    sg["conv_w"] = lax.dynamic_slice_in_dim(sg["conv_w"], chip * cw_cols, cw_cols, axis=2)
    gp, _ = _pack([sg[n] for n in SMALL], SUBLANES)
    gp = gp + joins[-1][1][3][0, 0]
    wp, _ = _pack([W[n] for n in SMALL], SUBLANES)
    mp, _ = _pack([M[n] for n in SMALL], SUBLANES)
    vp, _ = _pack([V[n] for n in SMALL], SUBLANES)
    like = [W[n] for n in SMALL]
    small_out = _adamw_small("adamw_small", wp, gp, mp, vp)
    dsm, msm, vsm = [dict(zip(SMALL, _unpack(a, like))) for a in small_out]
    for layer, (send, recv, fulls, _) in joins:
        fulls = _split_wait(f"l{layer}_join_wait", join, send, recv, fulls, [small_out[0]])
        for n, r in zip(BIG, fulls):
            big_g[n][layer] = r

    grads, deltas, new_m, new_v = dict(sg), dsm, msm, vsm
    for n in BIG:
        shp = W[n].shape
        flat = lambda a: a.reshape(shp[0], shp[1], shp[2])
        gr, de, mn, vn = _adamw_big(f"adamw_{n}", flat(W[n]), flat(M[n]), flat(V[n]), big_g[n])
        grads[n], deltas[n], new_m[n], new_v[n] = gr, de, mn, vn

    outs = [loss, grad_x[None]]
    for d in (grads, deltas, new_m, new_v):
        outs += [d[n] for n in WEIGHTS]
    return tuple(outs)


def kernel(x, ffn1_gate, ffn1_up, ffn1_down, ln1_g, ln1_b, w_in, s5_lam_re, s5_lam_im, s5_log_dt, s5_b_re, s5_b_im, s5_c_re, s5_c_im, s5_d, s5_w_glu, conv_w, conv_b, g_s5, g_conv, w_out, ln2_g, ln2_b, ffn2_gate, ffn2_up, ffn2_down, ln3_g, ln3_b, loss_target, m_ffn1_gate, m_ffn1_up, m_ffn1_down, m_ln1_g, m_ln1_b, m_w_in, m_s5_lam_re, m_s5_lam_im, m_s5_log_dt, m_s5_b_re, m_s5_b_im, m_s5_c_re, m_s5_c_im, m_s5_d, m_s5_w_glu, m_conv_w, m_conv_b, m_g_s5, m_g_conv, m_w_out, m_ln2_g, m_ln2_b, m_ffn2_gate, m_ffn2_up, m_ffn2_down, m_ln3_g, m_ln3_b, v_ffn1_gate, v_ffn1_up, v_ffn1_down, v_ln1_g, v_ln1_b, v_w_in, v_s5_lam_re, v_s5_lam_im, v_s5_log_dt, v_s5_b_re, v_s5_b_im, v_s5_c_re, v_s5_c_im, v_s5_d, v_s5_w_glu, v_conv_w, v_conv_b, v_g_s5, v_g_conv, v_w_out, v_ln2_g, v_ln2_b, v_ffn2_gate, v_ffn2_up, v_ffn2_down, v_ln3_g, v_ln3_b):
    a = dict(locals())
    W = {n: a[n] for n in WEIGHTS}
    M = {n: a["m_" + n] for n in WEIGHTS}
    V = {n: a["v_" + n] for n in WEIGHTS}
    return _step(W, M, V, x[0], loss_target[0])
```
